```python
import jax, jax.numpy as jnp
from jax import lax
import numpy as np

D_MODEL = 1024
BATCH = 8
SEQ = 2048
DEPTH = 4

CTX_LEN = 256
GRID_W = 64
MIX_WIDTH = D_MODEL
POOL_WIDTH = MIX_WIDTH // 2
POOL_GROUPS = 4
POOL_GROUP_W = POOL_WIDTH // POOL_GROUPS
POOL_WINDOWS = (2, 4, 8, 16)
GLA_WIDTH = MIX_WIDTH - POOL_WIDTH
GLA_HEADS = 4
GLA_DV = GLA_WIDTH // GLA_HEADS
GLA_DK = GLA_DV // 2
GLA_QK = GLA_HEADS * GLA_DK
GATE_RANK = 16
GATE_NORM = 16.0
CHUNK = 32
N_EXPERTS = 16
N_GROUPS = 4
EXPERTS_PER_GROUP = N_EXPERTS // N_GROUPS
TOP_K = 2
D_EXPERT = D_MODEL // 2
MOE_BLOCK = 128
N_MOD = 6
EPS = 1e-6
IN_SPLITS = (POOL_WIDTH, POOL_WIDTH + GLA_QK, POOL_WIDTH + 2 * GLA_QK,
             POOL_WIDTH + 2 * GLA_QK + GLA_WIDTH, POOL_WIDTH + 2 * GLA_QK + 2 * GLA_WIDTH)
IN_WIDTH = IN_SPLITS[-1] + 2 * GATE_RANK

kernel_name = "hybrid_pool_gla_moe_prefix_dit"


def _rmsnorm(x, gain):
    xf = x.astype(jnp.float32)
    y = xf * lax.rsqrt(jnp.mean(xf * xf, axis=-1, keepdims=True) + EPS)
    return (y * gain.astype(jnp.float32)).astype(x.dtype)


def _modulation(cvec, w_mod, b_mod):
    return jnp.split(jax.nn.silu(cvec) @ w_mod + b_mod, N_MOD, axis=-1)


def _box_mean(u, axis, w):
    L = u.shape[axis]
    idx = np.arange(L)
    lo = np.clip(idx - w // 2, 0, L)
    hi = np.clip(idx + w - w // 2, 0, L)
    pad = [(0, 0)] * u.ndim
    pad[axis] = (1, 0)
    cs = jnp.pad(jnp.cumsum(u, axis=axis), pad)
    total = jnp.take(cs, hi, axis=axis) - jnp.take(cs, lo, axis=axis)
    cnt = (hi - lo).astype(np.float32).reshape((L,) + (1,) * (u.ndim - axis - 1))
    return total / cnt


def _pool_mixer(u, w_pool, pool_scale, grid):
    B, L, _ = u.shape
    uf = u.astype(jnp.float32)
    if grid:
        rows = L // GRID_W
        ug = uf.reshape(B, rows, GRID_W, POOL_GROUPS, POOL_GROUP_W)
        means = [_box_mean(_box_mean(ug[..., g, :], 1, w), 2, w) for g, w in enumerate(POOL_WINDOWS)]
    else:
        ug = uf.reshape(B, L, POOL_GROUPS, POOL_GROUP_W)
        means = [_box_mean(ug[..., g, :], 1, w) for g, w in enumerate(POOL_WINDOWS)]
    pooled = jnp.stack(means, axis=-2).reshape(B, L, POOL_GROUPS, POOL_GROUP_W) \
        - uf.reshape(B, L, POOL_GROUPS, POOL_GROUP_W)
    mixed = jnp.einsum('blgc,gcd->blgd', pooled, w_pool)
    return (mixed.reshape(B, L, POOL_WIDTH) * pool_scale).astype(u.dtype)


def _split_proj(p, w_gk2, b_gk):
    B, L, _ = p.shape
    u, q, k, v, g, r = jnp.split(p, IN_SPLITS, axis=-1)
    heads = lambda t, d: t.reshape(B, L, GLA_HEADS, d).transpose(0, 2, 1, 3)
    q = heads(q, GLA_DK) * (GLA_DK ** -0.5)
    k = heads(k, GLA_DK)
    v = heads(v, GLA_DV)
    r = r.reshape(B, L, 2, GATE_RANK)
    logit = jnp.einsum('blnr,nrk->nblk', r, w_gk2) + b_gk[:, None, None, :]
    lg = jax.nn.log_sigmoid(logit.astype(jnp.float32)) / GATE_NORM
    lg = lg.reshape(2, B, L, GLA_HEADS, GLA_DK).transpose(0, 1, 3, 2, 4)
    return u, q, k, v, g, lg


def _gla_chunked(q, k, v, lg, s0):
    B, H, L, dk = q.shape
    dv = v.shape[-1]
    n = L // CHUNK
    q = q.reshape(B, H, n, CHUNK, dk)
    k = k.reshape(B, H, n, CHUNK, dk)
    v = v.reshape(B, H, n, CHUNK, dv)
    b = jnp.cumsum(lg.reshape(B, H, n, CHUNK, dk), axis=3)
    b_last = b[:, :, :, -1:, :]
    mask = np.tril(np.ones((CHUNK, CHUNK), dtype=bool))[:, :, None]
    decay = jnp.exp(jnp.where(mask, b[..., :, None, :] - b[..., None, :, :], -jnp.inf))
    scores = jnp.einsum('bhnid,bhnjd,bhnijd->bhnij', q, k, decay)
    o_intra = jnp.einsum('bhnij,bhnje->bhnie', scores, v)
    kv = jnp.einsum('bhncd,bhnce->bhnde', k * jnp.exp(b_last - b), v)
    chunk_decay = jnp.exp(b_last[:, :, :, 0, :])

    def step(S, inp):
        a, kvn = inp
        return a[..., None] * S + kvn, S

    s_final, s_prev = lax.scan(step, s0, (jnp.moveaxis(chunk_decay, 2, 0), jnp.moveaxis(kv, 2, 0)))
    s_prev = jnp.moveaxis(s_prev, 0, 2)
    o_inter = jnp.einsum('bhncd,bhnde->bhnce', q * jnp.exp(b), s_prev)
    return (o_intra + o_inter).reshape(B, H, L, dv), s_final


def _mix_out(u, o, g, grid, w_pool, pool_scale, gla_gain, w_out):
    B, L, _ = u.shape
    pool = _pool_mixer(u, w_pool, pool_scale, grid)
    o = _rmsnorm(o, gla_gain).transpose(0, 2, 1, 3).reshape(B, L, GLA_WIDTH)
    gla = (o * jax.nn.silu(g)).astype(u.dtype)
    return jnp.concatenate([pool, gla], axis=-1) @ w_out


def _mixer(hx, hc, w_in, w_gk2, b_gk, w_pool, pool_scale, gla_gain, w_out, need_ctx):
    ux, qx, kx, vx, gx, lgx = _split_proj(hx @ w_in, w_gk2, b_gk)
    uc, qc, kc, vc, gc, lgc = _split_proj(hc @ w_in, w_gk2, b_gk)
    B = hx.shape[0]
    zero = jnp.zeros((B, GLA_HEADS, GLA_DK, GLA_DV), jnp.float32)
    flip = lambda t: jnp.flip(t, axis=2)
    oc_f, s_f = _gla_chunked(qc, kc, vc, lgc[0], zero)
    ox_f, _ = _gla_chunked(qx, kx, vx, lgx[0], s_f)
    oc_b, s_b = _gla_chunked(flip(qc), flip(kc), flip(vc), flip(lgc[1]), zero)
    ox_b, _ = _gla_chunked(flip(qx), flip(kx), flip(vx), flip(lgx[1]), s_b)
    yx = _mix_out(ux, ox_f + flip(ox_b), gx, True, w_pool, pool_scale, gla_gain, w_out)
    yc = _mix_out(uc, oc_f + flip(oc_b), gc, False, w_pool, pool_scale, gla_gain, w_out) if need_ctx else None
    return yx, yc


def _moe(xf, w_router, router_bias, w1, w3, w2):
    T, D = xf.shape
    s = jax.nn.sigmoid((xf @ w_router).astype(jnp.float32))
    sb = (s + router_bias).reshape(T, N_GROUPS, EXPERTS_PER_GROUP)
    group_score = lax.top_k(sb, 2)[0].sum(-1)
    gsel = jnp.argmax(group_score, axis=-1)
    in_group = jnp.take_along_axis(sb, gsel[:, None, None], axis=1)[:, 0]
    _, loc = lax.top_k(in_group, TOP_K)
    eidx = gsel[:, None] * EXPERTS_PER_GROUP + loc
    wts = jnp.take_along_axis(s, eidx, axis=1)
    wts = wts / jnp.sum(wts, axis=-1, keepdims=True)
    flat_e = eidx.reshape(-1)
    order = jnp.argsort(flat_e)
    se = flat_e[order]
    tok = order // TOP_K
    counts = jnp.bincount(flat_e, length=N_EXPERTS)
    padded = (counts + MOE_BLOCK - 1) // MOE_BLOCK * MOE_BLOCK
    start = jnp.cumsum(counts) - counts
    pend = jnp.cumsum(padded)
    pstart = pend - padded
    dest = pstart[se] + (jnp.arange(T * TOP_K) - start[se])
    n_blocks = (T * TOP_K + N_EXPERTS * (MOE_BLOCK - 1) + MOE_BLOCK - 1) // MOE_BLOCK
    xb = jnp.zeros((n_blocks * MOE_BLOCK, D), xf.dtype).at[dest].set(xf[tok])
    block_e = jnp.clip(jnp.searchsorted(pend, jnp.arange(n_blocks) * MOE_BLOCK, side='right'), 0, N_EXPERTS - 1)

    def expert_block(args):
        xblk, e = args
        return (jax.nn.silu(xblk @ w1[e]) * (xblk @ w3[e])) @ w2[e]

    yb = lax.map(expert_block, (xb.reshape(n_blocks, MOE_BLOCK, D), block_e)).reshape(-1, D)
    y = (yb[dest] * wts.reshape(-1)[order][:, None]).astype(xf.dtype)
    return jnp.zeros_like(xf).at[tok].add(y)


def setup_inputs(seed: int = 0) -> dict:
    key = jax.random.key(seed)
    ks = jax.random.split(key, 22)
    nrm = lambda kk, shape, scale: scale * jax.random.normal(kk, shape, jnp.float32)
    return {
        "x": nrm(ks[0], (BATCH, SEQ, D_MODEL), 1.0),
        "c": nrm(ks[1], (BATCH, D_MODEL), 1.0),
        "ctx": nrm(ks[2], (BATCH, CTX_LEN, D_MODEL), 1.0),
        "c_ctx": nrm(ks[3], (D_MODEL,), 1.0),
        "w_mod": nrm(ks[4], (DEPTH, D_MODEL, N_MOD * D_MODEL), 0.5 * D_MODEL ** -0.5),
        "b_mod": nrm(ks[5], (DEPTH, N_MOD * D_MODEL), 0.02),
        "norm1": 1.0 + nrm(ks[6], (DEPTH, D_MODEL), 0.05),
        "norm2": 1.0 + nrm(ks[7], (DEPTH, D_MODEL), 0.05),
        "w_in": nrm(ks[8], (DEPTH, D_MODEL, IN_WIDTH), D_MODEL ** -0.5),
        "w_gk2": nrm(ks[9], (DEPTH, 2, GATE_RANK, GLA_QK), GATE_RANK ** -0.5),
        "b_gk": nrm(ks[10], (DEPTH, 2, GLA_QK), 0.5),
        "w_pool": nrm(ks[11], (DEPTH, POOL_GROUPS, POOL_GROUP_W, POOL_GROUP_W), POOL_GROUP_W ** -0.5),
        "pool_scale": 1.0 + nrm(ks[12], (DEPTH, POOL_WIDTH), 0.1),
        "gla_gain": 1.0 + nrm(ks[13], (DEPTH, GLA_DV), 0.05),
        "w_out": nrm(ks[14], (DEPTH, MIX_WIDTH, D_MODEL), MIX_WIDTH ** -0.5),
        "w_router": nrm(ks[15], (D_MODEL, N_EXPERTS), D_MODEL ** -0.5),
        "router_bias": nrm(ks[16], (N_EXPERTS,), 0.01),
        "w1": nrm(ks[17], (DEPTH, N_EXPERTS, D_MODEL, D_EXPERT), D_MODEL ** -0.5),
        "w3": nrm(ks[18], (DEPTH, N_EXPERTS, D_MODEL, D_EXPERT), D_MODEL ** -0.5),
        "w2": nrm(ks[19], (DEPTH, N_EXPERTS, D_EXPERT, D_MODEL), D_EXPERT ** -0.5),
        "norm_f": 1.0 + nrm(ks[20], (D_MODEL,), 0.05),
    }


def reference(x, c, ctx, c_ctx, w_mod, b_mod, norm1, norm2, w_in, w_gk2, b_gk, w_pool, pool_scale,
              gla_gain, w_out, w_router, router_bias, w1, w3, w2, norm_f):
    B, S, D = x.shape
    for l in range(DEPTH):
        keep_ctx = l < DEPTH - 1
        sx1, cx1, gx1, sx2, cx2, gx2 = [m[:, None, :] for m in _modulation(c, w_mod[l], b_mod[l])]
        sc1, cc1, gc1, sc2, cc2, gc2 = _modulation(c_ctx, w_mod[l], b_mod[l])
        hx = _rmsnorm(x, norm1[l]) * (1.0 + cx1) + sx1
        hc = _rmsnorm(ctx, norm1[l]) * (1.0 + cc1) + sc1
        yx, yc = _mixer(hx, hc, w_in[l], w_gk2[l], b_gk[l], w_pool[l], pool_scale[l], gla_gain[l], w_out[l],
                        keep_ctx)
        x = x + gx1 * yx
        hx = _rmsnorm(x, norm2[l]) * (1.0 + cx2) + sx2
        if keep_ctx:
            ctx = ctx + gc1 * yc
            hc = _rmsnorm(ctx, norm2[l]) * (1.0 + cc2) + sc2
            tokens = jnp.concatenate([hx.reshape(-1, D), hc.reshape(-1, D)], axis=0)
            out = _moe(tokens, w_router, router_bias, w1[l], w3[l], w2[l])
            x = x + gx2 * out[:B * S].reshape(x.shape)
            ctx = ctx + gc2 * out[B * S:].reshape(ctx.shape)
        else:
            x = x + gx2 * _moe(hx.reshape(-1, D), w_router, router_bias, w1[l], w3[l], w2[l]).reshape(x.shape)
    return _rmsnorm(x, norm_f)
```

```python
import functools

import numpy as np
import jax
import jax.numpy as jnp
from jax import lax
from jax.experimental import pallas as pl
from jax.experimental.pallas import tpu as pltpu

F32 = jnp.float32
BF16 = jnp.bfloat16

GRID_W = 64
POOL_GROUPS = 4
POOL_WINDOWS = (2, 4, 8, 16)
GLA_HEADS = 4
GLA_DK = 64
GLA_DV = 128
GLA_QK = GLA_HEADS * GLA_DK
GATE_RANK = 16
GATE_NORM = 16.0
N_EXPERTS = 16
N_GROUPS = 4
EXPERTS_PER_GROUP = 4
EPS = 1e-6
N_MOD = 6

VMEM_LIMIT_BYTES = 56 * 1024 * 1024
SPAN = 256
SUB = 32
N_SUB = SPAN // SUB
EXP_CLAMP = 80.0
DIAG_W = 128
GLA_KW = DIAG_W + N_SUB * GLA_DK
MOE_BM = 256


def _cparams(n_axes):
    return pltpu.CompilerParams(
        dimension_semantics=("arbitrary",) * n_axes, vmem_limit_bytes=VMEM_LIMIT_BYTES)


def _split(a):
    hi = a.astype(BF16)
    lo = (a - hi.astype(F32)).astype(BF16)
    return hi, lo


def _dot(a, b):
    return jnp.dot(a, b, preferred_element_type=F32)


def _dot_nt(a, b):
    return lax.dot_general(a, b, (((1,), (1,)), ((), ())), preferred_element_type=F32)


def _dot_tn(a, b):
    return lax.dot_general(a, b, (((0,), (0,)), ((), ())), preferred_element_type=F32)


def _dot3(a, b):
    ah, al = _split(a)
    bh, bl = _split(b)
    return _dot(ah, bh) + (_dot(ah, bl) + _dot(al, bh))


def _dot3_nt(a, b):
    ah, al = _split(a)
    bh, bl = _split(b)
    return _dot_nt(ah, bh) + (_dot_nt(ah, bl) + _dot_nt(al, bh))


def _dot_exact_lhs(m_bf16, b):
    b0 = b.astype(BF16)
    r1 = b - b0.astype(F32)
    b1 = r1.astype(BF16)
    b2 = (r1 - b1.astype(F32)).astype(BF16)
    return _dot(m_bf16, b0) + (_dot(m_bf16, b1) + _dot(m_bf16, b2))


def _silu(x):
    return x * jax.nn.sigmoid(x)


def _mod_kernel(c_ref, w_ref, b_ref, o_ref):
    o_ref[0] = _dot3(_silu(c_ref[...]), w_ref[0]) + b_ref[0]


def _modulation(cvecs, w_mod, b_mod):
    depth, d, _ = w_mod.shape
    rows = cvecs.shape[0]
    return pl.pallas_call(
        _mod_kernel,
        grid=(depth, N_MOD),
        in_specs=[
            pl.BlockSpec((rows, d), lambda l, j: (0, 0)),
            pl.BlockSpec((1, d, d), lambda l, j: (l, 0, j)),
            pl.BlockSpec((1, 1, d), lambda l, j: (l, 0, j)),
        ],
        out_specs=pl.BlockSpec((1, rows, d), lambda l, j: (l, 0, j)),
        out_shape=jax.ShapeDtypeStruct((depth, rows, N_MOD * d), F32),
        compiler_params=_cparams(2),
        name="adaln_vectors",
    )(cvecs, w_mod, b_mod.reshape(depth, 1, N_MOD * d))


def _norm_mod(x, gain, scale, shift):
    y = x * lax.rsqrt(jnp.mean(x * x, axis=-1, keepdims=True) + EPS)
    return (y * gain) * (1.0 + scale) + shift


def _inproj_kernel(x_ref, scale_ref, shift_ref, gain_ref, w_ref, wr_ref, w2_ref, bgk_ref,
                   p_ref, lg_ref, *, n_chunk):
    h = _norm_mod(x_ref[...], gain_ref[...], scale_ref[...], shift_ref[...])
    hb = h.astype(BF16)
    width = p_ref.shape[1] // n_chunk
    for n in range(n_chunk):
        cols = slice(n * width, (n + 1) * width)
        p_ref[:, cols] = _dot(hb, w_ref[:, cols]).astype(BF16)
    r = _dot(hb, wr_ref[...])
    z = _dot3(r, w2_ref[...]) + bgk_ref[...]
    log_sig = jnp.minimum(z, 0.0) - jnp.log(1.0 + jnp.exp(-jnp.abs(z)))
    lg_ref[...] = log_sig / GATE_NORM


def _inproj(xf, modl, gain, w_main, w_r, w2, bgk, *, tile, mod_row):
    t, d = xf.shape
    n_main = w_main.shape[1]
    n_gate = w2.shape[1]
    mod_spec = lambda m: pl.BlockSpec((None, None, 1, d), lambda i: (mod_row(i), m, 0, 0))
    full = lambda a: pl.BlockSpec(a.shape, lambda i: (0,) * a.ndim)
    return pl.pallas_call(
        functools.partial(_inproj_kernel, n_chunk=4),
        grid=(t // tile,),
        in_specs=[
            pl.BlockSpec((tile, d), lambda i: (i, 0)),
            mod_spec(1), mod_spec(0), full(gain), full(w_main), full(w_r), full(w2), full(bgk),
        ],
        out_specs=[
            pl.BlockSpec((tile, n_main), lambda i: (i, 0)),
            pl.BlockSpec((tile, n_gate), lambda i: (i, 0)),
        ],
        out_shape=[
            jax.ShapeDtypeStruct((t, n_main), BF16),
            jax.ShapeDtypeStruct((t, n_gate), F32),
        ],
        compiler_params=_cparams(1),
        name="norm_inproj",
    )(xf, modl, modl, gain, w_main, w_r, w2, bgk)


def _store_heads(dst_ref, rows, col0, val):
    half = (col0 // GLA_DK) & 1
    plain = val.astype(BF16)
    turned = pltpu.roll(val, GLA_DK, axis=1).astype(BF16)
    for h in range(GLA_HEADS):
        if (h & 1) == half:
            src = plain[:, h * GLA_DK:(h + 1) * GLA_DK]
        else:
            hh = (h + 1) % GLA_HEADS
            src = turned[:, hh * GLA_DK:(hh + 1) * GLA_DK]
        dst_ref[h, rows, col0:col0 + GLA_DK] = src


def _gla_span(q_ref, k_ref, v_ref, lg_ref, o_ref, row0, st_ref, qh_ref, kh_ref, b_ref, *, rev, first_dir):
    rows = pl.ds(row0, SPAN)
    dcol = slice(rev * GLA_QK, (rev + 1) * GLA_QK)
    lg = lg_ref[rows, dcol]
    q = q_ref[rows, :].astype(F32) * (GLA_DK ** -0.5)
    k = k_ref[rows, :].astype(F32)
    ri = lax.broadcasted_iota(jnp.int32, (SPAN, SPAN), 0)
    ci = lax.broadcasted_iota(jnp.int32, (SPAN, SPAN), 1)
    causal = (ci >= ri) if rev else (ci <= ri)
    tri = jnp.where(causal, 1.0, 0.0).astype(BF16)
    b_ref[...] = _dot_exact_lhs(tri, lg)
    bc = b_ref[...]
    e_tot = b_ref[0:1, :] if rev else b_ref[SPAN - 1:SPAN, :]

    for s in range(N_SUB - 1):
        j = s + 1 if rev else s
        ref_row = j * SUB if rev else j * SUB + SUB - 1
        e_j = b_ref[ref_row:ref_row + 1, :]
        blk = slice(j * SUB, (j + 1) * SUB)
        qrows = slice(0, j * SUB) if rev else slice((j + 1) * SUB, SPAN)
        col0 = DIAG_W + s * GLA_DK
        _store_heads(qh_ref, qrows, col0, q[qrows] * jnp.exp(jnp.minimum(bc[qrows] - e_j, 0.0)))
        _store_heads(kh_ref, blk, col0, k[blk] * jnp.exp(jnp.minimum(e_j - bc[blk], 0.0)))
    for i in range(N_SUB):
        blk = slice(i * SUB, (i + 1) * SUB)
        mid = i * SUB + SUB // 2
        e_m = b_ref[mid:mid + 1, :]
        _store_heads(qh_ref, blk, 0, q[blk] * jnp.exp(jnp.minimum(bc[blk] - e_m, EXP_CLAMP)))
        _store_heads(kh_ref, blk, 0, k[blk] * jnp.exp(jnp.minimum(e_m - bc[blk], EXP_CLAMP)))
    same_blk = (ri >> (SUB.bit_length() - 1)) == (ci >> (SUB.bit_length() - 1))

    st = st_ref[...]
    qd = (q * jnp.exp(bc)).astype(BF16)
    o_state = _dot_nt(qd, st.astype(BF16))
    for h in range(GLA_HEADS):
        vc = slice(h * GLA_DV, (h + 1) * GLA_DV)
        a_diag = _dot_nt(qh_ref[h, :, 0:DIAG_W], kh_ref[h, :, 0:DIAG_W])
        a_cross = _dot_nt(qh_ref[h, :, DIAG_W:], kh_ref[h, :, DIAG_W:])
        a = (a_cross + jnp.where(same_blk, jnp.where(causal, a_diag, 0.0), 0.0)).astype(BF16)
        oh = _dot(a, v_ref[rows, vc]) + o_state[:, vc]
        if first_dir:
            o_ref[rows, vc] = oh
        else:
            o_ref[rows, vc] += oh
    kd = (k * jnp.exp(e_tot - bc)).astype(BF16)
    upd = _dot_tn(v_ref[rows, :], kd)
    sr = lax.broadcasted_iota(jnp.int32, upd.shape, 0) >> (GLA_DV.bit_length() - 1)
    sc = lax.broadcasted_iota(jnp.int32, upd.shape, 1) >> (GLA_DK.bit_length() - 1)
    st_ref[...] = st * jnp.exp(e_tot) + jnp.where(sr == sc, upd, 0.0)


def _gla_finish(o_ref, g_ref, gain, out_ref, n_rows):
    def body(c, carry):
        rows = pl.ds(pl.multiple_of(c * SPAN, SPAN), SPAN)
        for h in range(GLA_HEADS):
            vc = slice(h * GLA_DV, (h + 1) * GLA_DV)
            o = o_ref[rows, vc]
            y = o * lax.rsqrt(jnp.mean(o * o, axis=-1, keepdims=True) + EPS) * gain
            g = g_ref[rows, vc].astype(F32)
            out_ref[rows, vc] = (y * _silu(g)).astype(BF16)
        return carry
    lax.fori_loop(0, n_rows // SPAN, body, 0)


def _gla_kernel(qc, kc, vc, gc, lgc, qx, kx, vx, gx, lgx, gain_ref, outc, outx,
                oc, ox, st, qh, kh, bsc):
    n_c = qc.shape[0] // SPAN
    n_x = qx.shape[0] // SPAN
    for rev in (0, 1):
        qh[...] = jnp.zeros_like(qh)
        kh[...] = jnp.zeros_like(kh)
        st[...] = jnp.zeros_like(st)
        span = functools.partial(_gla_span, st_ref=st, qh_ref=qh, kh_ref=kh, b_ref=bsc,
                                 rev=rev, first_dir=(rev == 0))
        for s in range(n_c):
            span(qc, kc, vc, lgc, oc, ((n_c - 1 - s) if rev else s) * SPAN)

        def body(s, carry):
            idx = (n_x - 1 - s) if rev else s
            span(qx, kx, vx, lgx, ox, pl.multiple_of(idx * SPAN, SPAN))
            return carry
        lax.fori_loop(0, n_x, body, 0)
    gain = gain_ref[...]
    _gla_finish(oc, gc, gain, outc, qc.shape[0])
    _gla_finish(ox, gx, gain, outx, qx.shape[0])


def _gla(pbf, lg, gla_gain, *, batch, ctx_len, seq):
    nxb = (batch * ctx_len) // seq
    assert nxb * seq == batch * ctx_len
    hv = GLA_HEADS * GLA_DV
    cspec = lambda w, j: pl.BlockSpec((ctx_len, w), lambda b: (b, j))
    xspec = lambda w, j: pl.BlockSpec((seq, w), lambda b: (nxb + b, j))
    return pl.pallas_call(
        _gla_kernel,
        grid=(batch,),
        in_specs=[
            cspec(GLA_QK, 2), cspec(GLA_QK, 3), cspec(hv, 2), cspec(hv, 3), cspec(2 * GLA_QK, 0),
            xspec(GLA_QK, 2), xspec(GLA_QK, 3), xspec(hv, 2), xspec(hv, 3), xspec(2 * GLA_QK, 0),
            pl.BlockSpec((1, GLA_DV), lambda b: (0, 0)),
        ],
        out_specs=[
            pl.BlockSpec((ctx_len, hv), lambda b: (b, 0)),
            pl.BlockSpec((seq, hv), lambda b: (b, 0)),
        ],
        out_shape=[
            jax.ShapeDtypeStruct((batch * ctx_len, hv), BF16),
            jax.ShapeDtypeStruct((batch * seq, hv), BF16),
        ],
        scratch_shapes=[
            pltpu.VMEM((ctx_len, hv), F32),
            pltpu.VMEM((seq, hv), F32),
            pltpu.VMEM((hv, GLA_QK), F32),
            pltpu.VMEM((GLA_HEADS, SPAN, GLA_KW), BF16),
            pltpu.VMEM((GLA_HEADS, SPAN, GLA_KW), BF16),
            pltpu.VMEM((SPAN, GLA_QK), F32),
        ],
        compiler_params=_cparams(1),
        name="gla_bidirectional",
    )(pbf, pbf, pbf, pbf, lg, pbf, pbf, pbf, pbf, lg, gla_gain.reshape(1, GLA_DV))


def _window_offsets(w):
    return range(-(w // 2), w - w // 2)


def _band(w, period):
    ri = lax.broadcasted_iota(jnp.int32, (SPAN, SPAN), 0)
    ci = lax.broadcasted_iota(jnp.int32, (SPAN, SPAN), 1)
    d = ci - ri
    shift = period.bit_length() - 1
    assert period == 1 << shift
    same_row = (ri >> shift) == (ci >> shift)
    inside = jnp.where(d >= -(w // 2), jnp.where(d <= w - w // 2 - 1, 1.0, 0.0), 0.0)
    return jnp.where(same_row, inside, 0.0).astype(BF16)


def _pool_kernel(uc_ref, ux_ref, invc_ref, invx_ref, wp_ref, ps_ref, outc_ref, outx_ref, pad_ref,
                 *, pad_rows):
    n_x = ux_ref.shape[0]
    gw = wp_ref.shape[1]
    pad_ref[0:pad_rows, :] = jnp.zeros((pad_rows, gw), F32)
    pad_ref[pad_rows + n_x:, :] = jnp.zeros((pad_rows, gw), F32)
    for g, w in enumerate(POOL_WINDOWS):
        gc = slice(g * gw, (g + 1) * gw)
        wp = wp_ref[g]
        ps = ps_ref[:, gc]
        band_c = _band(w, SPAN)
        for t in range(uc_ref.shape[0] // SPAN):
            rows = slice(t * SPAN, (t + 1) * SPAN)
            u = uc_ref[rows, gc].astype(F32)
            pooled = _dot_exact_lhs(band_c, u) * invc_ref[g, rows, :] - u
            outc_ref[rows, gc] = (_dot(pooled.astype(BF16), wp) * ps).astype(BF16)
        pad_ref[pad_rows:pad_rows + n_x, :] = ux_ref[:, gc].astype(F32)
        band_x = _band(w, GRID_W)

        def body(t, carry):
            r0 = pl.multiple_of(t * SPAN, SPAN)
            acc = jnp.zeros((SPAN, gw), F32)
            for dr in _window_offsets(w):
                acc = acc + pad_ref[pl.ds(pad_rows + r0 + dr * GRID_W, SPAN), :]
            rows = pl.ds(r0, SPAN)
            u = pad_ref[pl.ds(pad_rows + r0, SPAN), :]
            pooled = _dot_exact_lhs(band_x, acc) * invx_ref[g, rows, :] - u
            outx_ref[rows, gc] = (_dot(pooled.astype(BF16), wp) * ps).astype(BF16)
            return carry
        lax.fori_loop(0, n_x // SPAN, body, 0)


def _inv_counts(length, rows_of):
    out = []
    for w in POOL_WINDOWS:
        cnt = np.ones((length,), np.float64)
        for axis_len, coord in rows_of(length):
            lo = np.clip(coord - w // 2, 0, axis_len)
            hi = np.clip(coord + w - w // 2, 0, axis_len)
            cnt = cnt * (hi - lo)
        out.append(1.0 / cnt)
    return np.broadcast_to(np.stack(out)[:, :, None], (len(POOL_WINDOWS), length, 128)).astype(np.float32)


def _pool(pbf, w_pool, pool_scale, *, batch, ctx_len, seq):
    nxb = (batch * ctx_len) // seq
    pw = w_pool.shape[0] * w_pool.shape[1]
    pad_rows = (max(POOL_WINDOWS) // 2) * GRID_W
    inv_c = jnp.asarray(_inv_counts(ctx_len, lambda n: [(n, np.arange(n))]))
    inv_x = jnp.asarray(_inv_counts(
        seq, lambda n: [(n // GRID_W, np.arange(n) // GRID_W), (GRID_W, np.arange(n) % GRID_W)]))
    full = lambda a: pl.BlockSpec(a.shape, lambda b: (0,) * a.ndim)
    return pl.pallas_call(
        functools.partial(_pool_kernel, pad_rows=pad_rows),
        grid=(batch,),
        in_specs=[
            pl.BlockSpec((ctx_len, pw), lambda b: (b, 0)),
            pl.BlockSpec((seq, pw), lambda b: (nxb + b, 0)),
            full(inv_c), full(inv_x), full(w_pool),
            pl.BlockSpec((1, pw), lambda b: (0, 0)),
        ],
        out_specs=[
            pl.BlockSpec((ctx_len, pw), lambda b: (b, 0)),
            pl.BlockSpec((seq, pw), lambda b: (b, 0)),
        ],
        out_shape=[
            jax.ShapeDtypeStruct((batch * ctx_len, pw), BF16),
            jax.ShapeDtypeStruct((batch * seq, pw), BF16),
        ],
        scratch_shapes=[pltpu.VMEM((seq + 2 * pad_rows, w_pool.shape[1]), F32)],
        compiler_params=_cparams(1),
        name="pool_mixer",
    )(pbf, pbf, inv_c, inv_x, w_pool, pool_scale.reshape(1, pw))


def _route(sb, s):
    def rank_in_group(vals):
        ranks = []
        for i, vi in enumerate(vals):
            r = jnp.zeros(vi.shape, jnp.int32)
            for j, vj in enumerate(vals):
                if j == i:
                    continue
                ahead = (vj >= vi) if j < i else (vj > vi)
                r = r + jnp.where(ahead, 1, 0)
            ranks.append(r)
        return ranks

    best = None
    for g in range(N_GROUPS):
        vals = sb[g * EXPERTS_PER_GROUP:(g + 1) * EXPERTS_PER_GROUP]
        svals = s[g * EXPERTS_PER_GROUP:(g + 1) * EXPERTS_PER_GROUP]
        score = None
        for i in range(EXPERTS_PER_GROUP):
            for j in range(i + 1, EXPERTS_PER_GROUP):
                pair = vals[i] + vals[j]
                score = pair if score is None else jnp.maximum(score, pair)
        ranks = rank_in_group(vals)
        e1 = jnp.zeros(score.shape, jnp.int32)
        e2 = jnp.zeros(score.shape, jnp.int32)
        w1 = jnp.zeros(score.shape, F32)
        w2 = jnp.zeros(score.shape, F32)
        for i in range(EXPERTS_PER_GROUP):
            e1 = jnp.where(ranks[i] == 0, g * EXPERTS_PER_GROUP + i, e1)
            e2 = jnp.where(ranks[i] == 1, g * EXPERTS_PER_GROUP + i, e2)
            w1 = jnp.where(ranks[i] == 0, svals[i], w1)
            w2 = jnp.where(ranks[i] == 1, svals[i], w2)
        cand = (score, e1, e2, w1, w2)
        if best is None:
            best = cand
        else:
            take = cand[0] > best[0]
            best = tuple(jnp.where(take, c, b) for c, b in zip(cand, best))
    _, e1, e2, w1, w2 = best
    tot = w1 + w2
    return e1, e2, w1 / tot, w2 / tot


def _outproj_kernel(x_ref, pc_ref, px_ref, ac_ref, ax_ref, gate_ref, scale_ref, shift_ref, gain_ref,
                    wo_ref, wrt_ref, rb_ref, x1_ref, h2_ref, e_ref, wt_ref, *, n_ctx_tiles, half):
    is_ctx = pl.program_id(0) < n_ctx_tiles
    pool = jnp.where(is_ctx, pc_ref[...], px_ref[...])
    att = jnp.where(is_ctx, ac_ref[...], ax_ref[...])
    y = _dot(pool, wo_ref[0:half, :]) + _dot(att, wo_ref[half:, :])
    x1 = x_ref[...] + gate_ref[...] * y
    x1_ref[...] = x1
    h2 = _norm_mod(x1, gain_ref[...], scale_ref[...], shift_ref[...])
    h2_ref[...] = h2
    logits = _dot3_nt(wrt_ref[...], h2)
    s = jax.nn.sigmoid(logits)
    sb = s + rb_ref[...]
    rows = lambda a: [a[i:i + 1, :] for i in range(N_EXPERTS)]
    e1, e2, w1, w2 = _route(rows(sb), rows(s))
    e_ref[0:1, :] = e1
    e_ref[1:2, :] = e2
    wt_ref[0:1, :] = w1
    wt_ref[1:2, :] = w2


def _outproj(xf, pool_c, pool_x, att_c, att_x, modl, gain2, w_out, w_router_t, router_bias,
             *, tile, mod_row, first_tile, n_tiles):
    t, d = xf.shape
    half = pool_c.shape[1]
    n_ctx_tiles = pool_c.shape[0] // tile
    cidx = lambda i: (jnp.minimum(i + first_tile, n_ctx_tiles - 1), 0)
    xidx = lambda i: (jnp.maximum(i + first_tile - n_ctx_tiles, 0), 0)
    mod_spec = lambda m: pl.BlockSpec((None, None, 1, d), lambda i: (mod_row(i + first_tile), m, 0, 0))
    full = lambda a: pl.BlockSpec(a.shape, lambda i: (0,) * a.ndim)
    tok = lambda w: pl.BlockSpec((tile, w), lambda i: (i, 0))
    lane = pl.BlockSpec((2, tile), lambda i: (0, i))
    t_out = n_tiles * tile
    return pl.pallas_call(
        functools.partial(_outproj_kernel, n_ctx_tiles=n_ctx_tiles - first_tile, half=half),
        grid=(n_tiles,),
        in_specs=[
            pl.BlockSpec((tile, d), lambda i: (i + first_tile, 0)),
            pl.BlockSpec((tile, half), cidx), pl.BlockSpec((tile, half), xidx),
            pl.BlockSpec((tile, half), cidx), pl.BlockSpec((tile, half), xidx),
            mod_spec(2), mod_spec(4), mod_spec(3), full(gain2), full(w_out), full(w_router_t),
            full(router_bias),
        ],
        out_specs=[tok(d), tok(d), lane, lane],
        out_shape=[
            jax.ShapeDtypeStruct((t_out, d), F32),
            jax.ShapeDtypeStruct((t_out, d), F32),
            jax.ShapeDtypeStruct((2, t_out), jnp.int32),
            jax.ShapeDtypeStruct((2, t_out), F32),
        ],
        compiler_params=_cparams(1),
        name="outproj_router",
    )(xf, pool_c, pool_x, att_c, att_x, modl, modl, modl, gain2, w_out, w_router_t, router_bias)


def _row_copies(n, make):
    def body(r, carry):
        make(r).start()
        return carry
    lax.fori_loop(0, n, body, 0)


def _dispatch_kernel(dest_ref, h_ref, xb_in, xb_ref, sem, *, tile):
    del xb_in
    def copy(k):
        return lambda r: pltpu.make_async_copy(
            h_ref.at[pl.ds(r, 1), :], xb_ref.at[pl.ds(dest_ref[k * tile + r], 1), :], sem.at[k])
    for k in range(2):
        _row_copies(tile, copy(k))
    for k in range(2):
        pltpu.make_async_copy(h_ref, xb_ref.at[pl.ds(0, tile), :], sem.at[k]).wait()


def _dispatch(h2, dest_tiles, n_slots, *, tile):
    t, d = h2.shape
    xb0 = jnp.zeros((n_slots, d), h2.dtype)
    return pl.pallas_call(
        functools.partial(_dispatch_kernel, tile=tile),
        grid=(t // tile,),
        in_specs=[
            pl.BlockSpec((2 * tile,), lambda i: (i,), memory_space=pltpu.SMEM),
            pl.BlockSpec((tile, d), lambda i: (i, 0)),
            pl.BlockSpec(memory_space=pl.ANY),
        ],
        out_specs=pl.BlockSpec(memory_space=pl.ANY),
        out_shape=jax.ShapeDtypeStruct((n_slots, d), h2.dtype),
        scratch_shapes=[pltpu.SemaphoreType.DMA((2,))],
        input_output_aliases={2: 0},
        compiler_params=_cparams(1),
        name="moe_dispatch",
    )(dest_tiles, h2, xb0)


def _expert_kernel(be_ref, nu_ref, xb_ref, w1_ref, w3_ref, w2_ref, yb_ref):
    del be_ref
    used = pl.program_id(0) < nu_ref[0]

    @pl.when(used)
    def _():
        x = xb_ref[...].astype(BF16)
        a = _dot(x, w1_ref[...])
        b = _dot(x, w3_ref[...])
        yb_ref[...] = _dot((_silu(a) * b).astype(BF16), w2_ref[...])

    @pl.when(jnp.logical_not(used))
    def _():
        yb_ref[...] = jnp.zeros_like(yb_ref)


def _experts(xb, block_e, n_used, w1, w3, w2):
    n_slots, d = xb.shape
    de = w1.shape[2]
    nb = n_slots // MOE_BM
    blk = lambda i, be, nu: (jnp.minimum(i, nu[0] - 1), 0)
    return pl.pallas_call(
        _expert_kernel,
        grid_spec=pltpu.PrefetchScalarGridSpec(
            num_scalar_prefetch=2,
            grid=(nb,),
            in_specs=[
                pl.BlockSpec((MOE_BM, d), blk),
                pl.BlockSpec((None, d, de), lambda i, be, nu: (be[i], 0, 0)),
                pl.BlockSpec((None, d, de), lambda i, be, nu: (be[i], 0, 0)),
                pl.BlockSpec((None, de, d), lambda i, be, nu: (be[i], 0, 0)),
            ],
            out_specs=pl.BlockSpec((MOE_BM, d), lambda i, be, nu: (i, 0)),
        ),
        out_shape=jax.ShapeDtypeStruct((n_slots, d), F32),
        compiler_params=_cparams(1),
        name="moe_experts",
    )(block_e, n_used, xb, w1, w3, w2)


def _combine_kernel(dest_ref, x_ref, wt_ref, gate_ref, gainf_ref, yb_ref, o_ref, buf, sem,
                    *, tile, final_norm):
    def copy(k):
        return lambda r: pltpu.make_async_copy(
            yb_ref.at[pl.ds(dest_ref[k * tile + r], 1), :], buf.at[k, pl.ds(r, 1), :], sem.at[k])
    for k in range(2):
        _row_copies(tile, copy(k))
    for k in range(2):
        pltpu.make_async_copy(yb_ref.at[pl.ds(0, tile), :], buf.at[k], sem.at[k]).wait()
    wt = wt_ref[...]
    y = (buf[0] * wt[:, 0:1] + buf[1] * wt[:, 1:2]).astype(F32)
    x2 = x_ref[...] + gate_ref[...] * y
    if final_norm:
        x2 = x2 * lax.rsqrt(jnp.mean(x2 * x2, axis=-1, keepdims=True) + EPS) * gainf_ref[...]
    o_ref[...] = x2


def _combine(x1, yb, dest_tiles, wt_cols, modl, gain_f, *, tile, mod_row, first_tile, final_norm):
    t, d = x1.shape
    return pl.pallas_call(
        functools.partial(_combine_kernel, tile=tile, final_norm=final_norm),
        grid=(t // tile,),
        in_specs=[
            pl.BlockSpec((2 * tile,), lambda i: (i,), memory_space=pltpu.SMEM),
            pl.BlockSpec((tile, d), lambda i: (i, 0)),
            pl.BlockSpec((tile, 2), lambda i: (i, 0)),
            pl.BlockSpec((None, None, 1, d), lambda i: (mod_row(i + first_tile), 5, 0, 0)),
            pl.BlockSpec((1, d), lambda i: (0, 0)),
            pl.BlockSpec(memory_space=pl.ANY),
        ],
        out_specs=pl.BlockSpec((tile, d), lambda i: (i, 0)),
        out_shape=jax.ShapeDtypeStruct((t, d), F32),
        scratch_shapes=[pltpu.VMEM((2, tile, d), F32), pltpu.SemaphoreType.DMA((2,))],
        compiler_params=_cparams(1),
        name="moe_combine",
    )(dest_tiles, x1, wt_cols, modl, gain_f, yb)


def _slot_plan(eidx, *, tile, n_slots):
    _, t = eidx.shape
    onehot = (eidx[:, :, None] == jnp.arange(N_EXPERTS, dtype=jnp.int32)).astype(jnp.int32)
    flat = onehot.reshape(2 * t, N_EXPERTS)
    csum = jnp.cumsum(flat, axis=0)
    counts = csum[-1]
    rank = jnp.sum((csum - flat) * flat, axis=1).reshape(2, t)
    padded = (counts + MOE_BM - 1) // MOE_BM * MOE_BM
    pend = jnp.cumsum(padded)
    pstart = pend - padded
    dest = pstart[eidx] + rank
    nb = n_slots // MOE_BM
    block_e = jnp.searchsorted(pend, jnp.arange(nb, dtype=jnp.int32) * MOE_BM, side='right')
    n_used = (pend[-1] // MOE_BM).astype(jnp.int32)
    block_e = jnp.minimum(block_e, N_EXPERTS - 1).astype(jnp.int32)
    block_e = jnp.where(jnp.arange(nb) < n_used, block_e, block_e[jnp.maximum(n_used - 1, 0)])
    dest_tiles = dest.reshape(2, t // tile, tile).transpose(1, 0, 2).reshape(-1).astype(jnp.int32)
    return dest_tiles, block_e, n_used.reshape(1)


def kernel(x, c, ctx, c_ctx, w_mod, b_mod, norm1, norm2, w_in, w_gk2, b_gk, w_pool, pool_scale, gla_gain,
           w_out, w_router, router_bias, w1, w3, w2, norm_f):
    batch, seq, d = x.shape
    ctx_len = ctx.shape[1]
    depth = w_mod.shape[0]
    n_ctx = batch * ctx_len
    t_all = n_ctx + batch * seq
    def row_tile(limit):
        return max(t for t in (1024, 512, 256) if t <= limit and n_ctx % t == 0 and seq % t == 0)

    tile_a = row_tile(1024)
    tile_b = row_tile(512)

    def mod_row_for(tile):
        nct, per_b = n_ctx // tile, seq // tile
        return lambda i: jnp.where(i < nct, 0, 1 + (i - nct) // per_b)

    xf = jnp.concatenate([ctx.reshape(n_ctx, d), x.reshape(batch * seq, d)], axis=0)

    cvecs = jnp.zeros((16, d), F32).at[0].set(c_ctx).at[1:1 + batch].set(c)
    mods = _modulation(cvecs, w_mod, b_mod).reshape(depth, 16, N_MOD, 1, d)

    n_main = w_in.shape[2] - 2 * GATE_RANK
    w_main = w_in[:, :, :n_main].astype(BF16)
    w_r = jnp.pad(w_in[:, :, n_main:], ((0, 0), (0, 0), (0, 128 - 2 * GATE_RANK))).astype(BF16)
    w2g = jnp.zeros((depth, 128, 2 * GLA_QK), F32)
    w2g = w2g.at[:, 0:GATE_RANK, 0:GLA_QK].set(w_gk2[:, 0])
    w2g = w2g.at[:, GATE_RANK:2 * GATE_RANK, GLA_QK:].set(w_gk2[:, 1])
    bgk = b_gk.reshape(depth, 1, 2 * GLA_QK)
    w_out_b = w_out.astype(BF16)
    w_pool_b = w_pool.astype(BF16)
    w1b, w3b, w2b = w1.astype(BF16), w3.astype(BF16), w2.astype(BF16)
    w_router_t = w_router.T
    rbias = router_bias.reshape(N_EXPERTS, 1)
    g1 = lambda a: a.reshape(1, -1)

    out = None
    for l in range(depth):
        last = l == depth - 1
        pbf, lg = _inproj(xf, mods[l], g1(norm1[l]), w_main[l], w_r[l], w2g[l], bgk[l],
                          tile=tile_a, mod_row=mod_row_for(tile_a))
        att_c, att_x = _gla(pbf, lg, gla_gain[l], batch=batch, ctx_len=ctx_len, seq=seq)
        pool_c, pool_x = _pool(pbf, w_pool_b[l], pool_scale[l], batch=batch, ctx_len=ctx_len, seq=seq)
        first_tile = n_ctx // tile_b if last else 0
        n_tiles = t_all // tile_b - first_tile
        x1, h2, eidx, wts = _outproj(
            xf, pool_c, pool_x, att_c, att_x, mods[l], g1(norm2[l]), w_out_b[l], w_router_t, rbias,
            tile=tile_b, mod_row=mod_row_for(tile_b), first_tile=first_tile, n_tiles=n_tiles)
        t_act = eidx.shape[1]
        n_slots = -(-(2 * t_act + N_EXPERTS * (MOE_BM - 1)) // MOE_BM) * MOE_BM
        dest_tiles, block_e, n_used = _slot_plan(eidx, tile=tile_b, n_slots=n_slots)
        xb = _dispatch(h2, dest_tiles, n_slots, tile=tile_b)
        yb = _experts(xb, block_e, n_used, w1b[l], w3b[l], w2b[l])
        res = _combine(x1, yb, dest_tiles, wts.T, mods[l], g1(norm_f), tile=tile_b,
                       mod_row=mod_row_for(tile_b), first_tile=first_tile, final_norm=last)
        if last:
            out = res
        else:
            xf = res
    return out.reshape(batch, seq, d)
```

```python
import functools

import numpy as np
import jax
import jax.numpy as jnp
from jax import lax
from jax.experimental import pallas as pl
from jax.experimental.pallas import tpu as pltpu

F32 = jnp.float32
BF16 = jnp.bfloat16

GRID_W = 64
POOL_GROUPS = 4
POOL_WINDOWS = (2, 4, 8, 16)
GLA_HEADS = 4
GLA_DK = 64
GLA_DV = 128
GLA_QK = GLA_HEADS * GLA_DK
GATE_RANK = 16
GATE_NORM = 16.0
N_EXPERTS = 16
N_GROUPS = 4
EXPERTS_PER_GROUP = 4
EPS = 1e-6
N_MOD = 6

VMEM_LIMIT_BYTES = 56 * 1024 * 1024
SPAN = 256
SUB = 32
N_SUB = SPAN // SUB
EXP_CLAMP = 80.0
DIAG_W = 128
GLA_KW = DIAG_W + N_SUB * GLA_DK
MOE_BM = 256


def _cparams(n_axes):
    return pltpu.CompilerParams(
        dimension_semantics=("arbitrary",) * n_axes, vmem_limit_bytes=VMEM_LIMIT_BYTES)


def _split(a):
    hi = a.astype(BF16)
    lo = (a - hi.astype(F32)).astype(BF16)
    return hi, lo


def _dot(a, b):
    return jnp.dot(a, b, preferred_element_type=F32)


def _dot_nt(a, b):
    return lax.dot_general(a, b, (((1,), (1,)), ((), ())), preferred_element_type=F32)


def _dot_tn(a, b):
    return lax.dot_general(a, b, (((0,), (0,)), ((), ())), preferred_element_type=F32)


def _dot3(a, b):
    ah, al = _split(a)
    bh, bl = _split(b)
    return _dot(ah, bh) + (_dot(ah, bl) + _dot(al, bh))


def _dot3_nt(a, b):
    ah, al = _split(a)
    bh, bl = _split(b)
    return _dot_nt(ah, bh) + (_dot_nt(ah, bl) + _dot_nt(al, bh))


def _dot_exact_lhs(m_bf16, b):
    b0 = b.astype(BF16)
    r1 = b - b0.astype(F32)
    b1 = r1.astype(BF16)
    b2 = (r1 - b1.astype(F32)).astype(BF16)
    return _dot(m_bf16, b0) + (_dot(m_bf16, b1) + _dot(m_bf16, b2))


def _silu(x):
    return x * jax.nn.sigmoid(x)


def _mod_kernel(c_ref, w_ref, b_ref, o_ref):
    o_ref[0] = _dot3(_silu(c_ref[...]), w_ref[0]) + b_ref[0]


def _modulation(cvecs, w_mod, b_mod):
    depth, d, _ = w_mod.shape
    rows = cvecs.shape[0]
    return pl.pallas_call(
        _mod_kernel,
        grid=(depth, N_MOD),
        in_specs=[
            pl.BlockSpec((rows, d), lambda l, j: (0, 0)),
            pl.BlockSpec((1, d, d), lambda l, j: (l, 0, j)),
            pl.BlockSpec((1, 1, d), lambda l, j: (l, 0, j)),
        ],
        out_specs=pl.BlockSpec((1, rows, d), lambda l, j: (l, 0, j)),
        out_shape=jax.ShapeDtypeStruct((depth, rows, N_MOD * d), F32),
        compiler_params=_cparams(2),
        name="adaln_vectors",
    )(cvecs, w_mod, b_mod.reshape(depth, 1, N_MOD * d))


def _norm_mod(x, gain, scale, shift):
    y = x * lax.rsqrt(jnp.mean(x * x, axis=-1, keepdims=True) + EPS)
    return (y * gain) * (1.0 + scale) + shift


def _inproj_kernel(x_ref, scale_ref, shift_ref, gain_ref, w_ref, wr_ref, w2_ref, bgk_ref,
                   p_ref, lg_ref, *, n_chunk):
    h = _norm_mod(x_ref[...], gain_ref[...], scale_ref[...], shift_ref[...])
    hb = h.astype(BF16)
    width = p_ref.shape[1] // n_chunk
    for n in range(n_chunk):
        cols = slice(n * width, (n + 1) * width)
        p_ref[:, cols] = _dot(hb, w_ref[:, cols]).astype(BF16)
    r = _dot(hb, wr_ref[...])
    z = _dot3(r, w2_ref[...]) + bgk_ref[...]
    log_sig = jnp.minimum(z, 0.0) - jnp.log(1.0 + jnp.exp(-jnp.abs(z)))
    lg_ref[...] = log_sig / GATE_NORM


def _inproj(xf, modl, gain, w_main, w_r, w2, bgk, *, tile, mod_row):
    t, d = xf.shape
    n_main = w_main.shape[1]
    n_gate = w2.shape[1]
    mod_spec = lambda m: pl.BlockSpec((None, None, 1, d), lambda i: (mod_row(i), m, 0, 0))
    full = lambda a: pl.BlockSpec(a.shape, lambda i: (0,) * a.ndim)
    return pl.pallas_call(
        functools.partial(_inproj_kernel, n_chunk=4),
        grid=(t // tile,),
        in_specs=[
            pl.BlockSpec((tile, d), lambda i: (i, 0)),
            mod_spec(1), mod_spec(0), full(gain), full(w_main), full(w_r), full(w2), full(bgk),
        ],
        out_specs=[
            pl.BlockSpec((tile, n_main), lambda i: (i, 0)),
            pl.BlockSpec((tile, n_gate), lambda i: (i, 0)),
        ],
        out_shape=[
            jax.ShapeDtypeStruct((t, n_main), BF16),
            jax.ShapeDtypeStruct((t, n_gate), F32),
        ],
        compiler_params=_cparams(1),
        name="norm_inproj",
    )(xf, modl, modl, gain, w_main, w_r, w2, bgk)


def _store_heads(dst_ref, rows, col0, val):
    half = (col0 // GLA_DK) & 1
    plain = val.astype(BF16)
    turned = pltpu.roll(val, GLA_DK, axis=1).astype(BF16)
    for h in range(GLA_HEADS):
        if (h & 1) == half:
            src = plain[:, h * GLA_DK:(h + 1) * GLA_DK]
        else:
            hh = (h + 1) % GLA_HEADS
            src = turned[:, hh * GLA_DK:(hh + 1) * GLA_DK]
        dst_ref[h, rows, col0:col0 + GLA_DK] = src


def _gla_span(q_ref, k_ref, v_ref, lg_ref, o_ref, row0, st_ref, qh_ref, kh_ref, b_ref, *, rev, first_dir):
    rows = pl.ds(row0, SPAN)
    dcol = slice(rev * GLA_QK, (rev + 1) * GLA_QK)
    lg = lg_ref[rows, dcol]
    q = q_ref[rows, :].astype(F32) * (GLA_DK ** -0.5)
    k = k_ref[rows, :].astype(F32)
    ri = lax.broadcasted_iota(jnp.int32, (SPAN, SPAN), 0)
    ci = lax.broadcasted_iota(jnp.int32, (SPAN, SPAN), 1)
    causal = (ci >= ri) if rev else (ci <= ri)
    tri = jnp.where(causal, 1.0, 0.0).astype(BF16)
    b_ref[...] = _dot_exact_lhs(tri, lg)
    bc = b_ref[...]
    e_tot = b_ref[0:1, :] if rev else b_ref[SPAN - 1:SPAN, :]

    for s in range(N_SUB - 1):
        j = s + 1 if rev else s
        ref_row = j * SUB if rev else j * SUB + SUB - 1
        e_j = b_ref[ref_row:ref_row + 1, :]
        blk = slice(j * SUB, (j + 1) * SUB)
        qrows = slice(0, j * SUB) if rev else slice((j + 1) * SUB, SPAN)
        col0 = DIAG_W + s * GLA_DK
        _store_heads(qh_ref, qrows, col0, q[qrows] * jnp.exp(jnp.minimum(bc[qrows] - e_j, 0.0)))
        _store_heads(kh_ref, blk, col0, k[blk] * jnp.exp(jnp.minimum(e_j - bc[blk], 0.0)))
    for i in range(N_SUB):
        blk = slice(i * SUB, (i + 1) * SUB)
        mid = i * SUB + SUB // 2
        e_m = b_ref[mid:mid + 1, :]
        _store_heads(qh_ref, blk, 0, q[blk] * jnp.exp(jnp.minimum(bc[blk] - e_m, EXP_CLAMP)))
        _store_heads(kh_ref, blk, 0, k[blk] * jnp.exp(jnp.minimum(e_m - bc[blk], EXP_CLAMP)))
    same_blk = (ri >> (SUB.bit_length() - 1)) == (ci >> (SUB.bit_length() - 1))

    st = st_ref[...]
    qd = (q * jnp.exp(bc)).astype(BF16)
    o_state = _dot_nt(qd, st.astype(BF16))
    for h in range(GLA_HEADS):
        vc = slice(h * GLA_DV, (h + 1) * GLA_DV)
        a_diag = _dot_nt(qh_ref[h, :, 0:DIAG_W], kh_ref[h, :, 0:DIAG_W])
        a_cross = _dot_nt(qh_ref[h, :, DIAG_W:], kh_ref[h, :, DIAG_W:])
        a = (a_cross + jnp.where(same_blk, jnp.where(causal, a_diag, 0.0), 0.0)).astype(BF16)
        oh = _dot(a, v_ref[rows, vc]) + o_state[:, vc]
        if first_dir:
            o_ref[rows, vc] = oh
        else:
            o_ref[rows, vc] += oh
    kd = (k * jnp.exp(e_tot - bc)).astype(BF16)
    upd = _dot_tn(v_ref[rows, :], kd)
    sr = lax.broadcasted_iota(jnp.int32, upd.shape, 0) >> (GLA_DV.bit_length() - 1)
    sc = lax.broadcasted_iota(jnp.int32, upd.shape, 1) >> (GLA_DK.bit_length() - 1)
    st_ref[...] = st * jnp.exp(e_tot) + jnp.where(sr == sc, upd, 0.0)


def _gla_finish(o_ref, g_ref, gain, out_ref, n_rows):
    def body(c, carry):
        rows = pl.ds(pl.multiple_of(c * SPAN, SPAN), SPAN)
        for h in range(GLA_HEADS):
            vc = slice(h * GLA_DV, (h + 1) * GLA_DV)
            o = o_ref[rows, vc]
            y = o * lax.rsqrt(jnp.mean(o * o, axis=-1, keepdims=True) + EPS) * gain
            g = g_ref[rows, vc].astype(F32)
            out_ref[rows, vc] = (y * _silu(g)).astype(BF16)
        return carry
    lax.fori_loop(0, n_rows // SPAN, body, 0)


def _gla_kernel(qc, kc, vc, gc, lgc, qx, kx, vx, gx, lgx, gain_ref, outc, outx,
                oc, ox, st, qh, kh, bsc):
    n_c = qc.shape[0] // SPAN
    n_x = qx.shape[0] // SPAN
    for rev in (0, 1):
        qh[...] = jnp.zeros_like(qh)
        kh[...] = jnp.zeros_like(kh)
        st[...] = jnp.zeros_like(st)
        span = functools.partial(_gla_span, st_ref=st, qh_ref=qh, kh_ref=kh, b_ref=bsc,
                                 rev=rev, first_dir=(rev == 0))
        for s in range(n_c):
            span(qc, kc, vc, lgc, oc, ((n_c - 1 - s) if rev else s) * SPAN)

        def body(s, carry):
            idx = (n_x - 1 - s) if rev else s
            span(qx, kx, vx, lgx, ox, pl.multiple_of(idx * SPAN, SPAN))
            return carry
        lax.fori_loop(0, n_x, body, 0)
    gain = gain_ref[...]
    _gla_finish(oc, gc, gain, outc, qc.shape[0])
    _gla_finish(ox, gx, gain, outx, qx.shape[0])


def _gla(pbf, lg, gla_gain, *, batch, ctx_len, seq):
    nxb = (batch * ctx_len) // seq
    assert nxb * seq == batch * ctx_len
    hv = GLA_HEADS * GLA_DV
    cspec = lambda w, j: pl.BlockSpec((ctx_len, w), lambda b: (b, j))
    xspec = lambda w, j: pl.BlockSpec((seq, w), lambda b: (nxb + b, j))
    return pl.pallas_call(
        _gla_kernel,
        grid=(batch,),
        in_specs=[
            cspec(GLA_QK, 2), cspec(GLA_QK, 3), cspec(hv, 2), cspec(hv, 3), cspec(2 * GLA_QK, 0),
            xspec(GLA_QK, 2), xspec(GLA_QK, 3), xspec(hv, 2), xspec(hv, 3), xspec(2 * GLA_QK, 0),
            pl.BlockSpec((1, GLA_DV), lambda b: (0, 0)),
        ],
        out_specs=[
            pl.BlockSpec((ctx_len, hv), lambda b: (b, 0)),
            pl.BlockSpec((seq, hv), lambda b: (b, 0)),
        ],
        out_shape=[
            jax.ShapeDtypeStruct((batch * ctx_len, hv), BF16),
            jax.ShapeDtypeStruct((batch * seq, hv), BF16),
        ],
        scratch_shapes=[
            pltpu.VMEM((ctx_len, hv), F32),
            pltpu.VMEM((seq, hv), F32),
            pltpu.VMEM((hv, GLA_QK), F32),
            pltpu.VMEM((GLA_HEADS, SPAN, GLA_KW), BF16),
            pltpu.VMEM((GLA_HEADS, SPAN, GLA_KW), BF16),
            pltpu.VMEM((SPAN, GLA_QK), F32),
        ],
        compiler_params=_cparams(1),
        name="gla_bidirectional",
    )(pbf, pbf, pbf, pbf, lg, pbf, pbf, pbf, pbf, lg, gla_gain.reshape(1, GLA_DV))


def _window_offsets(w):
    return range(-(w // 2), w - w // 2)


def _band(w, period):
    ri = lax.broadcasted_iota(jnp.int32, (SPAN, SPAN), 0)
    ci = lax.broadcasted_iota(jnp.int32, (SPAN, SPAN), 1)
    d = ci - ri
    shift = period.bit_length() - 1
    assert period == 1 << shift
    same_row = (ri >> shift) == (ci >> shift)
    inside = jnp.where(d >= -(w // 2), jnp.where(d <= w - w // 2 - 1, 1.0, 0.0), 0.0)
    return jnp.where(same_row, inside, 0.0).astype(BF16)


def _pool_kernel(uc_ref, ux_ref, invc_ref, invx_ref, wp_ref, ps_ref, outc_ref, outx_ref, pad_ref,
                 *, pad_rows):
    n_x = ux_ref.shape[0]
    gw = wp_ref.shape[1]
    pad_ref[0:pad_rows, :] = jnp.zeros((pad_rows, gw), F32)
    pad_ref[pad_rows + n_x:, :] = jnp.zeros((pad_rows, gw), F32)
    for g, w in enumerate(POOL_WINDOWS):
        gc = slice(g * gw, (g + 1) * gw)
        wp = wp_ref[g]
        ps = ps_ref[:, gc]
        band_c = _band(w, SPAN)
        for t in range(uc_ref.shape[0] // SPAN):
            rows = slice(t * SPAN, (t + 1) * SPAN)
            u = uc_ref[rows, gc].astype(F32)
            pooled = _dot_exact_lhs(band_c, u) * invc_ref[g, rows, :] - u
            outc_ref[rows, gc] = (_dot(pooled.astype(BF16), wp) * ps).astype(BF16)
        pad_ref[pad_rows:pad_rows + n_x, :] = ux_ref[:, gc].astype(F32)
        band_x = _band(w, GRID_W)

        def body(t, carry):
            r0 = pl.multiple_of(t * SPAN, SPAN)
            acc = jnp.zeros((SPAN, gw), F32)
            for dr in _window_offsets(w):
                acc = acc + pad_ref[pl.ds(pad_rows + r0 + dr * GRID_W, SPAN), :]
            rows = pl.ds(r0, SPAN)
            u = pad_ref[pl.ds(pad_rows + r0, SPAN), :]
            pooled = _dot_exact_lhs(band_x, acc) * invx_ref[g, rows, :] - u
            outx_ref[rows, gc] = (_dot(pooled.astype(BF16), wp) * ps).astype(BF16)
            return carry
        lax.fori_loop(0, n_x // SPAN, body, 0)


def _inv_counts(length, rows_of):
    out = []
    for w in POOL_WINDOWS:
        cnt = np.ones((length,), np.float64)
        for axis_len, coord in rows_of(length):
            lo = np.clip(coord - w // 2, 0, axis_len)
            hi = np.clip(coord + w - w // 2, 0, axis_len)
            cnt = cnt * (hi - lo)
        out.append(1.0 / cnt)
    return np.broadcast_to(np.stack(out)[:, :, None], (len(POOL_WINDOWS), length, 128)).astype(np.float32)


def _pool(pbf, w_pool, pool_scale, *, batch, ctx_len, seq):
    nxb = (batch * ctx_len) // seq
    pw = w_pool.shape[0] * w_pool.shape[1]
    pad_rows = (max(POOL_WINDOWS) // 2) * GRID_W
    inv_c = jnp.asarray(_inv_counts(ctx_len, lambda n: [(n, np.arange(n))]))
    inv_x = jnp.asarray(_inv_counts(
        seq, lambda n: [(n // GRID_W, np.arange(n) // GRID_W), (GRID_W, np.arange(n) % GRID_W)]))
    full = lambda a: pl.BlockSpec(a.shape, lambda b: (0,) * a.ndim)
    return pl.pallas_call(
        functools.partial(_pool_kernel, pad_rows=pad_rows),
        grid=(batch,),
        in_specs=[
            pl.BlockSpec((ctx_len, pw), lambda b: (b, 0)),
            pl.BlockSpec((seq, pw), lambda b: (nxb + b, 0)),
            full(inv_c), full(inv_x), full(w_pool),
            pl.BlockSpec((1, pw), lambda b: (0, 0)),
        ],
        out_specs=[
            pl.BlockSpec((ctx_len, pw), lambda b: (b, 0)),
            pl.BlockSpec((seq, pw), lambda b: (b, 0)),
        ],
        out_shape=[
            jax.ShapeDtypeStruct((batch * ctx_len, pw), BF16),
            jax.ShapeDtypeStruct((batch * seq, pw), BF16),
        ],
        scratch_shapes=[pltpu.VMEM((seq + 2 * pad_rows, w_pool.shape[1]), F32)],
        compiler_params=_cparams(1),
        name="pool_mixer",
    )(pbf, pbf, inv_c, inv_x, w_pool, pool_scale.reshape(1, pw))


def _route(sb, s):
    def rank_in_group(vals):
        ranks = []
        for i, vi in enumerate(vals):
            r = jnp.zeros(vi.shape, jnp.int32)
            for j, vj in enumerate(vals):
                if j == i:
                    continue
                ahead = (vj >= vi) if j < i else (vj > vi)
                r = r + jnp.where(ahead, 1, 0)
            ranks.append(r)
        return ranks

    best = None
    for g in range(N_GROUPS):
        vals = sb[g * EXPERTS_PER_GROUP:(g + 1) * EXPERTS_PER_GROUP]
        svals = s[g * EXPERTS_PER_GROUP:(g + 1) * EXPERTS_PER_GROUP]
        score = None
        for i in range(EXPERTS_PER_GROUP):
            for j in range(i + 1, EXPERTS_PER_GROUP):
                pair = vals[i] + vals[j]
                score = pair if score is None else jnp.maximum(score, pair)
        ranks = rank_in_group(vals)
        e1 = jnp.zeros(score.shape, jnp.int32)
        e2 = jnp.zeros(score.shape, jnp.int32)
        w1 = jnp.zeros(score.shape, F32)
        w2 = jnp.zeros(score.shape, F32)
        for i in range(EXPERTS_PER_GROUP):
            e1 = jnp.where(ranks[i] == 0, g * EXPERTS_PER_GROUP + i, e1)
            e2 = jnp.where(ranks[i] == 1, g * EXPERTS_PER_GROUP + i, e2)
            w1 = jnp.where(ranks[i] == 0, svals[i], w1)
            w2 = jnp.where(ranks[i] == 1, svals[i], w2)
        cand = (score, e1, e2, w1, w2)
        if best is None:
            best = cand
        else:
            take = cand[0] > best[0]
            best = tuple(jnp.where(take, c, b) for c, b in zip(cand, best))
    _, e1, e2, w1, w2 = best
    tot = w1 + w2
    return e1, e2, w1 / tot, w2 / tot


def _outproj_kernel(x_ref, pc_ref, px_ref, ac_ref, ax_ref, gate_ref, scale_ref, shift_ref, gain_ref,
                    wo_ref, wrt_ref, rb_ref, before_ref, x1_ref, h2_ref, e_ref, wt_ref, rank_ref, cnt_ref,
                    *, n_ctx_tiles, half):
    is_ctx = pl.program_id(0) < n_ctx_tiles
    pool = jnp.where(is_ctx, pc_ref[...], px_ref[...])
    att = jnp.where(is_ctx, ac_ref[...], ax_ref[...])
    y = _dot(pool, wo_ref[0:half, :]) + _dot(att, wo_ref[half:, :])
    x1 = x_ref[...] + gate_ref[...] * y
    x1_ref[...] = x1
    h2 = _norm_mod(x1, gain_ref[...], scale_ref[...], shift_ref[...])
    h2_ref[...] = h2
    logits = _dot3_nt(wrt_ref[...], h2)
    s = jax.nn.sigmoid(logits)
    sb = s + rb_ref[...]
    rows = lambda a: [a[i:i + 1, :] for i in range(N_EXPERTS)]
    e1, e2, w1, w2 = _route(rows(sb), rows(s))
    e_ref[0:1, :] = e1
    e_ref[1:2, :] = e2
    wt_ref[0:1, :] = w1
    wt_ref[1:2, :] = w2
    ie = lax.broadcasted_iota(jnp.int32, logits.shape, 0)
    oh1 = jnp.where(ie == e1, 1.0, 0.0)
    oh2 = jnp.where(ie == e2, 1.0, 0.0)
    c1 = jnp.sum(oh1, axis=1, keepdims=True)
    c2 = jnp.sum(oh2, axis=1, keepdims=True)
    before1 = _dot(oh1.astype(BF16), before_ref[...])
    before2 = _dot(oh2.astype(BF16), before_ref[...]) + c1
    rank_ref[0:1, :] = jnp.sum(oh1 * before1, axis=0, keepdims=True).astype(jnp.int32)
    rank_ref[1:2, :] = jnp.sum(oh2 * before2, axis=0, keepdims=True).astype(jnp.int32)
    cnt_ref[...] = jnp.broadcast_to(c1 + c2, cnt_ref.shape).astype(jnp.int32)


def _outproj(xf, pool_c, pool_x, att_c, att_x, modl, gain2, w_out, w_router_t, router_bias,
             *, tile, mod_row, first_tile, n_tiles):
    t, d = xf.shape
    half = pool_c.shape[1]
    n_ctx_tiles = pool_c.shape[0] // tile
    cidx = lambda i: (jnp.minimum(i + first_tile, n_ctx_tiles - 1), 0)
    xidx = lambda i: (jnp.maximum(i + first_tile - n_ctx_tiles, 0), 0)
    mod_spec = lambda m: pl.BlockSpec((None, None, 1, d), lambda i: (mod_row(i + first_tile), m, 0, 0))
    full = lambda a: pl.BlockSpec(a.shape, lambda i: (0,) * a.ndim)
    tok = lambda w: pl.BlockSpec((tile, w), lambda i: (i, 0))
    lane = pl.BlockSpec((2, tile), lambda i: (0, i))
    t_out = n_tiles * tile
    before = jnp.asarray(np.triu(np.ones((tile, tile), np.float32), 1), BF16)
    return pl.pallas_call(
        functools.partial(_outproj_kernel, n_ctx_tiles=n_ctx_tiles - first_tile, half=half),
        grid=(n_tiles,),
        in_specs=[
            pl.BlockSpec((tile, d), lambda i: (i + first_tile, 0)),
            pl.BlockSpec((tile, half), cidx), pl.BlockSpec((tile, half), xidx),
            pl.BlockSpec((tile, half), cidx), pl.BlockSpec((tile, half), xidx),
            mod_spec(2), mod_spec(4), mod_spec(3), full(gain2), full(w_out), full(w_router_t),
            full(router_bias), full(before),
        ],
        out_specs=[tok(d), tok(d), lane, lane, lane,
                   pl.BlockSpec((None, N_EXPERTS, 128), lambda i: (i, 0, 0))],
        out_shape=[
            jax.ShapeDtypeStruct((t_out, d), F32),
            jax.ShapeDtypeStruct((t_out, d), F32),
            jax.ShapeDtypeStruct((2, t_out), jnp.int32),
            jax.ShapeDtypeStruct((2, t_out), F32),
            jax.ShapeDtypeStruct((2, t_out), jnp.int32),
            jax.ShapeDtypeStruct((n_tiles, N_EXPERTS, 128), jnp.int32),
        ],
        compiler_params=_cparams(1),
        name="outproj_router",
    )(xf, pool_c, pool_x, att_c, att_x, modl, modl, modl, gain2, w_out, w_router_t, router_bias, before)


def _row_copies(n, make):
    def body(r, carry):
        make(r).start()
        return carry
    lax.fori_loop(0, n, body, 0, unroll=8)


def _dispatch_kernel(tail_ref, dest_ref, h_ref, xb_ref, zero_ref, sem, zsem, *, tile):
    @pl.when(pl.program_id(0) == 0)
    def _():
        zero_ref[...] = jnp.zeros_like(zero_ref)

        def tail_copy(e):
            start = pl.multiple_of(jnp.maximum(tail_ref[e], 0), MOE_BM)
            return pltpu.make_async_copy(zero_ref, xb_ref.at[pl.ds(start, MOE_BM), :], zsem)
        n_blocks = xb_ref.shape[0] // MOE_BM

        def spare_copy(j):
            start = pl.multiple_of(jnp.minimum(tail_ref[N_EXPERTS] + j, n_blocks - 1) * MOE_BM, MOE_BM)
            return pltpu.make_async_copy(zero_ref, xb_ref.at[pl.ds(start, MOE_BM), :], zsem)
        for e in range(N_EXPERTS):
            pl.when(tail_ref[e] >= 0)(lambda e=e: tail_copy(e).start())
            pl.when(tail_ref[N_EXPERTS] + e < n_blocks)(lambda e=e: spare_copy(e).start())
        for e in range(N_EXPERTS):
            pl.when(tail_ref[e] >= 0)(lambda e=e: tail_copy(e).wait())
            pl.when(tail_ref[N_EXPERTS] + e < n_blocks)(lambda e=e: spare_copy(e).wait())

    def copy(k):
        return lambda r: pltpu.make_async_copy(
            h_ref.at[pl.ds(r, 1), :], xb_ref.at[pl.ds(dest_ref[k * tile + r], 1), :], sem.at[k])
    for k in range(2):
        _row_copies(tile, copy(k))
    for k in range(2):
        pltpu.make_async_copy(h_ref, xb_ref.at[pl.ds(0, tile), :], sem.at[k]).wait()


def _dispatch(h2, dest_tiles, tail_rows, n_slots, *, tile):
    t, d = h2.shape
    return pl.pallas_call(
        functools.partial(_dispatch_kernel, tile=tile),
        grid_spec=pltpu.PrefetchScalarGridSpec(
            num_scalar_prefetch=1,
            grid=(t // tile,),
            in_specs=[
                pl.BlockSpec((2 * tile,), lambda i, tail: (i,), memory_space=pltpu.SMEM),
                pl.BlockSpec((tile, d), lambda i, tail: (i, 0)),
            ],
            out_specs=pl.BlockSpec(memory_space=pl.ANY),
            scratch_shapes=[pltpu.VMEM((MOE_BM, d), h2.dtype), pltpu.SemaphoreType.DMA((2,)),
                            pltpu.SemaphoreType.DMA(())],
        ),
        out_shape=jax.ShapeDtypeStruct((n_slots, d), h2.dtype),
        compiler_params=_cparams(1),
        name="moe_dispatch",
    )(tail_rows, dest_tiles, h2)


def _expert_kernel(be_ref, nu_ref, xb_ref, w1_ref, w3_ref, w2_ref, yb_ref):
    del be_ref
    used = pl.program_id(0) < nu_ref[0]

    @pl.when(used)
    def _():
        x = xb_ref[...].astype(BF16)
        a = _dot(x, w1_ref[...])
        b = _dot(x, w3_ref[...])
        yb_ref[...] = _dot((_silu(a) * b).astype(BF16), w2_ref[...])

    @pl.when(jnp.logical_not(used))
    def _():
        yb_ref[...] = jnp.zeros_like(yb_ref)


def _experts(xb, block_e, n_used, w1, w3, w2):
    n_slots, d = xb.shape
    de = w1.shape[2]
    nb = n_slots // MOE_BM
    blk = lambda i, be, nu: (jnp.minimum(i, nu[0] - 1), 0)
    return pl.pallas_call(
        _expert_kernel,
        grid_spec=pltpu.PrefetchScalarGridSpec(
            num_scalar_prefetch=2,
            grid=(nb,),
            in_specs=[
                pl.BlockSpec((MOE_BM, d), blk),
                pl.BlockSpec((None, d, de), lambda i, be, nu: (be[i], 0, 0)),
                pl.BlockSpec((None, d, de), lambda i, be, nu: (be[i], 0, 0)),
                pl.BlockSpec((None, de, d), lambda i, be, nu: (be[i], 0, 0)),
            ],
            out_specs=pl.BlockSpec((MOE_BM, d), lambda i, be, nu: (i, 0)),
        ),
        out_shape=jax.ShapeDtypeStruct((n_slots, d), F32),
        compiler_params=_cparams(1),
        name="moe_experts",
    )(block_e, n_used, xb, w1, w3, w2)


def _combine_kernel(dest_ref, x_ref, wt_ref, gate_ref, gainf_ref, yb_ref, o_ref, buf, sem,
                    *, tile, final_norm):
    def copy(k):
        return lambda r: pltpu.make_async_copy(
            yb_ref.at[pl.ds(dest_ref[k * tile + r], 1), :], buf.at[k, pl.ds(r, 1), :], sem.at[k])
    for k in range(2):
        _row_copies(tile, copy(k))
    for k in range(2):
        pltpu.make_async_copy(yb_ref.at[pl.ds(0, tile), :], buf.at[k], sem.at[k]).wait()
    wt = wt_ref[...]
    y = (buf[0] * wt[:, 0:1] + buf[1] * wt[:, 1:2]).astype(F32)
    x2 = x_ref[...] + gate_ref[...] * y
    if final_norm:
        x2 = x2 * lax.rsqrt(jnp.mean(x2 * x2, axis=-1, keepdims=True) + EPS) * gainf_ref[...]
    o_ref[...] = x2


def _combine(x1, yb, dest_tiles, wt_cols, modl, gain_f, *, tile, mod_row, first_tile, final_norm):
    t, d = x1.shape
    return pl.pallas_call(
        functools.partial(_combine_kernel, tile=tile, final_norm=final_norm),
        grid=(t // tile,),
        in_specs=[
            pl.BlockSpec((2 * tile,), lambda i: (i,), memory_space=pltpu.SMEM),
            pl.BlockSpec((tile, d), lambda i: (i, 0)),
            pl.BlockSpec((tile, 2), lambda i: (i, 0)),
            pl.BlockSpec((None, None, 1, d), lambda i: (mod_row(i + first_tile), 5, 0, 0)),
            pl.BlockSpec((1, d), lambda i: (0, 0)),
            pl.BlockSpec(memory_space=pl.ANY),
        ],
        out_specs=pl.BlockSpec((tile, d), lambda i: (i, 0)),
        out_shape=jax.ShapeDtypeStruct((t, d), F32),
        scratch_shapes=[pltpu.VMEM((2, tile, d), F32), pltpu.SemaphoreType.DMA((2,))],
        compiler_params=_cparams(1),
        name="moe_combine",
    )(dest_tiles, x1, wt_cols, modl, gain_f, yb)


def _slot_plan(eidx, rank, tile_counts, *, tile, n_slots):
    _, t = eidx.shape
    experts = jnp.arange(N_EXPERTS, dtype=jnp.int32)
    counts = jnp.sum(tile_counts, axis=0)
    padded = (counts + MOE_BM - 1) // MOE_BM * MOE_BM
    pend = jnp.cumsum(padded)
    pstart = pend - padded
    tile_base = pstart[None, :] + jnp.cumsum(tile_counts, axis=0) - tile_counts
    base_tok = jnp.broadcast_to(tile_base[:, None, :], (t // tile, tile, N_EXPERTS)).reshape(t, N_EXPERTS)
    dest = jnp.sum(jnp.where(eidx[:, :, None] == experts, base_tok[None], 0), axis=-1) + rank
    nb = n_slots // MOE_BM
    n_used = pend[-1] // MOE_BM
    blk = jnp.minimum(jnp.arange(nb, dtype=jnp.int32), n_used - 1)
    block_e = jnp.sum((pend[None, :] <= (blk * MOE_BM)[:, None]).astype(jnp.int32), axis=1)
    block_e = jnp.minimum(block_e, N_EXPERTS - 1).astype(jnp.int32)
    tail_rows = jnp.where(counts > 0, pend - MOE_BM, -1).astype(jnp.int32)
    tail_rows = jnp.concatenate([tail_rows, n_used.astype(jnp.int32).reshape(1)])
    dest_tiles = dest.reshape(2, t // tile, tile).transpose(1, 0, 2).reshape(-1).astype(jnp.int32)
    return dest_tiles, block_e, n_used.astype(jnp.int32).reshape(1), tail_rows


def kernel(x, c, ctx, c_ctx, w_mod, b_mod, norm1, norm2, w_in, w_gk2, b_gk, w_pool, pool_scale, gla_gain,
           w_out, w_router, router_bias, w1, w3, w2, norm_f):
    batch, seq, d = x.shape
    ctx_len = ctx.shape[1]
    depth = w_mod.shape[0]
    n_ctx = batch * ctx_len
    t_all = n_ctx + batch * seq
    def row_tile(limit):
        return max(t for t in (1024, 512, 256) if t <= limit and n_ctx % t == 0 and seq % t == 0)

    tile_a = row_tile(1024)
    tile_b = row_tile(512)

    def mod_row_for(tile):
        nct, per_b = n_ctx // tile, seq // tile
        return lambda i: jnp.where(i < nct, 0, 1 + (i - nct) // per_b)

    xf = jnp.concatenate([ctx.reshape(n_ctx, d), x.reshape(batch * seq, d)], axis=0)

    cvecs = jnp.zeros((16, d), F32).at[0].set(c_ctx).at[1:1 + batch].set(c)
    mods = _modulation(cvecs, w_mod, b_mod).reshape(depth, 16, N_MOD, 1, d)

    n_main = w_in.shape[2] - 2 * GATE_RANK
    w_main = w_in[:, :, :n_main].astype(BF16)
    w_r = jnp.pad(w_in[:, :, n_main:], ((0, 0), (0, 0), (0, 128 - 2 * GATE_RANK))).astype(BF16)
    w2g = jnp.zeros((depth, 128, 2 * GLA_QK), F32)
    w2g = w2g.at[:, 0:GATE_RANK, 0:GLA_QK].set(w_gk2[:, 0])
    w2g = w2g.at[:, GATE_RANK:2 * GATE_RANK, GLA_QK:].set(w_gk2[:, 1])
    bgk = b_gk.reshape(depth, 1, 2 * GLA_QK)
    w_out_b = w_out.astype(BF16)
    w_pool_b = w_pool.astype(BF16)
    w1b, w3b, w2b = w1.astype(BF16), w3.astype(BF16), w2.astype(BF16)
    w_router_t = w_router.T
    rbias = router_bias.reshape(N_EXPERTS, 1)
    g1 = lambda a: a.reshape(1, -1)

    out = None
    for l in range(depth):
        last = l == depth - 1
        pbf, lg = _inproj(xf, mods[l], g1(norm1[l]), w_main[l], w_r[l], w2g[l], bgk[l],
                          tile=tile_a, mod_row=mod_row_for(tile_a))
        att_c, att_x = _gla(pbf, lg, gla_gain[l], batch=batch, ctx_len=ctx_len, seq=seq)
        pool_c, pool_x = _pool(pbf, w_pool_b[l], pool_scale[l], batch=batch, ctx_len=ctx_len, seq=seq)
        first_tile = n_ctx // tile_b if last else 0
        n_tiles = t_all // tile_b - first_tile
        x1, h2, eidx, wts, rank, cnt = _outproj(
            xf, pool_c, pool_x, att_c, att_x, mods[l], g1(norm2[l]), w_out_b[l], w_router_t, rbias,
            tile=tile_b, mod_row=mod_row_for(tile_b), first_tile=first_tile, n_tiles=n_tiles)
        t_act = eidx.shape[1]
        n_slots = -(-(2 * t_act + N_EXPERTS * (MOE_BM - 1)) // MOE_BM) * MOE_BM
        dest_tiles, block_e, n_used, tail_rows = _slot_plan(
            eidx, rank, cnt[:, :, 0], tile=tile_b, n_slots=n_slots)
        xb = _dispatch(h2, dest_tiles, tail_rows, n_slots, tile=tile_b)
        yb = _experts(xb, block_e, n_used, w1b[l], w3b[l], w2b[l])
        res = _combine(x1, yb, dest_tiles, wts.T, mods[l], g1(norm_f), tile=tile_b,
                       mod_row=mod_row_for(tile_b), first_tile=first_tile, final_norm=last)
        if last:
            out = res
        else:
            xf = res
    return out.reshape(batch, seq, d)
```

```python
import functools

import numpy as np
import jax
import jax.numpy as jnp
from jax import lax
from jax.experimental import pallas as pl
from jax.experimental.pallas import tpu as pltpu

F32 = jnp.float32
BF16 = jnp.bfloat16

GRID_W = 64
POOL_GROUPS = 4
POOL_WINDOWS = (2, 4, 8, 16)
GLA_HEADS = 4
GLA_DK = 64
GLA_DV = 128
GLA_QK = GLA_HEADS * GLA_DK
GATE_RANK = 16
GATE_NORM = 16.0
N_EXPERTS = 16
N_GROUPS = 4
EXPERTS_PER_GROUP = 4
EPS = 1e-6
N_MOD = 6

VMEM_LIMIT_BYTES = 56 * 1024 * 1024
SPAN = 256
SUB = 32
N_SUB = SPAN // SUB
LOG2_E = 1.4426950408889634
EXP2_CLAMP = 115.0
DIAG_W = 128
GLA_KW = DIAG_W + N_SUB * GLA_DK
MOE_BM = 256
SEG_ALIGN = 16
SEG_CHUNKS = (512, 256, 128, 64, 32, 16)


def _cparams(n_axes):
    return pltpu.CompilerParams(
        dimension_semantics=("arbitrary",) * n_axes, vmem_limit_bytes=VMEM_LIMIT_BYTES)


def _split(a):
    hi = a.astype(BF16)
    lo = (a - hi.astype(F32)).astype(BF16)
    return hi, lo


def _dot(a, b):
    return jnp.dot(a, b, preferred_element_type=F32)


def _dot_nt(a, b):
    return lax.dot_general(a, b, (((1,), (1,)), ((), ())), preferred_element_type=F32)


def _dot_tn(a, b):
    return lax.dot_general(a, b, (((0,), (0,)), ((), ())), preferred_element_type=F32)


def _dot3(a, b):
    ah, al = _split(a)
    bh, bl = _split(b)
    return _dot(ah, bh) + (_dot(ah, bl) + _dot(al, bh))


def _dot3_nt(a, b):
    ah, al = _split(a)
    bh, bl = _split(b)
    return _dot_nt(ah, bh) + (_dot_nt(ah, bl) + _dot_nt(al, bh))


def _dot_exact_lhs(m_bf16, b):
    b0 = b.astype(BF16)
    r1 = b - b0.astype(F32)
    b1 = r1.astype(BF16)
    b2 = (r1 - b1.astype(F32)).astype(BF16)
    return _dot(m_bf16, b0) + (_dot(m_bf16, b1) + _dot(m_bf16, b2))


def _silu(x):
    return x * jax.nn.sigmoid(x)


def _mod_kernel(c_ref, w_ref, b_ref, o_ref):
    o_ref[0] = _dot3(_silu(c_ref[...]), w_ref[0]) + b_ref[0]


def _modulation(cvecs, w_mod, b_mod):
    depth, d, _ = w_mod.shape
    rows = cvecs.shape[0]
    return pl.pallas_call(
        _mod_kernel,
        grid=(depth, N_MOD),
        in_specs=[
            pl.BlockSpec((rows, d), lambda l, j: (0, 0)),
            pl.BlockSpec((1, d, d), lambda l, j: (l, 0, j)),
            pl.BlockSpec((1, 1, d), lambda l, j: (l, 0, j)),
        ],
        out_specs=pl.BlockSpec((1, rows, d), lambda l, j: (l, 0, j)),
        out_shape=jax.ShapeDtypeStruct((depth, rows, N_MOD * d), F32),
        compiler_params=_cparams(2),
        name="adaln_vectors",
    )(cvecs, w_mod, b_mod.reshape(depth, 1, N_MOD * d))


def _norm_mod(x, gain, scale, shift):
    y = x * lax.rsqrt(jnp.mean(x * x, axis=-1, keepdims=True) + EPS)
    return (y * gain) * (1.0 + scale) + shift


def _inproj_kernel(x_ref, scale_ref, shift_ref, gain_ref, w_ref, wr_ref, w2_ref, bgk_ref,
                   p_ref, lg_ref, *, n_chunk):
    h = _norm_mod(x_ref[...], gain_ref[...], scale_ref[...], shift_ref[...])
    hb = h.astype(BF16)
    width = p_ref.shape[1] // n_chunk
    for n in range(n_chunk):
        cols = slice(n * width, (n + 1) * width)
        p_ref[:, cols] = _dot(hb, w_ref[:, cols]).astype(BF16)
    r = _dot(hb, wr_ref[...])
    z = _dot3(r, w2_ref[...]) + bgk_ref[...]
    log_sig = jnp.minimum(z, 0.0) - jnp.log(1.0 + jnp.exp(-jnp.abs(z)))
    lg_ref[...] = log_sig * (LOG2_E / GATE_NORM)


def _inproj(xf, modl, gain, w_main, w_r, w2, bgk, *, tile, mod_row):
    t, d = xf.shape
    n_main = w_main.shape[1]
    n_gate = w2.shape[1]
    mod_spec = lambda m: pl.BlockSpec((None, None, 1, d), lambda i: (mod_row(i), m, 0, 0))
    full = lambda a: pl.BlockSpec(a.shape, lambda i: (0,) * a.ndim)
    return pl.pallas_call(
        functools.partial(_inproj_kernel, n_chunk=4),
        grid=(t // tile,),
        in_specs=[
            pl.BlockSpec((tile, d), lambda i: (i, 0)),
            mod_spec(1), mod_spec(0), full(gain), full(w_main), full(w_r), full(w2), full(bgk),
        ],
        out_specs=[
            pl.BlockSpec((tile, n_main), lambda i: (i, 0)),
            pl.BlockSpec((tile, n_gate), lambda i: (i, 0)),
        ],
        out_shape=[
            jax.ShapeDtypeStruct((t, n_main), BF16),
            jax.ShapeDtypeStruct((t, n_gate), F32),
        ],
        compiler_params=_cparams(1),
        name="norm_inproj",
    )(xf, modl, modl, gain, w_main, w_r, w2, bgk)


def _store_heads(dst_ref, rows, col0, val):
    half = (col0 // GLA_DK) & 1
    plain = val.astype(BF16)
    turned = pltpu.roll(val, GLA_DK, axis=1).astype(BF16)
    for h in range(GLA_HEADS):
        if (h & 1) == half:
            src = plain[:, h * GLA_DK:(h + 1) * GLA_DK]
        else:
            hh = (h + 1) % GLA_HEADS
            src = turned[:, hh * GLA_DK:(hh + 1) * GLA_DK]
        dst_ref[h, rows, col0:col0 + GLA_DK] = src


def _gla_span(q_ref, k_ref, v_ref, lg_ref, o_ref, row0, st_ref, qh_ref, kh_ref, b_ref, qf_ref, kf_ref,
              tri_ref, dmask_ref, smask_ref, *, rev):
    rows = pl.ds(row0, SPAN)
    dcol = slice(rev * GLA_QK, (rev + 1) * GLA_QK)
    b_ref[...] = _dot_exact_lhs(tri_ref[rev], lg_ref[rows, dcol])
    qf_ref[...] = q_ref[rows, :].astype(F32) * (GLA_DK ** -0.5)
    kf_ref[...] = k_ref[rows, :].astype(F32)
    e_tot = b_ref[0:1, :] if rev else b_ref[SPAN - 1:SPAN, :]

    for s in range(N_SUB - 1):
        j = s + 1 if rev else s
        ref_row = j * SUB if rev else j * SUB + SUB - 1
        e_j = b_ref[ref_row:ref_row + 1, :]
        blk = slice(j * SUB, (j + 1) * SUB)
        qrows = slice(0, j * SUB) if rev else slice((j + 1) * SUB, SPAN)
        col0 = DIAG_W + s * GLA_DK
        _store_heads(qh_ref, qrows, col0, qf_ref[qrows, :] * jnp.exp2(b_ref[qrows, :] - e_j))
        _store_heads(kh_ref, blk, col0, kf_ref[blk, :] * jnp.exp2(e_j - b_ref[blk, :]))
    for i in range(N_SUB):
        blk = slice(i * SUB, (i + 1) * SUB)
        mid = i * SUB + SUB // 2
        e_m = b_ref[mid:mid + 1, :]
        _store_heads(qh_ref, blk, 0, qf_ref[blk, :] * jnp.exp2(jnp.minimum(b_ref[blk, :] - e_m, EXP2_CLAMP)))
        _store_heads(kh_ref, blk, 0, kf_ref[blk, :] * jnp.exp2(jnp.minimum(e_m - b_ref[blk, :], EXP2_CLAMP)))

    st = st_ref[...]
    qd = (qf_ref[...] * jnp.exp2(b_ref[...])).astype(BF16)
    o_state = _dot_nt(qd, st.astype(BF16))
    in_block = dmask_ref[rev] != 0.0
    for h in range(GLA_HEADS):
        vc = slice(h * GLA_DV, (h + 1) * GLA_DV)
        a_diag = _dot_nt(qh_ref[h, :, 0:DIAG_W], kh_ref[h, :, 0:DIAG_W])
        a_cross = _dot_nt(qh_ref[h, :, DIAG_W:], kh_ref[h, :, DIAG_W:])
        a = (a_cross + jnp.where(in_block, a_diag, 0.0)).astype(BF16)
        o_ref[rows, vc] += _dot(a, v_ref[rows, vc]) + o_state[:, vc]
    kd =(kf_ref[...] * jnp.exp2(e_tot - b_ref[...])).astype(BF16)
    upd = _dot_tn(v_ref[rows, :], kd)
    st_ref[...] = st * jnp.exp2(e_tot) + upd * smask_ref[...]


def _gla_finish(o_ref, g_ref, gain, out_ref, n_rows):
    def body(c, carry):
        rows = pl.ds(pl.multiple_of(c * SPAN, SPAN), SPAN)
        for h in range(GLA_HEADS):
            vc = slice(h * GLA_DV, (h + 1) * GLA_DV)
            o = o_ref[rows, vc]
            y = o * lax.rsqrt(jnp.mean(o * o, axis=-1, keepdims=True) + EPS) * gain
            g = g_ref[rows, vc].astype(F32)
            out_ref[rows, vc] = (y * _silu(g)).astype(BF16)
        return carry
    lax.fori_loop(0, n_rows // SPAN, body, 0)


def _gla_kernel(qc, kc, vc, gc, lgc, qx, kx, vx, gx, lgx, gain_ref, tri, dmask, smask, outc, outx,
                oc, ox, *per_direction):
    n_c = qc.shape[0] // SPAN
    n_x = qx.shape[0] // SPAN
    names = ("st_ref", "qh_ref", "kh_ref", "b_ref", "qf_ref", "kf_ref")
    scratch = [dict(zip(names, per_direction[r * len(names):(r + 1) * len(names)])) for r in (0, 1)]
    for ref in (oc, ox) + tuple(s[n] for s in scratch for n in ("st_ref", "qh_ref", "kh_ref")):
        ref[...] = jnp.zeros_like(ref)
    spans = [functools.partial(_gla_span, tri_ref=tri, dmask_ref=dmask, smask_ref=smask, rev=rev,
                               **scratch[rev]) for rev in (0, 1)]
    for s in range(n_c):
        spans[0](qc, kc, vc, lgc, oc, s * SPAN)
        spans[1](qc, kc, vc, lgc, oc, (n_c - 1 - s) * SPAN)

    def body(s, carry):
        spans[0](qx, kx, vx, lgx, ox, pl.multiple_of(s * SPAN, SPAN))
        spans[1](qx, kx, vx, lgx, ox, pl.multiple_of((n_x - 1 - s) * SPAN, SPAN))
        return carry
    lax.fori_loop(0, n_x, body, 0)
    gain = gain_ref[...]
    _gla_finish(oc, gc, gain, outc, qc.shape[0])
    _gla_finish(ox, gx, gain, outx, qx.shape[0])


def _gla(pbf, lg, gla_gain, *, batch, ctx_len, seq):
    nxb = (batch * ctx_len) // seq
    assert nxb * seq == batch * ctx_len
    hv = GLA_HEADS * GLA_DV
    cspec = lambda w, j: pl.BlockSpec((ctx_len, w), lambda b: (b, j))
    xspec = lambda w, j: pl.BlockSpec((seq, w), lambda b: (nxb + b, j))
    full = lambda a: pl.BlockSpec(a.shape, lambda b: (0,) * a.ndim)
    pos = np.arange(SPAN)
    lower = pos[None, :] <= pos[:, None]
    same = (pos[None, :] // SUB) == (pos[:, None] // SUB)
    tri = jnp.asarray(np.stack([lower, lower.T]).astype(np.float32), BF16)
    dmask = jnp.asarray(np.stack([lower & same, lower.T & same]).astype(np.float32))
    smask = jnp.asarray(((np.arange(hv)[:, None] // GLA_DV) == (np.arange(GLA_QK)[None, :] // GLA_DK))
                        .astype(np.float32))
    return pl.pallas_call(
        _gla_kernel,
        grid=(batch,),
        in_specs=[
            cspec(GLA_QK, 2), cspec(GLA_QK, 3), cspec(hv, 2), cspec(hv, 3), cspec(2 * GLA_QK, 0),
            xspec(GLA_QK, 2), xspec(GLA_QK, 3), xspec(hv, 2), xspec(hv, 3), xspec(2 * GLA_QK, 0),
            pl.BlockSpec((1, GLA_DV), lambda b: (0, 0)), full(tri), full(dmask), full(smask),
        ],
        out_specs=[
            pl.BlockSpec((ctx_len, hv), lambda b: (b, 0)),
            pl.BlockSpec((seq, hv), lambda b: (b, 0)),
        ],
        out_shape=[
            jax.ShapeDtypeStruct((batch * ctx_len, hv), BF16),
            jax.ShapeDtypeStruct((batch * seq, hv), BF16),
        ],
        scratch_shapes=[
            pltpu.VMEM((ctx_len, hv), F32),
            pltpu.VMEM((seq, hv), F32),
        ] + 2 * [
            pltpu.VMEM((hv, GLA_QK), F32),
            pltpu.VMEM((GLA_HEADS, SPAN, GLA_KW), BF16),
            pltpu.VMEM((GLA_HEADS, SPAN, GLA_KW), BF16),
            pltpu.VMEM((SPAN, GLA_QK), F32),
            pltpu.VMEM((SPAN, GLA_QK), F32),
            pltpu.VMEM((SPAN, GLA_QK), F32),
        ],
        compiler_params=_cparams(1),
        name="gla_bidirectional",
    )(pbf, pbf, pbf, pbf, lg, pbf, pbf, pbf, pbf, lg, gla_gain.reshape(1, GLA_DV), tri, dmask, smask)


def _window_offsets(w):
    return range(-(w // 2), w - w // 2)


def _band(w, period):
    ri = lax.broadcasted_iota(jnp.int32, (SPAN, SPAN), 0)
    ci = lax.broadcasted_iota(jnp.int32, (SPAN, SPAN), 1)
    d = ci - ri
    shift = period.bit_length() - 1
    assert period == 1 << shift
    same_row = (ri >> shift) == (ci >> shift)
    inside = jnp.where(d >= -(w // 2), jnp.where(d <= w - w // 2 - 1, 1.0, 0.0), 0.0)
    return jnp.where(same_row, inside, 0.0).astype(BF16)


def _pool_kernel(uc_ref, ux_ref, invc_ref, invx_ref, wp_ref, ps_ref, outc_ref, outx_ref, pad_ref,
                 *, pad_rows):
    n_x = ux_ref.shape[0]
    gw = wp_ref.shape[1]
    pad_ref[0:pad_rows, :] = jnp.zeros((pad_rows, gw), F32)
    pad_ref[pad_rows + n_x:, :] = jnp.zeros((pad_rows, gw), F32)
    for g, w in enumerate(POOL_WINDOWS):
        gc = slice(g * gw, (g + 1) * gw)
        wp = wp_ref[g]
        ps = ps_ref[:, gc]
        band_c = _band(w, SPAN)
        for t in range(uc_ref.shape[0] // SPAN):
            rows = slice(t * SPAN, (t + 1) * SPAN)
            u = uc_ref[rows, gc].astype(F32)
            pooled = _dot_exact_lhs(band_c, u) * invc_ref[g, rows, :] - u
            outc_ref[rows, gc] = (_dot(pooled.astype(BF16), wp) * ps).astype(BF16)
        pad_ref[pad_rows:pad_rows + n_x, :] = ux_ref[:, gc].astype(F32)
        band_x = _band(w, GRID_W)

        def body(t, carry):
            r0 = pl.multiple_of(t * SPAN, SPAN)
            acc = jnp.zeros((SPAN, gw), F32)
            for dr in _window_offsets(w):
                acc = acc + pad_ref[pl.ds(pad_rows + r0 + dr * GRID_W, SPAN), :]
            rows = pl.ds(r0, SPAN)
            u = pad_ref[pl.ds(pad_rows + r0, SPAN), :]
            pooled = _dot_exact_lhs(band_x, acc) * invx_ref[g, rows, :] - u
            outx_ref[rows, gc] = (_dot(pooled.astype(BF16), wp) * ps).astype(BF16)
            return carry
        lax.fori_loop(0, n_x // SPAN, body, 0)


def _inv_counts(length, rows_of):
    out = []
    for w in POOL_WINDOWS:
        cnt = np.ones((length,), np.float64)
        for axis_len, coord in rows_of(length):
            lo = np.clip(coord - w // 2, 0, axis_len)
            hi = np.clip(coord + w - w // 2, 0, axis_len)
            cnt = cnt * (hi - lo)
        out.append(1.0 / cnt)
    return np.broadcast_to(np.stack(out)[:, :, None], (len(POOL_WINDOWS), length, 128)).astype(np.float32)


def _pool(pbf, w_pool, pool_scale, *, batch, ctx_len, seq):
    nxb = (batch * ctx_len) // seq
    pw = w_pool.shape[0] * w_pool.shape[1]
    pad_rows = (max(POOL_WINDOWS) // 2) * GRID_W
    inv_c = jnp.asarray(_inv_counts(ctx_len, lambda n: [(n, np.arange(n))]))
    inv_x = jnp.asarray(_inv_counts(
        seq, lambda n: [(n // GRID_W, np.arange(n) // GRID_W), (GRID_W, np.arange(n) % GRID_W)]))
    full = lambda a: pl.BlockSpec(a.shape, lambda b: (0,) * a.ndim)
    return pl.pallas_call(
        functools.partial(_pool_kernel, pad_rows=pad_rows),
        grid=(batch,),
        in_specs=[
            pl.BlockSpec((ctx_len, pw), lambda b: (b, 0)),
            pl.BlockSpec((seq, pw), lambda b: (nxb + b, 0)),
            full(inv_c), full(inv_x), full(w_pool),
            pl.BlockSpec((1, pw), lambda b: (0, 0)),
        ],
        out_specs=[
            pl.BlockSpec((ctx_len, pw), lambda b: (b, 0)),
            pl.BlockSpec((seq, pw), lambda b: (b, 0)),
        ],
        out_shape=[
            jax.ShapeDtypeStruct((batch * ctx_len, pw), BF16),
            jax.ShapeDtypeStruct((batch * seq, pw), BF16),
        ],
        scratch_shapes=[pltpu.VMEM((seq + 2 * pad_rows, w_pool.shape[1]), F32)],
        compiler_params=_cparams(1),
        name="pool_mixer",
    )(pbf, pbf, inv_c, inv_x, w_pool, pool_scale.reshape(1, pw))


def _route(sb, s):
    def rank_in_group(vals):
        ranks = []
        for i, vi in enumerate(vals):
            r = jnp.zeros(vi.shape, jnp.int32)
            for j, vj in enumerate(vals):
                if j == i:
                    continue
                ahead = (vj >= vi) if j < i else (vj > vi)
                r = r + jnp.where(ahead, 1, 0)
            ranks.append(r)
        return ranks

    best = None
    for g in range(N_GROUPS):
        vals = sb[g * EXPERTS_PER_GROUP:(g + 1) * EXPERTS_PER_GROUP]
        svals = s[g * EXPERTS_PER_GROUP:(g + 1) * EXPERTS_PER_GROUP]
        score = None
        for i in range(EXPERTS_PER_GROUP):
            for j in range(i + 1, EXPERTS_PER_GROUP):
                pair = vals[i] + vals[j]
                score = pair if score is None else jnp.maximum(score, pair)
        ranks = rank_in_group(vals)
        e1 = jnp.zeros(score.shape, jnp.int32)
        e2 = jnp.zeros(score.shape, jnp.int32)
        w1 = jnp.zeros(score.shape, F32)
        w2 = jnp.zeros(score.shape, F32)
        for i in range(EXPERTS_PER_GROUP):
            e1 = jnp.where(ranks[i] == 0, g * EXPERTS_PER_GROUP + i, e1)
            e2 = jnp.where(ranks[i] == 1, g * EXPERTS_PER_GROUP + i, e2)
            w1 = jnp.where(ranks[i] == 0, svals[i], w1)
            w2 = jnp.where(ranks[i] == 1, svals[i], w2)
        cand = (score, e1, e2, w1, w2)
        if best is None:
            best = cand
        else:
            take = cand[0] > best[0]
            best = tuple(jnp.where(take, c, b) for c, b in zip(cand, best))
    _, e1, e2, w1, w2 = best
    tot = w1 + w2
    return e1, e2, w1 / tot, w2 / tot


def _outproj_kernel(x_ref, pc_ref, px_ref, ac_ref, ax_ref, gate_ref, scale_ref, shift_ref, gain_ref,
                    wo_ref, wrt_ref, rb_ref, before_ref, below_ref, x1_ref, xs_ref, slot_ref, wt_ref, len_ref,
                    *, n_ctx_tiles, half):
    is_ctx = pl.program_id(0) < n_ctx_tiles
    pool = jnp.where(is_ctx, pc_ref[...], px_ref[...])
    att = jnp.where(is_ctx, ac_ref[...], ax_ref[...])
    y = _dot(pool, wo_ref[0:half, :]) + _dot(att, wo_ref[half:, :])
    x1 = x_ref[...] + gate_ref[...] * y
    x1_ref[...] = x1
    h2 = _norm_mod(x1, gain_ref[...], scale_ref[...], shift_ref[...])
    logits = _dot3_nt(wrt_ref[...], h2)
    s = jax.nn.sigmoid(logits)
    sb = s + rb_ref[...]
    rows = lambda a: [a[i:i + 1, :] for i in range(N_EXPERTS)]
    e1, e2, w1, w2 = _route(rows(sb), rows(s))
    wt_ref[0:1, :] = w1
    wt_ref[1:2, :] = w2
    ie = lax.broadcasted_iota(jnp.int32, logits.shape, 0)
    oh1 = jnp.where(ie == e1, 1.0, 0.0)
    oh2 = jnp.where(ie == e2, 1.0, 0.0)
    c1 = jnp.sum(oh1, axis=1, keepdims=True)
    c2 = jnp.sum(oh2, axis=1, keepdims=True)
    seg_len = jnp.ceil((c1 + c2) * (1.0 / SEG_ALIGN)) * SEG_ALIGN
    seg_len_l = jnp.broadcast_to(seg_len, len_ref.shape)
    seg_off = _dot(below_ref[...], seg_len_l.astype(BF16))[:, 0:1]
    before1 = _dot(oh1.astype(BF16), before_ref[...]) + seg_off
    before2 = _dot(oh2.astype(BF16), before_ref[...]) + (seg_off + c1)
    slot1 = jnp.sum(oh1 * before1, axis=0, keepdims=True).astype(jnp.int32)
    slot2 = jnp.sum(oh2 * before2, axis=0, keepdims=True).astype(jnp.int32)
    slot_ref[0:1, :] = slot1
    slot_ref[1:2, :] = slot2
    len_ref[...] = seg_len_l.astype(jnp.int32)
    si = lax.broadcasted_iota(jnp.int32, (xs_ref.shape[0], slot1.shape[1]), 0)
    perm = jnp.where(si == slot1, 1.0, 0.0) + jnp.where(si == slot2, 1.0, 0.0)
    xs_ref[...] = _dot(perm.astype(BF16), h2.astype(BF16)).astype(BF16)


def _outproj(xf, pool_c, pool_x, att_c, att_x, modl, gain2, w_out, w_router_t, router_bias,
             *, tile, mod_row, first_tile, n_tiles):
    t, d = xf.shape
    half = pool_c.shape[1]
    n_ctx_tiles = pool_c.shape[0] // tile
    cidx = lambda i: (jnp.minimum(i + first_tile, n_ctx_tiles - 1), 0)
    xidx = lambda i: (jnp.maximum(i + first_tile - n_ctx_tiles, 0), 0)
    mod_spec = lambda m: pl.BlockSpec((None, None, 1, d), lambda i: (mod_row(i + first_tile), m, 0, 0))
    full = lambda a: pl.BlockSpec(a.shape, lambda i: (0,) * a.ndim)
    tok = lambda w: pl.BlockSpec((tile, w), lambda i: (i, 0))
    lane = pl.BlockSpec((2, tile), lambda i: (0, i))
    t_out = n_tiles * tile
    before = jnp.asarray(np.triu(np.ones((tile, tile), np.float32), 1), BF16)
    below = jnp.asarray(np.tril(np.ones((N_EXPERTS, N_EXPERTS), np.float32), -1), BF16)
    xs_rows = _xs_rows(tile)
    return pl.pallas_call(
        functools.partial(_outproj_kernel, n_ctx_tiles=n_ctx_tiles - first_tile, half=half),
        grid=(n_tiles,),
        in_specs=[
            pl.BlockSpec((tile, d), lambda i: (i + first_tile, 0)),
            pl.BlockSpec((tile, half), cidx), pl.BlockSpec((tile, half), xidx),
            pl.BlockSpec((tile, half), cidx), pl.BlockSpec((tile, half), xidx),
            mod_spec(2), mod_spec(4), mod_spec(3), full(gain2), full(w_out), full(w_router_t),
            full(router_bias), full(before), full(below),
        ],
        out_specs=[tok(d), pl.BlockSpec((xs_rows, d), lambda i: (i, 0)), lane, lane,
                   pl.BlockSpec((None, N_EXPERTS, 128), lambda i: (i, 0, 0))],
        out_shape=[
            jax.ShapeDtypeStruct((t_out, d), F32),
            jax.ShapeDtypeStruct((n_tiles * xs_rows, d), BF16),
            jax.ShapeDtypeStruct((2, t_out), jnp.int32),
            jax.ShapeDtypeStruct((2, t_out), F32),
            jax.ShapeDtypeStruct((n_tiles, N_EXPERTS, 128), jnp.int32),
        ],
        compiler_params=_cparams(1),
        name="outproj_router",
    )(xf, pool_c, pool_x, att_c, att_x, modl, modl, modl, gain2, w_out, w_router_t, router_bias,
      before, below)


def _xs_rows(tile):
    return 2 * tile + N_EXPERTS * SEG_ALIGN


def _for_each_chunk(n, fn):
    for size in SEG_CHUNKS:
        offset = (n // (2 * size)) * (2 * size)
        pl.when((n & size) != 0)(functools.partial(fn, offset, size))


def _segmove_kernel(src_row, dst_row, seg_len, z_row, z_len, src_ref, dst_ref, zero_ref, sem, zsem):
    zero_ref[...] = jnp.zeros_like(zero_ref)

    def seg_copy(j, offset, size):
        s0 = pl.multiple_of(src_row[j] + offset, SEG_ALIGN)
        d0 = pl.multiple_of(dst_row[j] + offset, SEG_ALIGN)
        return pltpu.make_async_copy(src_ref.at[pl.ds(s0, size), :], dst_ref.at[pl.ds(d0, size), :], sem)

    def zero_copy(j, offset, size):
        d0 = pl.multiple_of(z_row[j] + offset, SEG_ALIGN)
        return pltpu.make_async_copy(zero_ref.at[pl.ds(0, size), :], dst_ref.at[pl.ds(d0, size), :], zsem)

    def over(n_items, lens, make, action):
        def body(j, carry):
            _for_each_chunk(lens[j], lambda offset, size: action(make(j, offset, size)))
            return carry
        lax.fori_loop(0, n_items, body, 0)

    over(z_row.shape[0], z_len, zero_copy, lambda c: c.start())
    over(src_row.shape[0], seg_len, seg_copy, lambda c: c.start())
    over(z_row.shape[0], z_len, zero_copy, lambda c: c.wait())
    over(src_row.shape[0], seg_len, seg_copy, lambda c: c.wait())


def _segmove(src, src_row, dst_row, seg_len, z_row, z_len, n_out_rows):
    d = src.shape[1]
    flat = lambda a: a.reshape(-1).astype(jnp.int32)
    return pl.pallas_call(
        _segmove_kernel,
        grid_spec=pltpu.PrefetchScalarGridSpec(
            num_scalar_prefetch=5,
            grid=(1,),
            in_specs=[pl.BlockSpec(memory_space=pl.ANY)],
            out_specs=pl.BlockSpec(memory_space=pl.ANY),
            scratch_shapes=[pltpu.VMEM((SEG_CHUNKS[0], d), src.dtype), pltpu.SemaphoreType.DMA(()),
                            pltpu.SemaphoreType.DMA(())],
        ),
        out_shape=jax.ShapeDtypeStruct((n_out_rows, d), src.dtype),
        compiler_params=_cparams(1),
        name="moe_segment_move",
    )(flat(src_row), flat(dst_row), flat(seg_len), flat(z_row), flat(z_len), src)


def _expert_kernel(be_ref, nu_ref, xb_ref, w1_ref, w3_ref, w2_ref, yb_ref):
    del be_ref
    used = pl.program_id(0) < nu_ref[0]

    @pl.when(used)
    def _():
        x = xb_ref[...]
        a = _dot(x, w1_ref[...])
        b = _dot(x, w3_ref[...])
        yb_ref[...] = _dot((_silu(a) * b).astype(BF16), w2_ref[...]).astype(BF16)

    @pl.when(jnp.logical_not(used))
    def _():
        yb_ref[...] = jnp.zeros_like(yb_ref)


def _experts(xb, block_e, n_used, w1, w3, w2):
    n_slots, d = xb.shape
    de = w1.shape[2]
    nb = n_slots // MOE_BM
    blk = lambda i, be, nu: (jnp.minimum(i, nu[0] - 1), 0)
    return pl.pallas_call(
        _expert_kernel,
        grid_spec=pltpu.PrefetchScalarGridSpec(
            num_scalar_prefetch=2,
            grid=(nb,),
            in_specs=[
                pl.BlockSpec((MOE_BM, d), blk),
                pl.BlockSpec((None, d, de), lambda i, be, nu: (be[i], 0, 0)),
                pl.BlockSpec((None, d, de), lambda i, be, nu: (be[i], 0, 0)),
                pl.BlockSpec((None, de, d), lambda i, be, nu: (be[i], 0, 0)),
            ],
            out_specs=pl.BlockSpec((MOE_BM, d), lambda i, be, nu: (i, 0)),
        ),
        out_shape=jax.ShapeDtypeStruct((n_slots, d), BF16),
        compiler_params=_cparams(1),
        name="moe_experts",
    )(block_e, n_used, xb, w1, w3, w2)


def _combine_kernel(ys_ref, x_ref, slot_ref, wt_ref, gate_ref, gainf_ref, o_ref, *, final_norm):
    slot = slot_ref[...]
    wt = wt_ref[...]
    li = lax.broadcasted_iota(jnp.int32, (slot.shape[0], ys_ref.shape[0]), 1)
    mix = jnp.where(li == slot[:, 0:1], wt[:, 0:1], 0.0) + jnp.where(li == slot[:, 1:2], wt[:, 1:2], 0.0)
    y = _dot(mix.astype(BF16), ys_ref[...])
    x2 = x_ref[...] + gate_ref[...] * y
    if final_norm:
        x2 = x2 * lax.rsqrt(jnp.mean(x2 * x2, axis=-1, keepdims=True) + EPS) * gainf_ref[...]
    o_ref[...] = x2


def _combine(x1, ys, slot_cols, wt_cols, modl, gain_f, *, tile, mod_row, first_tile, final_norm):
    t, d = x1.shape
    xs_rows = _xs_rows(tile)
    return pl.pallas_call(
        functools.partial(_combine_kernel, final_norm=final_norm),
        grid=(t // tile,),
        in_specs=[
            pl.BlockSpec((xs_rows, d), lambda i: (i, 0)),
            pl.BlockSpec((tile, d), lambda i: (i, 0)),
            pl.BlockSpec((tile, 2), lambda i: (i, 0)),
            pl.BlockSpec((tile, 2), lambda i: (i, 0)),
            pl.BlockSpec((None, None, 1, d), lambda i: (mod_row(i + first_tile), 5, 0, 0)),
            pl.BlockSpec((1, d), lambda i: (0, 0)),
        ],
        out_specs=pl.BlockSpec((tile, d), lambda i: (i, 0)),
        out_shape=jax.ShapeDtypeStruct((t, d), F32),
        compiler_params=_cparams(1),
        name="moe_combine",
    )(ys, x1, slot_cols, wt_cols, modl, gain_f)


def _slot_plan(seg_len, *, tile, n_slots):
    n_tiles = seg_len.shape[0]
    xs_rows = _xs_rows(tile)
    nb = n_slots // MOE_BM
    seg_off = jnp.cumsum(seg_len, axis=1) - seg_len
    tile_rows = jnp.sum(seg_len, axis=1)
    rows_e = jnp.sum(seg_len, axis=0)
    padded = (rows_e + MOE_BM - 1) // MOE_BM * MOE_BM
    pend = jnp.cumsum(padded)
    pstart = pend - padded
    xs_row = jnp.arange(n_tiles, dtype=jnp.int32)[:, None] * xs_rows + seg_off
    xb_row = pstart[None, :] + jnp.cumsum(seg_len, axis=0) - seg_len
    n_used = pend[-1] // MOE_BM
    blk = jnp.minimum(jnp.arange(nb, dtype=jnp.int32), n_used - 1)
    block_e = jnp.sum((pend[None, :] <= (blk * MOE_BM)[:, None]).astype(jnp.int32), axis=1)
    block_e = jnp.minimum(block_e, N_EXPERTS - 1).astype(jnp.int32)
    n_spare = nb - (2 * n_tiles * tile) // MOE_BM
    spare = n_used + jnp.arange(n_spare, dtype=jnp.int32)
    xb_zero_row = jnp.concatenate([pstart + rows_e, jnp.minimum(spare, nb - 1) * MOE_BM])
    xb_zero_len = jnp.concatenate([padded - rows_e, jnp.where(spare < nb, MOE_BM, 0)])
    xs_zero_row = jnp.arange(n_tiles, dtype=jnp.int32) * xs_rows + tile_rows
    xs_zero_len = xs_rows - tile_rows
    return (xs_row, xb_row, xb_zero_row, xb_zero_len, xs_zero_row, xs_zero_len, block_e,
            n_used.astype(jnp.int32).reshape(1))


def kernel(x, c, ctx, c_ctx, w_mod, b_mod, norm1, norm2, w_in, w_gk2, b_gk, w_pool, pool_scale, gla_gain,
           w_out, w_router, router_bias, w1, w3, w2, norm_f):
    batch, seq, d = x.shape
    ctx_len = ctx.shape[1]
    depth = w_mod.shape[0]
    n_ctx = batch * ctx_len
    t_all = n_ctx + batch * seq
    def row_tile(limit):
        return max(t for t in (1024, 512, 256) if t <= limit and n_ctx % t == 0 and seq % t == 0)

    tile_a = row_tile(1024)
    tile_b = row_tile(512)

    def mod_row_for(tile):
        nct, per_b = n_ctx // tile, seq // tile
        return lambda i: jnp.where(i < nct, 0, 1 + (i - nct) // per_b)

    xf = jnp.concatenate([ctx.reshape(n_ctx, d), x.reshape(batch * seq, d)], axis=0)

    cvecs = jnp.zeros((16, d), F32).at[0].set(c_ctx).at[1:1 + batch].set(c)
    mods = _modulation(cvecs, w_mod, b_mod).reshape(depth, 16, N_MOD, 1, d)

    n_main = w_in.shape[2] - 2 * GATE_RANK
    w_main = w_in[:, :, :n_main].astype(BF16)
    w_r = jnp.pad(w_in[:, :, n_main:], ((0, 0), (0, 0), (0, 128 - 2 * GATE_RANK))).astype(BF16)
    w2g = jnp.zeros((depth, 128, 2 * GLA_QK), F32)
    w2g = w2g.at[:, 0:GATE_RANK, 0:GLA_QK].set(w_gk2[:, 0])
    w2g = w2g.at[:, GATE_RANK:2 * GATE_RANK, GLA_QK:].set(w_gk2[:, 1])
    bgk = b_gk.reshape(depth, 1, 2 * GLA_QK)
    w_out_b = w_out.astype(BF16)
    w_pool_b = w_pool.astype(BF16)
    w1b, w3b, w2b = w1.astype(BF16), w3.astype(BF16), w2.astype(BF16)
    w_router_t = w_router.T
    rbias = router_bias.reshape(N_EXPERTS, 1)
    g1 = lambda a: a.reshape(1, -1)

    out = None
    for l in range(depth):
        last = l == depth - 1
        pbf, lg = _inproj(xf, mods[l], g1(norm1[l]), w_main[l], w_r[l], w2g[l], bgk[l],
                          tile=tile_a, mod_row=mod_row_for(tile_a))
        att_c, att_x = _gla(pbf, lg, gla_gain[l], batch=batch, ctx_len=ctx_len, seq=seq)
        pool_c, pool_x = _pool(pbf, w_pool_b[l], pool_scale[l], batch=batch, ctx_len=ctx_len, seq=seq)
        first_tile = n_ctx // tile_b if last else 0
        n_tiles = t_all // tile_b - first_tile
        x1, xs, slots, wts, seg_len = _outproj(
            xf, pool_c, pool_x, att_c, att_x, mods[l], g1(norm2[l]), w_out_b[l], w_router_t, rbias,
            tile=tile_b, mod_row=mod_row_for(tile_b), first_tile=first_tile, n_tiles=n_tiles)
        seg_len = seg_len[:, :, 0]
        max_rows = n_tiles * (2 * tile_b + N_EXPERTS * (SEG_ALIGN - 1)) + N_EXPERTS * (MOE_BM - 1)
        n_slots = -(-max_rows // MOE_BM) * MOE_BM
        (xs_row, xb_row, xb_zero_row, xb_zero_len, xs_zero_row, xs_zero_len, block_e, n_used) = _slot_plan(
            seg_len, tile=tile_b, n_slots=n_slots)
        xb = _segmove(xs, xs_row, xb_row, seg_len, xb_zero_row, xb_zero_len, n_slots)
        yb = _experts(xb, block_e, n_used, w1b[l], w3b[l], w2b[l])
        ys = _segmove(yb, xb_row, xs_row, seg_len, xs_zero_row, xs_zero_len, xs.shape[0])
        res = _combine(x1, ys, slots.T, wts.T, mods[l], g1(norm_f), tile=tile_b,
                       mod_row=mod_row_for(tile_b), first_tile=first_tile, final_norm=last)
        if last:
            out = res
        else:
            xf = res
    return out.reshape(batch, seq, d)
```

```python
import functools

import numpy as np
import jax
import jax.numpy as jnp
from jax import lax
from jax.experimental import pallas as pl
from jax.experimental.pallas import tpu as pltpu

F32 = jnp.float32
BF16 = jnp.bfloat16

GRID_W = 64
POOL_GROUPS = 4
POOL_WINDOWS = (2, 4, 8, 16)
GLA_HEADS = 4
GLA_DK = 64
GLA_DV = 128
GLA_QK = GLA_HEADS * GLA_DK
GATE_RANK = 16
GATE_NORM = 16.0
N_EXPERTS = 16
N_GROUPS = 4
EXPERTS_PER_GROUP = 4
EPS = 1e-6
N_MOD = 6

VMEM_LIMIT_BYTES = 56 * 1024 * 1024
SPAN = 256
SUB = 32
N_SUB = SPAN // SUB
LOG2_E = 1.4426950408889634
EXP2_CLAMP = 115.0
DIAG_W = 128
GLA_KW = DIAG_W + N_SUB * GLA_DK
MOE_BM = 256
SEG_ALIGN = 16
SEG_CHUNKS = (512, 256, 128, 64, 32, 16)


def _cparams(n_axes):
    return pltpu.CompilerParams(
        dimension_semantics=("arbitrary",) * n_axes, vmem_limit_bytes=VMEM_LIMIT_BYTES)


def _split(a):
    hi = a.astype(BF16)
    lo = (a - hi.astype(F32)).astype(BF16)
    return hi, lo


def _dot(a, b):
    return jnp.dot(a, b, preferred_element_type=F32)


def _dot_nt(a, b):
    return lax.dot_general(a, b, (((1,), (1,)), ((), ())), preferred_element_type=F32)


def _dot_tn(a, b):
    return lax.dot_general(a, b, (((0,), (0,)), ((), ())), preferred_element_type=F32)


def _dot3(a, b):
    ah, al = _split(a)
    bh, bl = _split(b)
    return _dot(ah, bh) + (_dot(ah, bl) + _dot(al, bh))


def _dot3_nt(a, b):
    ah, al = _split(a)
    bh, bl = _split(b)
    return _dot_nt(ah, bh) + (_dot_nt(ah, bl) + _dot_nt(al, bh))


def _dot_exact_lhs(m_bf16, b):
    b0 = b.astype(BF16)
    r1 = b - b0.astype(F32)
    b1 = r1.astype(BF16)
    b2 = (r1 - b1.astype(F32)).astype(BF16)
    return _dot(m_bf16, b0) + (_dot(m_bf16, b1) + _dot(m_bf16, b2))


def _silu(x):
    return x * jax.nn.sigmoid(x)


def _mod_kernel(c_ref, w_ref, b_ref, o_ref):
    o_ref[0] = _dot3(_silu(c_ref[...]), w_ref[0]) + b_ref[0]


def _modulation(cvecs, w_mod, b_mod):
    depth, d, _ = w_mod.shape
    rows = cvecs.shape[0]
    return pl.pallas_call(
        _mod_kernel,
        grid=(depth, N_MOD),
        in_specs=[
            pl.BlockSpec((rows, d), lambda l, j: (0, 0)),
            pl.BlockSpec((1, d, d), lambda l, j: (l, 0, j)),
            pl.BlockSpec((1, 1, d), lambda l, j: (l, 0, j)),
        ],
        out_specs=pl.BlockSpec((1, rows, d), lambda l, j: (l, 0, j)),
        out_shape=jax.ShapeDtypeStruct((depth, rows, N_MOD * d), F32),
        compiler_params=_cparams(2),
        name="adaln_vectors",
    )(cvecs, w_mod, b_mod.reshape(depth, 1, N_MOD * d))


def _norm_mod(x, gain, scale, shift):
    y = x * lax.rsqrt(jnp.mean(x * x, axis=-1, keepdims=True) + EPS)
    return (y * gain) * (1.0 + scale) + shift


def _inproj_kernel(x_ref, scale_ref, shift_ref, gain_ref, w_ref, wr_ref, w2_ref, bgk_ref,
                   p_ref, lg_ref, *, n_chunk):
    h = _norm_mod(x_ref[...], gain_ref[...], scale_ref[...], shift_ref[...])
    hb = h.astype(BF16)
    width = p_ref.shape[1] // n_chunk
    for n in range(n_chunk):
        cols = slice(n * width, (n + 1) * width)
        p_ref[:, cols] = _dot(hb, w_ref[:, cols]).astype(BF16)
    r = _dot(hb, wr_ref[...])
    z = _dot3(r, w2_ref[...]) + bgk_ref[...]
    log_sig = jnp.minimum(z, 0.0) - jnp.log(1.0 + jnp.exp(-jnp.abs(z)))
    lg_ref[...] = log_sig * (LOG2_E / GATE_NORM)


def _inproj(xf, modl, gain, w_main, w_r, w2, bgk, *, tile, mod_row):
    t, d = xf.shape
    n_main = w_main.shape[1]
    n_gate = w2.shape[1]
    mod_spec = lambda m: pl.BlockSpec((None, None, 1, d), lambda i: (mod_row(i), m, 0, 0))
    full = lambda a: pl.BlockSpec(a.shape, lambda i: (0,) * a.ndim)
    return pl.pallas_call(
        functools.partial(_inproj_kernel, n_chunk=4),
        grid=(t // tile,),
        in_specs=[
            pl.BlockSpec((tile, d), lambda i: (i, 0)),
            mod_spec(1), mod_spec(0), full(gain), full(w_main), full(w_r), full(w2), full(bgk),
        ],
        out_specs=[
            pl.BlockSpec((tile, n_main), lambda i: (i, 0)),
            pl.BlockSpec((tile, n_gate), lambda i: (i, 0)),
        ],
        out_shape=[
            jax.ShapeDtypeStruct((t, n_main), BF16),
            jax.ShapeDtypeStruct((t, n_gate), F32),
        ],
        compiler_params=_cparams(1),
        name="norm_inproj",
    )(xf, modl, modl, gain, w_main, w_r, w2, bgk)


def _store_heads(dst_ref, rows, col0, val):
    half = (col0 // GLA_DK) & 1
    plain = val.astype(BF16)
    turned = pltpu.roll(val, GLA_DK, axis=1).astype(BF16)
    for h in range(GLA_HEADS):
        if (h & 1) == half:
            src = plain[:, h * GLA_DK:(h + 1) * GLA_DK]
        else:
            hh = (h + 1) % GLA_HEADS
            src = turned[:, hh * GLA_DK:(hh + 1) * GLA_DK]
        dst_ref[h, rows, col0:col0 + GLA_DK] = src


def _gla_span(q_ref, k_ref, v_ref, lg_ref, o_ref, row0, st_ref, qh_ref, kh_ref, b_ref, qf_ref, kf_ref,
              tri_ref, dmask_ref, smask_ref, *, rev):
    rows = pl.ds(row0, SPAN)
    dcol = slice(rev * GLA_QK, (rev + 1) * GLA_QK)
    b_ref[...] = _dot_exact_lhs(tri_ref[rev], lg_ref[rows, dcol])
    qf_ref[...] = q_ref[rows, :].astype(F32) * (GLA_DK ** -0.5)
    kf_ref[...] = k_ref[rows, :].astype(F32)
    e_tot = b_ref[0:1, :] if rev else b_ref[SPAN - 1:SPAN, :]

    for s in range(N_SUB - 1):
        j = s + 1 if rev else s
        ref_row = j * SUB if rev else j * SUB + SUB - 1
        e_j = b_ref[ref_row:ref_row + 1, :]
        blk = slice(j * SUB, (j + 1) * SUB)
        qrows = slice(0, j * SUB) if rev else slice((j + 1) * SUB, SPAN)
        col0 = DIAG_W + s * GLA_DK
        _store_heads(qh_ref, qrows, col0, qf_ref[qrows, :] * jnp.exp2(b_ref[qrows, :] - e_j))
        _store_heads(kh_ref, blk, col0, kf_ref[blk, :] * jnp.exp2(e_j - b_ref[blk, :]))
    for i in range(N_SUB):
        blk = slice(i * SUB, (i + 1) * SUB)
        mid = i * SUB + SUB // 2
        e_m = b_ref[mid:mid + 1, :]
        _store_heads(qh_ref, blk, 0, qf_ref[blk, :] * jnp.exp2(jnp.minimum(b_ref[blk, :] - e_m, EXP2_CLAMP)))
        _store_heads(kh_ref, blk, 0, kf_ref[blk, :] * jnp.exp2(jnp.minimum(e_m - b_ref[blk, :], EXP2_CLAMP)))

    st = st_ref[...]
    qd = (qf_ref[...] * jnp.exp2(b_ref[...])).astype(BF16)
    o_state = _dot_nt(qd, st.astype(BF16))
    in_block = dmask_ref[rev] != 0.0
    for h in range(GLA_HEADS):
        vc = slice(h * GLA_DV, (h + 1) * GLA_DV)
        a_diag = _dot_nt(qh_ref[h, :, 0:DIAG_W], kh_ref[h, :, 0:DIAG_W])
        a_cross = _dot_nt(qh_ref[h, :, DIAG_W:], kh_ref[h, :, DIAG_W:])
        a = (a_cross + jnp.where(in_block, a_diag, 0.0)).astype(BF16)
        o_ref[rows, vc] += _dot(a, v_ref[rows, vc]) + o_state[:, vc]
    kd =(kf_ref[...] * jnp.exp2(e_tot - b_ref[...])).astype(BF16)
    upd = _dot_tn(v_ref[rows, :], kd)
    st_ref[...] = st * jnp.exp2(e_tot) + upd * smask_ref[...]


def _gla_finish(o_ref, g_ref, gain, out_ref, n_rows):
    def body(c, carry):
        rows = pl.ds(pl.multiple_of(c * SPAN, SPAN), SPAN)
        for h in range(GLA_HEADS):
            vc = slice(h * GLA_DV, (h + 1) * GLA_DV)
            o = o_ref[rows, vc]
            y = o * lax.rsqrt(jnp.mean(o * o, axis=-1, keepdims=True) + EPS) * gain
            g = g_ref[rows, vc].astype(F32)
            out_ref[rows, vc] = (y * _silu(g)).astype(BF16)
        return carry
    lax.fori_loop(0, n_rows // SPAN, body, 0)


def _gla_kernel(qc, kc, vc, gc, lgc, qx, kx, vx, gx, lgx, gain_ref, tri, dmask, smask, outc, outx,
                oc, ox, *per_direction):
    n_c = qc.shape[0] // SPAN
    n_x = qx.shape[0] // SPAN
    names = ("st_ref", "qh_ref", "kh_ref", "b_ref", "qf_ref", "kf_ref")
    scratch = [dict(zip(names, per_direction[r * len(names):(r + 1) * len(names)])) for r in (0, 1)]
    for ref in (oc, ox) + tuple(s[n] for s in scratch for n in ("st_ref", "qh_ref", "kh_ref")):
        ref[...] = jnp.zeros_like(ref)
    spans = [functools.partial(_gla_span, tri_ref=tri, dmask_ref=dmask, smask_ref=smask, rev=rev,
                               **scratch[rev]) for rev in (0, 1)]
    for s in range(n_c):
        spans[0](qc, kc, vc, lgc, oc, s * SPAN)
        spans[1](qc, kc, vc, lgc, oc, (n_c - 1 - s) * SPAN)

    def body(s, carry):
        spans[0](qx, kx, vx, lgx, ox, pl.multiple_of(s * SPAN, SPAN))
        spans[1](qx, kx, vx, lgx, ox, pl.multiple_of((n_x - 1 - s) * SPAN, SPAN))
        return carry
    lax.fori_loop(0, n_x, body, 0)
    gain = gain_ref[...]
    _gla_finish(oc, gc, gain, outc, qc.shape[0])
    _gla_finish(ox, gx, gain, outx, qx.shape[0])


def _gla(pbf, lg, gla_gain, *, batch, ctx_len, seq):
    nxb = (batch * ctx_len) // seq
    assert nxb * seq == batch * ctx_len
    hv = GLA_HEADS * GLA_DV
    cspec = lambda w, j: pl.BlockSpec((ctx_len, w), lambda b: (b, j))
    xspec = lambda w, j: pl.BlockSpec((seq, w), lambda b: (nxb + b, j))
    full = lambda a: pl.BlockSpec(a.shape, lambda b: (0,) * a.ndim)
    pos = np.arange(SPAN)
    lower = pos[None, :] <= pos[:, None]
    same = (pos[None, :] // SUB) == (pos[:, None] // SUB)
    tri = jnp.asarray(np.stack([lower, lower.T]).astype(np.float32), BF16)
    dmask = jnp.asarray(np.stack([lower & same, lower.T & same]).astype(np.float32))
    smask = jnp.asarray(((np.arange(hv)[:, None] // GLA_DV) == (np.arange(GLA_QK)[None, :] // GLA_DK))
                        .astype(np.float32))
    return pl.pallas_call(
        _gla_kernel,
        grid=(batch,),
        in_specs=[
            cspec(GLA_QK, 2), cspec(GLA_QK, 3), cspec(hv, 2), cspec(hv, 3), cspec(2 * GLA_QK, 0),
            xspec(GLA_QK, 2), xspec(GLA_QK, 3), xspec(hv, 2), xspec(hv, 3), xspec(2 * GLA_QK, 0),
            pl.BlockSpec((1, GLA_DV), lambda b: (0, 0)), full(tri), full(dmask), full(smask),
        ],
        out_specs=[
            pl.BlockSpec((ctx_len, hv), lambda b: (b, 0)),
            pl.BlockSpec((seq, hv), lambda b: (b, 0)),
        ],
        out_shape=[
            jax.ShapeDtypeStruct((batch * ctx_len, hv), BF16),
            jax.ShapeDtypeStruct((batch * seq, hv), BF16),
        ],
        scratch_shapes=[
            pltpu.VMEM((ctx_len, hv), F32),
            pltpu.VMEM((seq, hv), F32),
        ] + 2 * [
            pltpu.VMEM((hv, GLA_QK), F32),
            pltpu.VMEM((GLA_HEADS, SPAN, GLA_KW), BF16),
            pltpu.VMEM((GLA_HEADS, SPAN, GLA_KW), BF16),
            pltpu.VMEM((SPAN, GLA_QK), F32),
            pltpu.VMEM((SPAN, GLA_QK), F32),
            pltpu.VMEM((SPAN, GLA_QK), F32),
        ],
        compiler_params=_cparams(1),
        name="gla_bidirectional",
    )(pbf, pbf, pbf, pbf, lg, pbf, pbf, pbf, pbf, lg, gla_gain.reshape(1, GLA_DV), tri, dmask, smask)


def _window_offsets(w):
    return range(-(w // 2), w - w // 2)


def _band(w, period):
    ri = lax.broadcasted_iota(jnp.int32, (SPAN, SPAN), 0)
    ci = lax.broadcasted_iota(jnp.int32, (SPAN, SPAN), 1)
    d = ci - ri
    shift = period.bit_length() - 1
    assert period == 1 << shift
    same_row = (ri >> shift) == (ci >> shift)
    inside = jnp.where(d >= -(w // 2), jnp.where(d <= w - w // 2 - 1, 1.0, 0.0), 0.0)
    return jnp.where(same_row, inside, 0.0).astype(BF16)


def _pool_kernel(uc_ref, ux_ref, invc_ref, invx_ref, wp_ref, ps_ref, outc_ref, outx_ref, pad_ref,
                 *, pad_rows):
    n_x = ux_ref.shape[0]
    gw = wp_ref.shape[1]
    pad_ref[0:pad_rows, :] = jnp.zeros((pad_rows, gw), F32)
    pad_ref[pad_rows + n_x:, :] = jnp.zeros((pad_rows, gw), F32)
    for g, w in enumerate(POOL_WINDOWS):
        gc = slice(g * gw, (g + 1) * gw)
        wp = wp_ref[g]
        ps = ps_ref[:, gc]
        band_c = _band(w, SPAN)
        for t in range(uc_ref.shape[0] // SPAN):
            rows = slice(t * SPAN, (t + 1) * SPAN)
            u = uc_ref[rows, gc].astype(F32)
            pooled = _dot_exact_lhs(band_c, u) * invc_ref[g, rows, :] - u
            outc_ref[rows, gc] = (_dot(pooled.astype(BF16), wp) * ps).astype(BF16)
        pad_ref[pad_rows:pad_rows + n_x, :] = ux_ref[:, gc].astype(F32)
        band_x = _band(w, GRID_W)

        def body(t, carry):
            r0 = pl.multiple_of(t * SPAN, SPAN)
            acc = jnp.zeros((SPAN, gw), F32)
            for dr in _window_offsets(w):
                acc = acc + pad_ref[pl.ds(pad_rows + r0 + dr * GRID_W, SPAN), :]
            rows = pl.ds(r0, SPAN)
            u = pad_ref[pl.ds(pad_rows + r0, SPAN), :]
            pooled = _dot_exact_lhs(band_x, acc) * invx_ref[g, rows, :] - u
            outx_ref[rows, gc] = (_dot(pooled.astype(BF16), wp) * ps).astype(BF16)
            return carry
        lax.fori_loop(0, n_x // SPAN, body, 0)


def _inv_counts(length, rows_of):
    out = []
    for w in POOL_WINDOWS:
        cnt = np.ones((length,), np.float64)
        for axis_len, coord in rows_of(length):
            lo = np.clip(coord - w // 2, 0, axis_len)
            hi = np.clip(coord + w - w // 2, 0, axis_len)
            cnt = cnt * (hi - lo)
        out.append(1.0 / cnt)
    return np.broadcast_to(np.stack(out)[:, :, None], (len(POOL_WINDOWS), length, 128)).astype(np.float32)


def _pool(pbf, w_pool, pool_scale, *, batch, ctx_len, seq):
    nxb = (batch * ctx_len) // seq
    pw = w_pool.shape[0] * w_pool.shape[1]
    pad_rows = (max(POOL_WINDOWS) // 2) * GRID_W
    inv_c = jnp.asarray(_inv_counts(ctx_len, lambda n: [(n, np.arange(n))]))
    inv_x = jnp.asarray(_inv_counts(
        seq, lambda n: [(n // GRID_W, np.arange(n) // GRID_W), (GRID_W, np.arange(n) % GRID_W)]))
    full = lambda a: pl.BlockSpec(a.shape, lambda b: (0,) * a.ndim)
    return pl.pallas_call(
        functools.partial(_pool_kernel, pad_rows=pad_rows),
        grid=(batch,),
        in_specs=[
            pl.BlockSpec((ctx_len, pw), lambda b: (b, 0)),
            pl.BlockSpec((seq, pw), lambda b: (nxb + b, 0)),
            full(inv_c), full(inv_x), full(w_pool),
            pl.BlockSpec((1, pw), lambda b: (0, 0)),
        ],
        out_specs=[
            pl.BlockSpec((ctx_len, pw), lambda b: (b, 0)),
            pl.BlockSpec((seq, pw), lambda b: (b, 0)),
        ],
        out_shape=[
            jax.ShapeDtypeStruct((batch * ctx_len, pw), BF16),
            jax.ShapeDtypeStruct((batch * seq, pw), BF16),
        ],
        scratch_shapes=[pltpu.VMEM((seq + 2 * pad_rows, w_pool.shape[1]), F32)],
        compiler_params=_cparams(1),
        name="pool_mixer",
    )(pbf, pbf, inv_c, inv_x, w_pool, pool_scale.reshape(1, pw))


def _route(sb, s):
    def rank_in_group(vals):
        ranks = []
        for i, vi in enumerate(vals):
            r = jnp.zeros(vi.shape, jnp.int32)
            for j, vj in enumerate(vals):
                if j == i:
                    continue
                ahead = (vj >= vi) if j < i else (vj > vi)
                r = r + jnp.where(ahead, 1, 0)
            ranks.append(r)
        return ranks

    best = None
    for g in range(N_GROUPS):
        vals = sb[g * EXPERTS_PER_GROUP:(g + 1) * EXPERTS_PER_GROUP]
        svals = s[g * EXPERTS_PER_GROUP:(g + 1) * EXPERTS_PER_GROUP]
        score = None
        for i in range(EXPERTS_PER_GROUP):
            for j in range(i + 1, EXPERTS_PER_GROUP):
                pair = vals[i] + vals[j]
                score = pair if score is None else jnp.maximum(score, pair)
        ranks = rank_in_group(vals)
        e1 = jnp.zeros(score.shape, jnp.int32)
        e2 = jnp.zeros(score.shape, jnp.int32)
        w1 = jnp.zeros(score.shape, F32)
        w2 = jnp.zeros(score.shape, F32)
        for i in range(EXPERTS_PER_GROUP):
            e1 = jnp.where(ranks[i] == 0, g * EXPERTS_PER_GROUP + i, e1)
            e2 = jnp.where(ranks[i] == 1, g * EXPERTS_PER_GROUP + i, e2)
            w1 = jnp.where(ranks[i] == 0, svals[i], w1)
            w2 = jnp.where(ranks[i] == 1, svals[i], w2)
        cand = (score, e1, e2, w1, w2)
        if best is None:
            best = cand
        else:
            take = cand[0] > best[0]
            best = tuple(jnp.where(take, c, b) for c, b in zip(cand, best))
    _, e1, e2, w1, w2 = best
    tot = w1 + w2
    return e1, e2, w1 / tot, w2 / tot


def _outproj_kernel(x_ref, pc_ref, px_ref, ac_ref, ax_ref, gate_ref, scale_ref, shift_ref, gain_ref,
                    wo_ref, wrt_ref, rb_ref, before_ref, below_ref, x1_ref, xs_ref, slot_ref, wt_ref, len_ref,
                    *, n_ctx_tiles, half):
    is_ctx = pl.program_id(0) < n_ctx_tiles
    pool = jnp.where(is_ctx, pc_ref[...], px_ref[...])
    att = jnp.where(is_ctx, ac_ref[...], ax_ref[...])
    y = _dot(pool, wo_ref[0:half, :]) + _dot(att, wo_ref[half:, :])
    x1 = x_ref[...] + gate_ref[...] * y
    x1_ref[...] = x1
    h2 = _norm_mod(x1, gain_ref[...], scale_ref[...], shift_ref[...])
    logits = _dot3_nt(wrt_ref[...], h2)
    s = jax.nn.sigmoid(logits)
    sb = s + rb_ref[...]
    rows = lambda a: [a[i:i + 1, :] for i in range(N_EXPERTS)]
    e1, e2, w1, w2 = _route(rows(sb), rows(s))
    wt_ref[0:1, :] = w1
    wt_ref[1:2, :] = w2
    ie = lax.broadcasted_iota(jnp.int32, logits.shape, 0)
    oh1 = jnp.where(ie == e1, 1.0, 0.0)
    oh2 = jnp.where(ie == e2, 1.0, 0.0)
    c1 = jnp.sum(oh1, axis=1, keepdims=True)
    c2 = jnp.sum(oh2, axis=1, keepdims=True)
    seg_len = jnp.ceil((c1 + c2) * (1.0 / SEG_ALIGN)) * SEG_ALIGN
    seg_len_l = jnp.broadcast_to(seg_len, len_ref.shape)
    seg_off = _dot(below_ref[...], seg_len_l.astype(BF16))[:, 0:1]
    before1 = _dot(oh1.astype(BF16), before_ref[...]) + seg_off
    before2 = _dot(oh2.astype(BF16), before_ref[...]) + (seg_off + c1)
    slot1 = jnp.sum(oh1 * before1, axis=0, keepdims=True).astype(jnp.int32)
    slot2 = jnp.sum(oh2 * before2, axis=0, keepdims=True).astype(jnp.int32)
    slot_ref[0:1, :] = slot1
    slot_ref[1:2, :] = slot2
    len_ref[...] = seg_len_l.astype(jnp.int32)
    si = lax.broadcasted_iota(jnp.int32, (xs_ref.shape[0], slot1.shape[1]), 0)
    perm = jnp.where(si == slot1, 1.0, 0.0) + jnp.where(si == slot2, 1.0, 0.0)
    xs_ref[...] = _dot(perm.astype(BF16), h2.astype(BF16)).astype(BF16)


def _outproj(xf, pool_c, pool_x, att_c, att_x, modl, gain2, w_out, w_router_t, router_bias,
             *, tile, mod_row, first_tile, n_tiles):
    t, d = xf.shape
    half = pool_c.shape[1]
    n_ctx_tiles = pool_c.shape[0] // tile
    cidx = lambda i: (jnp.minimum(i + first_tile, n_ctx_tiles - 1), 0)
    xidx = lambda i: (jnp.maximum(i + first_tile - n_ctx_tiles, 0), 0)
    mod_spec = lambda m: pl.BlockSpec((None, None, 1, d), lambda i: (mod_row(i + first_tile), m, 0, 0))
    full = lambda a: pl.BlockSpec(a.shape, lambda i: (0,) * a.ndim)
    tok = lambda w: pl.BlockSpec((tile, w), lambda i: (i, 0))
    lane = pl.BlockSpec((2, tile), lambda i: (0, i))
    t_out = n_tiles * tile
    before = jnp.asarray(np.triu(np.ones((tile, tile), np.float32), 1), BF16)
    below = jnp.asarray(np.tril(np.ones((N_EXPERTS, N_EXPERTS), np.float32), -1), BF16)
    xs_rows = _xs_rows(tile)
    return pl.pallas_call(
        functools.partial(_outproj_kernel, n_ctx_tiles=n_ctx_tiles - first_tile, half=half),
        grid=(n_tiles,),
        in_specs=[
            pl.BlockSpec((tile, d), lambda i: (i + first_tile, 0)),
            pl.BlockSpec((tile, half), cidx), pl.BlockSpec((tile, half), xidx),
            pl.BlockSpec((tile, half), cidx), pl.BlockSpec((tile, half), xidx),
            mod_spec(2), mod_spec(4), mod_spec(3), full(gain2), full(w_out), full(w_router_t),
            full(router_bias), full(before), full(below),
        ],
        out_specs=[tok(d), pl.BlockSpec((xs_rows, d), lambda i: (i, 0)), lane, lane,
                   pl.BlockSpec((None, N_EXPERTS, 128), lambda i: (i, 0, 0))],
        out_shape=[
            jax.ShapeDtypeStruct((t_out, d), F32),
            jax.ShapeDtypeStruct((n_tiles * xs_rows, d), BF16),
            jax.ShapeDtypeStruct((2, t_out), jnp.int32),
            jax.ShapeDtypeStruct((2, t_out), F32),
            jax.ShapeDtypeStruct((n_tiles, N_EXPERTS, 128), jnp.int32),
        ],
        compiler_params=_cparams(1),
        name="outproj_router",
    )(xf, pool_c, pool_x, att_c, att_x, modl, modl, modl, gain2, w_out, w_router_t, router_bias,
      before, below)


def _xs_rows(tile):
    return 2 * tile + N_EXPERTS * SEG_ALIGN


def _for_each_chunk(n, fn):
    for size in SEG_CHUNKS:
        offset = (n // (2 * size)) * (2 * size)
        pl.when((n & size) != 0)(functools.partial(fn, offset, size))


def _dispatch_kernel(seg_off, dst_row, seg_len, z_row, z_len, xs_ref, xb_ref, zero_ref, sem, zsem):
    i = pl.program_id(0)

    def zero_copy(j, offset, size):
        d0 = pl.multiple_of(z_row[j] + offset, SEG_ALIGN)
        return pltpu.make_async_copy(zero_ref.at[pl.ds(0, size), :], xb_ref.at[pl.ds(d0, size), :], zsem)

    def zero_fill(action):
        def body(j, carry):
            _for_each_chunk(z_len[j], lambda offset, size: action(zero_copy(j, offset, size)))
            return carry
        lax.fori_loop(0, z_row.shape[0], body, 0)

    @pl.when(i == 0)
    def _():
        zero_ref[...] = jnp.zeros_like(zero_ref)
        zero_fill(lambda c: c.start())

    def seg_copy(e, offset, size):
        j = i * N_EXPERTS + e
        s0 = pl.multiple_of(seg_off[j] + offset, SEG_ALIGN)
        d0 = pl.multiple_of(dst_row[j] + offset, SEG_ALIGN)
        return pltpu.make_async_copy(xs_ref.at[pl.ds(s0, size), :], xb_ref.at[pl.ds(d0, size), :], sem)

    for action in (lambda c: c.start(), lambda c: c.wait()):
        for e in range(N_EXPERTS):
            _for_each_chunk(seg_len[i * N_EXPERTS + e],
                            lambda offset, size, e=e: action(seg_copy(e, offset, size)))

    pl.when(i == 0)(lambda: zero_fill(lambda c: c.wait()))


def _dispatch(xs, seg_off, dst_row, seg_len, z_row, z_len, n_slots, *, tile):
    d = xs.shape[1]
    xs_rows = _xs_rows(tile)
    flat = lambda a: a.reshape(-1).astype(jnp.int32)
    return pl.pallas_call(
        _dispatch_kernel,
        grid_spec=pltpu.PrefetchScalarGridSpec(
            num_scalar_prefetch=5,
            grid=(xs.shape[0] // xs_rows,),
            in_specs=[pl.BlockSpec((xs_rows, d), lambda i, *_: (i, 0))],
            out_specs=pl.BlockSpec(memory_space=pl.ANY),
            scratch_shapes=[pltpu.VMEM((SEG_CHUNKS[0], d), xs.dtype), pltpu.SemaphoreType.DMA(()),
                            pltpu.SemaphoreType.DMA(())],
        ),
        out_shape=jax.ShapeDtypeStruct((n_slots, d), xs.dtype),
        compiler_params=_cparams(1),
        name="moe_dispatch",
    )(flat(seg_off), flat(dst_row), flat(seg_len), flat(z_row), flat(z_len), xs)


def _expert_kernel(be_ref, nu_ref, xb_ref, w1_ref, w3_ref, w2_ref, yb_ref):
    del be_ref
    used = pl.program_id(0) < nu_ref[0]

    @pl.when(used)
    def _():
        x = xb_ref[...]
        a = _dot(x, w1_ref[...])
        b = _dot(x, w3_ref[...])
        yb_ref[...] = _dot((_silu(a) * b).astype(BF16), w2_ref[...]).astype(BF16)

    @pl.when(jnp.logical_not(used))
    def _():
        yb_ref[...] = jnp.zeros_like(yb_ref)


def _experts(xb, block_e, n_used, w1, w3, w2):
    n_slots, d = xb.shape
    de = w1.shape[2]
    nb = n_slots // MOE_BM
    blk = lambda i, be, nu: (jnp.minimum(i, nu[0] - 1), 0)
    return pl.pallas_call(
        _expert_kernel,
        grid_spec=pltpu.PrefetchScalarGridSpec(
            num_scalar_prefetch=2,
            grid=(nb,),
            in_specs=[
                pl.BlockSpec((MOE_BM, d), blk),
                pl.BlockSpec((None, d, de), lambda i, be, nu: (be[i], 0, 0)),
                pl.BlockSpec((None, d, de), lambda i, be, nu: (be[i], 0, 0)),
                pl.BlockSpec((None, de, d), lambda i, be, nu: (be[i], 0, 0)),
            ],
            out_specs=pl.BlockSpec((MOE_BM, d), lambda i, be, nu: (i, 0)),
        ),
        out_shape=jax.ShapeDtypeStruct((n_slots, d), BF16),
        compiler_params=_cparams(1),
        name="moe_experts",
    )(block_e, n_used, xb, w1, w3, w2)


def _combine_kernel(seg_off, src_row, seg_len, x_ref, slot_ref, wt_ref, gate_ref, gainf_ref, yb_ref, o_ref,
                    buf, sem, *, final_norm):
    i = pl.program_id(0)
    n = pl.num_programs(0)

    def fetch(tile_idx, half, action):
        for e in range(N_EXPERTS):
            j = tile_idx * N_EXPERTS + e

            def chunk(offset, size, j=j):
                s0 = pl.multiple_of(src_row[j] + offset, SEG_ALIGN)
                d0 = pl.multiple_of(seg_off[j] + offset, SEG_ALIGN)
                action(pltpu.make_async_copy(yb_ref.at[pl.ds(s0, size), :],
                                             buf.at[half, pl.ds(d0, size), :], sem.at[half]))
            _for_each_chunk(seg_len[j], chunk)

    @pl.when(i == 0)
    def _():
        buf[...] = jnp.zeros_like(buf)
        fetch(i, 0, lambda c: c.start())

    half = i % 2
    pl.when(i + 1 < n)(lambda: fetch(i + 1, 1 - half, lambda c: c.start()))
    fetch(i, half, lambda c: c.wait())
    slot = slot_ref[...]
    wt = wt_ref[...]
    li = lax.broadcasted_iota(jnp.int32, (slot.shape[0], buf.shape[1]), 1)
    mix = jnp.where(li == slot[:, 0:1], wt[:, 0:1], 0.0) + jnp.where(li == slot[:, 1:2], wt[:, 1:2], 0.0)
    y = _dot(mix.astype(BF16), buf[half])
    x2 = x_ref[...] + gate_ref[...] * y
    if final_norm:
        x2 = x2 * lax.rsqrt(jnp.mean(x2 * x2, axis=-1, keepdims=True) + EPS) * gainf_ref[...]
    o_ref[...] = x2


def _combine(x1, yb, seg_off, src_row, seg_len, slot_cols, wt_cols, modl, gain_f, *, tile, mod_row,
             first_tile, final_norm):
    t, d = x1.shape
    flat = lambda a: a.reshape(-1).astype(jnp.int32)
    return pl.pallas_call(
        functools.partial(_combine_kernel, final_norm=final_norm),
        grid_spec=pltpu.PrefetchScalarGridSpec(
            num_scalar_prefetch=3,
            grid=(t // tile,),
            in_specs=[
                pl.BlockSpec((tile, d), lambda i, *_: (i, 0)),
                pl.BlockSpec((tile, 2), lambda i, *_: (i, 0)),
                pl.BlockSpec((tile, 2), lambda i, *_: (i, 0)),
                pl.BlockSpec((None, None, 1, d), lambda i, *_: (mod_row(i + first_tile), 5, 0, 0)),
                pl.BlockSpec((1, d), lambda i, *_: (0, 0)),
                pl.BlockSpec(memory_space=pl.ANY),
            ],
            out_specs=pl.BlockSpec((tile, d), lambda i, *_: (i, 0)),
            scratch_shapes=[pltpu.VMEM((2, _xs_rows(tile), d), yb.dtype), pltpu.SemaphoreType.DMA((2,))],
        ),
        out_shape=jax.ShapeDtypeStruct((t, d), F32),
        compiler_params=_cparams(1),
        name="moe_combine",
    )(flat(seg_off), flat(src_row), flat(seg_len), x1, slot_cols, wt_cols, modl, gain_f, yb)


def _slot_plan(seg_len, *, tile, n_slots):
    n_tiles = seg_len.shape[0]
    xs_rows = _xs_rows(tile)
    nb = n_slots // MOE_BM
    seg_off = jnp.cumsum(seg_len, axis=1) - seg_len
    tile_rows = jnp.sum(seg_len, axis=1)
    rows_e = jnp.sum(seg_len, axis=0)
    padded = (rows_e + MOE_BM - 1) // MOE_BM * MOE_BM
    pend = jnp.cumsum(padded)
    pstart = pend - padded
    xb_row = pstart[None, :] + jnp.cumsum(seg_len, axis=0) - seg_len
    n_used = pend[-1] // MOE_BM
    blk = jnp.minimum(jnp.arange(nb, dtype=jnp.int32), n_used - 1)
    block_e = jnp.sum((pend[None, :] <= (blk * MOE_BM)[:, None]).astype(jnp.int32), axis=1)
    block_e = jnp.minimum(block_e, N_EXPERTS - 1).astype(jnp.int32)
    n_spare = nb - (2 * n_tiles * tile) // MOE_BM
    spare = n_used + jnp.arange(n_spare, dtype=jnp.int32)
    xb_zero_row = jnp.concatenate([pstart + rows_e, jnp.minimum(spare, nb - 1) * MOE_BM])
    xb_zero_len = jnp.concatenate([padded - rows_e, jnp.where(spare < nb, MOE_BM, 0)])
    return seg_off, xb_row, xb_zero_row, xb_zero_len, block_e, n_used.astype(jnp.int32).reshape(1)


def kernel(x, c, ctx, c_ctx, w_mod, b_mod, norm1, norm2, w_in, w_gk2, b_gk, w_pool, pool_scale, gla_gain,
           w_out, w_router, router_bias, w1, w3, w2, norm_f):
    batch, seq, d = x.shape
    ctx_len = ctx.shape[1]
    depth = w_mod.shape[0]
    n_ctx = batch * ctx_len
    t_all = n_ctx + batch * seq
    def row_tile(limit):
        return max(t for t in (1024, 512, 256) if t <= limit and n_ctx % t == 0 and seq % t == 0)

    tile_a = row_tile(1024)
    tile_b = row_tile(512)

    def mod_row_for(tile):
        nct, per_b = n_ctx // tile, seq // tile
        return lambda i: jnp.where(i < nct, 0, 1 + (i - nct) // per_b)

    xf = jnp.concatenate([ctx.reshape(n_ctx, d), x.reshape(batch * seq, d)], axis=0)

    cvecs = jnp.zeros((16, d), F32).at[0].set(c_ctx).at[1:1 + batch].set(c)
    mods = _modulation(cvecs, w_mod, b_mod).reshape(depth, 16, N_MOD, 1, d)

    n_main = w_in.shape[2] - 2 * GATE_RANK
    w_main = w_in[:, :, :n_main].astype(BF16)
    w_r = jnp.pad(w_in[:, :, n_main:], ((0, 0), (0, 0), (0, 128 - 2 * GATE_RANK))).astype(BF16)
    w2g = jnp.zeros((depth, 128, 2 * GLA_QK), F32)
    w2g = w2g.at[:, 0:GATE_RANK, 0:GLA_QK].set(w_gk2[:, 0])
    w2g = w2g.at[:, GATE_RANK:2 * GATE_RANK, GLA_QK:].set(w_gk2[:, 1])
    bgk = b_gk.reshape(depth, 1, 2 * GLA_QK)
    w_out_b = w_out.astype(BF16)
    w_pool_b = w_pool.astype(BF16)
    w1b, w3b, w2b = w1.astype(BF16), w3.astype(BF16), w2.astype(BF16)
    w_router_t = w_router.T
    rbias = router_bias.reshape(N_EXPERTS, 1)
    g1 = lambda a: a.reshape(1, -1)

    out = None
    for l in range(depth):
        last = l == depth - 1
        pbf, lg = _inproj(xf, mods[l], g1(norm1[l]), w_main[l], w_r[l], w2g[l], bgk[l],
                          tile=tile_a, mod_row=mod_row_for(tile_a))
        att_c, att_x = _gla(pbf, lg, gla_gain[l], batch=batch, ctx_len=ctx_len, seq=seq)
        pool_c, pool_x = _pool(pbf, w_pool_b[l], pool_scale[l], batch=batch, ctx_len=ctx_len, seq=seq)
        first_tile = n_ctx // tile_b if last else 0
        n_tiles = t_all // tile_b - first_tile
        x1, xs, slots, wts, seg_len = _outproj(
            xf, pool_c, pool_x, att_c, att_x, mods[l], g1(norm2[l]), w_out_b[l], w_router_t, rbias,
            tile=tile_b, mod_row=mod_row_for(tile_b), first_tile=first_tile, n_tiles=n_tiles)
        seg_len = seg_len[:, :, 0]
        max_rows = n_tiles * (2 * tile_b + N_EXPERTS * (SEG_ALIGN - 1)) + N_EXPERTS * (MOE_BM - 1)
        n_slots = -(-max_rows // MOE_BM) * MOE_BM
        seg_off, xb_row, xb_zero_row, xb_zero_len, block_e, n_used = _slot_plan(
            seg_len, tile=tile_b, n_slots=n_slots)
        xb = _dispatch(xs, seg_off, xb_row, seg_len, xb_zero_row, xb_zero_len, n_slots, tile=tile_b)
        yb = _experts(xb, block_e, n_used, w1b[l], w3b[l], w2b[l])
        res = _combine(x1, yb, seg_off, xb_row, seg_len, slots.T, wts.T, mods[l], g1(norm_f), tile=tile_b,
                       mod_row=mod_row_for(tile_b), first_tile=first_tile, final_norm=last)
        if last:
            out = res
        else:
            xf = res
    return out.reshape(batch, seq, d)
```

```python
import functools

import numpy as np
import jax
import jax.numpy as jnp
from jax import lax
from jax.experimental import pallas as pl
from jax.experimental.pallas import tpu as pltpu

F32 = jnp.float32
BF16 = jnp.bfloat16

GRID_W = 64
POOL_GROUPS = 4
POOL_WINDOWS = (2, 4, 8, 16)
GLA_HEADS = 4
GLA_DK = 64
GLA_DV = 128
GLA_QK = GLA_HEADS * GLA_DK
GATE_RANK = 16
GATE_NORM = 16.0
N_EXPERTS = 16
N_GROUPS = 4
EXPERTS_PER_GROUP = 4
EPS = 1e-6
N_MOD = 6

VMEM_LIMIT_BYTES = 56 * 1024 * 1024
SPAN = 256
SUB = 32
N_SUB = SPAN // SUB
LOG2_E = 1.4426950408889634
EXP2_CLAMP = 115.0
DIAG_W = 128
GLA_KW = DIAG_W + N_SUB * GLA_DK
MOE_BM = 256
SEG_ALIGN = 16


def _cparams(n_axes):
    return pltpu.CompilerParams(
        dimension_semantics=("arbitrary",) * n_axes, vmem_limit_bytes=VMEM_LIMIT_BYTES)


def _split(a):
    hi = a.astype(BF16)
    lo = (a - hi.astype(F32)).astype(BF16)
    return hi, lo


def _dot(a, b):
    return jnp.dot(a, b, preferred_element_type=F32)


def _dot_nt(a, b):
    return lax.dot_general(a, b, (((1,), (1,)), ((), ())), preferred_element_type=F32)


def _dot_tn(a, b):
    return lax.dot_general(a, b, (((0,), (0,)), ((), ())), preferred_element_type=F32)


def _dot3(a, b):
    ah, al = _split(a)
    bh, bl = _split(b)
    return _dot(ah, bh) + (_dot(ah, bl) + _dot(al, bh))


def _dot3_nt(a, b):
    ah, al = _split(a)
    bh, bl = _split(b)
    return _dot_nt(ah, bh) + (_dot_nt(ah, bl) + _dot_nt(al, bh))


def _dot_exact_lhs(m_bf16, b, terms=3):
    out = None
    rest = b
    for _ in range(terms):
        part = rest.astype(BF16)
        rest = rest - part.astype(F32)
        prod = _dot(m_bf16, part)
        out = prod if out is None else out + prod
    return out


def _silu(x):
    return x * jax.nn.sigmoid(x)


def _mod_kernel(c_ref, w_ref, b_ref, o_ref):
    o_ref[0] = _dot3(_silu(c_ref[...]), w_ref[0]) + b_ref[0]


def _modulation(cvecs, w_mod, b_mod):
    depth, d, _ = w_mod.shape
    rows = cvecs.shape[0]
    return pl.pallas_call(
        _mod_kernel,
        grid=(depth, N_MOD),
        in_specs=[
            pl.BlockSpec((rows, d), lambda l, j: (0, 0)),
            pl.BlockSpec((1, d, d), lambda l, j: (l, 0, j)),
            pl.BlockSpec((1, 1, d), lambda l, j: (l, 0, j)),
        ],
        out_specs=pl.BlockSpec((1, rows, d), lambda l, j: (l, 0, j)),
        out_shape=jax.ShapeDtypeStruct((depth, rows, N_MOD * d), F32),
        compiler_params=_cparams(2),
        name="adaln_vectors",
    )(cvecs, w_mod, b_mod.reshape(depth, 1, N_MOD * d))


def _norm_mod(x, gain, scale, shift):
    y = x * lax.rsqrt(jnp.mean(x * x, axis=-1, keepdims=True) + EPS)
    return (y * gain) * (1.0 + scale) + shift


def _inproj_kernel(x_ref, scale_ref, shift_ref, gain_ref, win_ref, w2_ref, bgk_ref,
                   p_ref, lg_ref, w_ref, wr_ref, *, n_chunk):
    n_main = p_ref.shape[1]
    width = n_main // n_chunk

    @pl.when(pl.program_id(0) == 0)
    def _():
        for n in range(n_chunk):
            cols = slice(n * width, (n + 1) * width)
            w_ref[:, cols] = win_ref[:, cols].astype(BF16)
        wr_ref[...] = jnp.zeros_like(wr_ref)
        wr_ref[:, 0:2 * GATE_RANK] = win_ref[:, n_main:].astype(BF16)

    h = _norm_mod(x_ref[...], gain_ref[...], scale_ref[...], shift_ref[...])
    hb = h.astype(BF16)
    for n in range(n_chunk):
        cols = slice(n * width, (n + 1) * width)
        p_ref[:, cols] = _dot(hb, w_ref[:, cols]).astype(BF16)
    r = _dot(hb, wr_ref[...])
    z = _dot3(r, w2_ref[...]) + bgk_ref[...]
    log_sig = jnp.minimum(z, 0.0) - jnp.log(1.0 + jnp.exp(-jnp.abs(z)))
    lg_ref[...] = log_sig * (LOG2_E / GATE_NORM)


def _inproj(xf, modl, gain, w_in_all, layer, w2, bgk, *, tile, mod_row):
    t, d = xf.shape
    n_main = w_in_all.shape[2] - 2 * GATE_RANK
    n_gate = w2.shape[1]
    mod_spec = lambda m: pl.BlockSpec((None, None, 1, d), lambda i: (mod_row(i), m, 0, 0))
    full = lambda a: pl.BlockSpec(a.shape, lambda i: (0,) * a.ndim)
    return pl.pallas_call(
        functools.partial(_inproj_kernel, n_chunk=4),
        grid=(t // tile,),
        in_specs=[
            pl.BlockSpec((tile, d), lambda i: (i, 0)),
            mod_spec(1), mod_spec(0), full(gain),
            pl.BlockSpec((None,) + w_in_all.shape[1:], lambda i: (layer, 0, 0)), full(w2), full(bgk),
        ],
        out_specs=[
            pl.BlockSpec((tile, n_main), lambda i: (i, 0)),
            pl.BlockSpec((tile, n_gate), lambda i: (i, 0)),
        ],
        out_shape=[
            jax.ShapeDtypeStruct((t, n_main), BF16),
            jax.ShapeDtypeStruct((t, n_gate), F32),
        ],
        scratch_shapes=[pltpu.VMEM((d, n_main), BF16), pltpu.VMEM((d, 128), BF16)],
        compiler_params=_cparams(1),
        name="norm_inproj",
    )(xf, modl, modl, gain, w_in_all, w2, bgk)


def _store_heads(dst_ref, rows, col0, val):
    half = (col0 // GLA_DK) & 1
    plain = val.astype(BF16)
    turned = pltpu.roll(val, GLA_DK, axis=1).astype(BF16)
    for h in range(GLA_HEADS):
        if (h & 1) == half:
            src = plain[:, h * GLA_DK:(h + 1) * GLA_DK]
        else:
            hh = (h + 1) % GLA_HEADS
            src = turned[:, hh * GLA_DK:(hh + 1) * GLA_DK]
        dst_ref[h, rows, col0:col0 + GLA_DK] = src


def _gla_span(q_ref, k_ref, v_ref, lg_ref, o_ref, row0, st_ref, qh_ref, kh_ref, b_ref, qf_ref, kf_ref,
              tri_ref, dmask_ref, smask_ref, *, rev):
    rows = pl.ds(row0, SPAN)
    dcol = slice(rev * GLA_QK, (rev + 1) * GLA_QK)
    b_ref[...] = _dot_exact_lhs(tri_ref[rev], lg_ref[rows, dcol])
    qf_ref[...] = q_ref[rows, :].astype(F32) * (GLA_DK ** -0.5)
    kf_ref[...] = k_ref[rows, :].astype(F32)
    e_tot = b_ref[0:1, :] if rev else b_ref[SPAN - 1:SPAN, :]

    for s in range(N_SUB - 1):
        j = s + 1 if rev else s
        ref_row = j * SUB if rev else j * SUB + SUB - 1
        e_j = b_ref[ref_row:ref_row + 1, :]
        blk = slice(j * SUB, (j + 1) * SUB)
        qrows = slice(0, j * SUB) if rev else slice((j + 1) * SUB, SPAN)
        col0 = DIAG_W + s * GLA_DK
        _store_heads(qh_ref, qrows, col0, qf_ref[qrows, :] * jnp.exp2(b_ref[qrows, :] - e_j))
        _store_heads(kh_ref, blk, col0, kf_ref[blk, :] * jnp.exp2(e_j - b_ref[blk, :]))
    for i in range(N_SUB):
        blk = slice(i * SUB, (i + 1) * SUB)
        mid = i * SUB + SUB // 2
        e_m = b_ref[mid:mid + 1, :]
        _store_heads(qh_ref, blk, 0, qf_ref[blk, :] * jnp.exp2(jnp.minimum(b_ref[blk, :] - e_m, EXP2_CLAMP)))
        _store_heads(kh_ref, blk, 0, kf_ref[blk, :] * jnp.exp2(jnp.minimum(e_m - b_ref[blk, :], EXP2_CLAMP)))

    st = st_ref[...]
    qd = (qf_ref[...] * jnp.exp2(b_ref[...])).astype(BF16)
    o_state = _dot_nt(qd, st.astype(BF16))
    in_block = dmask_ref[rev] != 0.0
    for h in range(GLA_HEADS):
        vc = slice(h * GLA_DV, (h + 1) * GLA_DV)
        a_diag = _dot_nt(qh_ref[h, :, 0:DIAG_W], kh_ref[h, :, 0:DIAG_W])
        a_cross = _dot_nt(qh_ref[h, :, DIAG_W:], kh_ref[h, :, DIAG_W:])
        a = (a_cross + jnp.where(in_block, a_diag, 0.0)).astype(BF16)
        o_ref[rows, vc] += _dot(a, v_ref[rows, vc]) + o_state[:, vc]
    kd =(kf_ref[...] * jnp.exp2(e_tot - b_ref[...])).astype(BF16)
    upd = _dot_tn(v_ref[rows, :], kd)
    st_ref[...] = st * jnp.exp2(e_tot) + upd * smask_ref[...]


def _gla_finish(o_ref, g_ref, gain, out_ref, n_rows):
    def body(c, carry):
        rows = pl.ds(pl.multiple_of(c * SPAN, SPAN), SPAN)
        for h in range(GLA_HEADS):
            vc = slice(h * GLA_DV, (h + 1) * GLA_DV)
            o = o_ref[rows, vc]
            y = o * lax.rsqrt(jnp.mean(o * o, axis=-1, keepdims=True) + EPS) * gain
            g = g_ref[rows, vc].astype(F32)
            out_ref[rows, vc] = (y * _silu(g)).astype(BF16)
        return carry
    lax.fori_loop(0, n_rows // SPAN, body, 0)


def _gla_kernel(qc, kc, vc, gc, lgc, qx, kx, vx, gx, lgx, gain_ref, tri, dmask, smask, outc, outx,
                oc, ox, *per_direction):
    n_c = qc.shape[0] // SPAN
    n_x = qx.shape[0] // SPAN
    names = ("st_ref", "qh_ref", "kh_ref", "b_ref", "qf_ref", "kf_ref")
    scratch = [dict(zip(names, per_direction[r * len(names):(r + 1) * len(names)])) for r in (0, 1)]
    for ref in (oc, ox) + tuple(s[n] for s in scratch for n in ("st_ref", "qh_ref", "kh_ref")):
        ref[...] = jnp.zeros_like(ref)
    spans = [functools.partial(_gla_span, tri_ref=tri, dmask_ref=dmask, smask_ref=smask, rev=rev,
                               **scratch[rev]) for rev in (0, 1)]
    for s in range(n_c):
        spans[0](qc, kc, vc, lgc, oc, s * SPAN)
        spans[1](qc, kc, vc, lgc, oc, (n_c - 1 - s) * SPAN)

    def body(s, carry):
        spans[0](qx, kx, vx, lgx, ox, pl.multiple_of(s * SPAN, SPAN))
        spans[1](qx, kx, vx, lgx, ox, pl.multiple_of((n_x - 1 - s) * SPAN, SPAN))
        return carry
    lax.fori_loop(0, n_x, body, 0)
    gain = gain_ref[...]
    _gla_finish(oc, gc, gain, outc, qc.shape[0])
    _gla_finish(ox, gx, gain, outx, qx.shape[0])


def _gla(pbf, lg, gla_gain, *, batch, ctx_len, seq):
    nxb = (batch * ctx_len) // seq
    assert nxb * seq == batch * ctx_len
    hv = GLA_HEADS * GLA_DV
    cspec = lambda w, j: pl.BlockSpec((ctx_len, w), lambda b: (b, j))
    xspec = lambda w, j: pl.BlockSpec((seq, w), lambda b: (nxb + b, j))
    full = lambda a: pl.BlockSpec(a.shape, lambda b: (0,) * a.ndim)
    pos = np.arange(SPAN)
    lower = pos[None, :] <= pos[:, None]
    same = (pos[None, :] // SUB) == (pos[:, None] // SUB)
    tri = jnp.asarray(np.stack([lower, lower.T]).astype(np.float32), BF16)
    dmask = jnp.asarray(np.stack([lower & same, lower.T & same]).astype(np.float32))
    smask = jnp.asarray(((np.arange(hv)[:, None] // GLA_DV) == (np.arange(GLA_QK)[None, :] // GLA_DK))
                        .astype(np.float32))
    return pl.pallas_call(
        _gla_kernel,
        grid=(batch,),
        in_specs=[
            cspec(GLA_QK, 2), cspec(GLA_QK, 3), cspec(hv, 2), cspec(hv, 3), cspec(2 * GLA_QK, 0),
            xspec(GLA_QK, 2), xspec(GLA_QK, 3), xspec(hv, 2), xspec(hv, 3), xspec(2 * GLA_QK, 0),
            pl.BlockSpec((1, GLA_DV), lambda b: (0, 0)), full(tri), full(dmask), full(smask),
        ],
        out_specs=[
            pl.BlockSpec((ctx_len, hv), lambda b: (b, 0)),
            pl.BlockSpec((seq, hv), lambda b: (b, 0)),
        ],
        out_shape=[
            jax.ShapeDtypeStruct((batch * ctx_len, hv), BF16),
            jax.ShapeDtypeStruct((batch * seq, hv), BF16),
        ],
        scratch_shapes=[
            pltpu.VMEM((ctx_len, hv), F32),
            pltpu.VMEM((seq, hv), F32),
        ] + 2 * [
            pltpu.VMEM((hv, GLA_QK), F32),
            pltpu.VMEM((GLA_HEADS, SPAN, GLA_KW), BF16),
            pltpu.VMEM((GLA_HEADS, SPAN, GLA_KW), BF16),
            pltpu.VMEM((SPAN, GLA_QK), F32),
            pltpu.VMEM((SPAN, GLA_QK), F32),
            pltpu.VMEM((SPAN, GLA_QK), F32),
        ],
        compiler_params=_cparams(1),
        name="gla_bidirectional",
    )(pbf, pbf, pbf, pbf, lg, pbf, pbf, pbf, pbf, lg, gla_gain.reshape(1, GLA_DV), tri, dmask, smask)


def _window_offsets(w):
    return range(-(w // 2), w - w // 2)


def _band(w, period):
    ri = lax.broadcasted_iota(jnp.int32, (SPAN, SPAN), 0)
    ci = lax.broadcasted_iota(jnp.int32, (SPAN, SPAN), 1)
    d = ci - ri
    shift = period.bit_length() - 1
    assert period == 1 << shift
    same_row = (ri >> shift) == (ci >> shift)
    inside = jnp.where(d >= -(w // 2), jnp.where(d <= w - w // 2 - 1, 1.0, 0.0), 0.0)
    return jnp.where(same_row, inside, 0.0).astype(BF16)


def _pool_kernel(uc_ref, ux_ref, invc_ref, invx_ref, wp_ref, ps_ref, outc_ref, outx_ref, pad_ref,
                 *, pad_rows):
    n_x = ux_ref.shape[0]
    gw = wp_ref.shape[1]
    pad_ref[0:pad_rows, :] = jnp.zeros((pad_rows, gw), F32)
    pad_ref[pad_rows + n_x:, :] = jnp.zeros((pad_rows, gw), F32)
    for g, w in enumerate(POOL_WINDOWS):
        gc = slice(g * gw, (g + 1) * gw)
        wp = wp_ref[g]
        ps = ps_ref[:, gc]
        band_c = _band(w, SPAN)
        for t in range(uc_ref.shape[0] // SPAN):
            rows = slice(t * SPAN, (t + 1) * SPAN)
            u = uc_ref[rows, gc].astype(F32)
            pooled = _dot_exact_lhs(band_c, u, terms=2) * invc_ref[g, rows, :] - u
            outc_ref[rows, gc] = (_dot(pooled.astype(BF16), wp) * ps).astype(BF16)
        pad_ref[pad_rows:pad_rows + n_x, :] = ux_ref[:, gc].astype(F32)
        band_x = _band(w, GRID_W)

        def body(t, carry):
            r0 = pl.multiple_of(t * SPAN, SPAN)
            acc = jnp.zeros((SPAN, gw), F32)
            for dr in _window_offsets(w):
                acc = acc + pad_ref[pl.ds(pad_rows + r0 + dr * GRID_W, SPAN), :]
            rows = pl.ds(r0, SPAN)
            u = pad_ref[pl.ds(pad_rows + r0, SPAN), :]
            pooled = _dot_exact_lhs(band_x, acc, terms=2) * invx_ref[g, rows, :] - u
            outx_ref[rows, gc] = (_dot(pooled.astype(BF16), wp) * ps).astype(BF16)
            return carry
        lax.fori_loop(0, n_x // SPAN, body, 0, unroll=4)


def _inv_counts(length, rows_of):
    out = []
    for w in POOL_WINDOWS:
        cnt = np.ones((length,), np.float64)
        for axis_len, coord in rows_of(length):
            lo = np.clip(coord - w // 2, 0, axis_len)
            hi = np.clip(coord + w - w // 2, 0, axis_len)
            cnt = cnt * (hi - lo)
        out.append(1.0 / cnt)
    return np.broadcast_to(np.stack(out)[:, :, None], (len(POOL_WINDOWS), length, 128)).astype(np.float32)


def _pool(pbf, w_pool, pool_scale, *, batch, ctx_len, seq):
    nxb = (batch * ctx_len) // seq
    pw = w_pool.shape[0] * w_pool.shape[1]
    pad_rows = (max(POOL_WINDOWS) // 2) * GRID_W
    inv_c = jnp.asarray(_inv_counts(ctx_len, lambda n: [(n, np.arange(n))]))
    inv_x = jnp.asarray(_inv_counts(
        seq, lambda n: [(n // GRID_W, np.arange(n) // GRID_W), (GRID_W, np.arange(n) % GRID_W)]))
    full = lambda a: pl.BlockSpec(a.shape, lambda b: (0,) * a.ndim)
    return pl.pallas_call(
        functools.partial(_pool_kernel, pad_rows=pad_rows),
        grid=(batch,),
        in_specs=[
            pl.BlockSpec((ctx_len, pw), lambda b: (b, 0)),
            pl.BlockSpec((seq, pw), lambda b: (nxb + b, 0)),
            full(inv_c), full(inv_x), full(w_pool),
            pl.BlockSpec((1, pw), lambda b: (0, 0)),
        ],
        out_specs=[
            pl.BlockSpec((ctx_len, pw), lambda b: (b, 0)),
            pl.BlockSpec((seq, pw), lambda b: (b, 0)),
        ],
        out_shape=[
            jax.ShapeDtypeStruct((batch * ctx_len, pw), BF16),
            jax.ShapeDtypeStruct((batch * seq, pw), BF16),
        ],
        scratch_shapes=[pltpu.VMEM((seq + 2 * pad_rows, w_pool.shape[1]), F32)],
        compiler_params=_cparams(1),
        name="pool_mixer",
    )(pbf, pbf, inv_c, inv_x, w_pool, pool_scale.reshape(1, pw))


def _route(sb, s):
    def rank_in_group(vals):
        ranks = []
        for i, vi in enumerate(vals):
            r = jnp.zeros(vi.shape, jnp.int32)
            for j, vj in enumerate(vals):
                if j == i:
                    continue
                ahead = (vj >= vi) if j < i else (vj > vi)
                r = r + jnp.where(ahead, 1, 0)
            ranks.append(r)
        return ranks

    best = None
    for g in range(N_GROUPS):
        vals = sb[g * EXPERTS_PER_GROUP:(g + 1) * EXPERTS_PER_GROUP]
        svals = s[g * EXPERTS_PER_GROUP:(g + 1) * EXPERTS_PER_GROUP]
        score = None
        for i in range(EXPERTS_PER_GROUP):
            for j in range(i + 1, EXPERTS_PER_GROUP):
                pair = vals[i] + vals[j]
                score = pair if score is None else jnp.maximum(score, pair)
        ranks = rank_in_group(vals)
        e1 = jnp.zeros(score.shape, jnp.int32)
        e2 = jnp.zeros(score.shape, jnp.int32)
        w1 = jnp.zeros(score.shape, F32)
        w2 = jnp.zeros(score.shape, F32)
        for i in range(EXPERTS_PER_GROUP):
            e1 = jnp.where(ranks[i] == 0, g * EXPERTS_PER_GROUP + i, e1)
            e2 = jnp.where(ranks[i] == 1, g * EXPERTS_PER_GROUP + i, e2)
            w1 = jnp.where(ranks[i] == 0, svals[i], w1)
            w2 = jnp.where(ranks[i] == 1, svals[i], w2)
        cand = (score, e1, e2, w1, w2)
        if best is None:
            best = cand
        else:
            take = cand[0] > best[0]
            best = tuple(jnp.where(take, c, b) for c, b in zip(cand, best))
    _, e1, e2, w1, w2 = best
    tot = w1 + w2
    return e1, e2, w1 / tot, w2 / tot


def _outproj_kernel(x_ref, pc_ref, px_ref, ac_ref, ax_ref, gate_ref, scale_ref, shift_ref, gain_ref,
                    wo_ref, wrt_ref, rb_ref, before_ref, below_ref, x1_ref, xs_ref, slot_ref, wt_ref, len_ref,
                    *, n_ctx_tiles, half):
    is_ctx = pl.program_id(0) < n_ctx_tiles
    pool = jnp.where(is_ctx, pc_ref[...], px_ref[...])
    att = jnp.where(is_ctx, ac_ref[...], ax_ref[...])
    y = _dot(pool, wo_ref[0:half, :]) + _dot(att, wo_ref[half:, :])
    x1 = x_ref[...] + gate_ref[...] * y
    x1_ref[...] = x1
    h2 = _norm_mod(x1, gain_ref[...], scale_ref[...], shift_ref[...])
    logits = _dot3_nt(wrt_ref[...], h2)
    s = jax.nn.sigmoid(logits)
    sb = s + rb_ref[...]
    rows = lambda a: [a[i:i + 1, :] for i in range(N_EXPERTS)]
    e1, e2, w1, w2 = _route(rows(sb), rows(s))
    wt_ref[0:1, :] = w1
    wt_ref[1:2, :] = w2
    ie = lax.broadcasted_iota(jnp.int32, logits.shape, 0)
    oh1 = jnp.where(ie == e1, 1.0, 0.0)
    oh2 = jnp.where(ie == e2, 1.0, 0.0)
    c1 = jnp.sum(oh1, axis=1, keepdims=True)
    c2 = jnp.sum(oh2, axis=1, keepdims=True)
    seg_len = jnp.ceil((c1 + c2) * (1.0 / SEG_ALIGN)) * SEG_ALIGN
    seg_len_l = jnp.broadcast_to(seg_len, len_ref.shape)
    seg_off = _dot(below_ref[...], seg_len_l.astype(BF16))[:, 0:1]
    before1 = _dot(oh1.astype(BF16), before_ref[...]) + seg_off
    before2 = _dot(oh2.astype(BF16), before_ref[...]) + (seg_off + c1)
    slot1 = jnp.sum(oh1 * before1, axis=0, keepdims=True).astype(jnp.int32)
    slot2 = jnp.sum(oh2 * before2, axis=0, keepdims=True).astype(jnp.int32)
    slot_ref[0:1, :] = slot1
    slot_ref[1:2, :] = slot2
    len_ref[...] = seg_len_l.astype(jnp.int32)
    si = lax.broadcasted_iota(jnp.int32, (xs_ref.shape[0], slot1.shape[1]), 0)
    perm = jnp.where(si == slot1, 1.0, 0.0) + jnp.where(si == slot2, 1.0, 0.0)
    xs_ref[...] = _dot(perm.astype(BF16), h2.astype(BF16)).astype(BF16)


def _outproj(xf, pool_c, pool_x, att_c, att_x, modl, gain2, w_out, w_router_t, router_bias,
             *, tile, mod_row, first_tile, n_tiles):
    t, d = xf.shape
    half = pool_c.shape[1]
    n_ctx_tiles = pool_c.shape[0] // tile
    cidx = lambda i: (jnp.minimum(i + first_tile, n_ctx_tiles - 1), 0)
    xidx = lambda i: (jnp.maximum(i + first_tile - n_ctx_tiles, 0), 0)
    mod_spec = lambda m: pl.BlockSpec((None, None, 1, d), lambda i: (mod_row(i + first_tile), m, 0, 0))
    full = lambda a: pl.BlockSpec(a.shape, lambda i: (0,) * a.ndim)
    tok = lambda w: pl.BlockSpec((tile, w), lambda i: (i, 0))
    lane = pl.BlockSpec((2, tile), lambda i: (0, i))
    t_out = n_tiles * tile
    before = jnp.asarray(np.triu(np.ones((tile, tile), np.float32), 1), BF16)
    below = jnp.asarray(np.tril(np.ones((N_EXPERTS, N_EXPERTS), np.float32), -1), BF16)
    xs_rows = _xs_rows(tile)
    return pl.pallas_call(
        functools.partial(_outproj_kernel, n_ctx_tiles=n_ctx_tiles - first_tile, half=half),
        grid=(n_tiles,),
        in_specs=[
            pl.BlockSpec((tile, d), lambda i: (i + first_tile, 0)),
            pl.BlockSpec((tile, half), cidx), pl.BlockSpec((tile, half), xidx),
            pl.BlockSpec((tile, half), cidx), pl.BlockSpec((tile, half), xidx),
            mod_spec(2), mod_spec(4), mod_spec(3), full(gain2), full(w_out), full(w_router_t),
            full(router_bias), full(before), full(below),
        ],
        out_specs=[tok(d), pl.BlockSpec((xs_rows, d), lambda i: (i, 0)), lane, lane,
                   pl.BlockSpec((None, N_EXPERTS, 128), lambda i: (i, 0, 0))],
        out_shape=[
            jax.ShapeDtypeStruct((t_out, d), F32),
            jax.ShapeDtypeStruct((n_tiles * xs_rows, d), BF16),
            jax.ShapeDtypeStruct((2, t_out), jnp.int32),
            jax.ShapeDtypeStruct((2, t_out), F32),
            jax.ShapeDtypeStruct((n_tiles, N_EXPERTS, 128), jnp.int32),
        ],
        compiler_params=_cparams(1),
        name="outproj_router",
    )(xf, pool_c, pool_x, att_c, att_x, modl, modl, modl, gain2, w_out, w_router_t, router_bias,
      before, below)


def _xs_rows(tile):
    return 2 * tile + N_EXPERTS * SEG_ALIGN


def _for_each_unit(n, fn):
    def body(u, carry):
        fn(u * SEG_ALIGN)
        return carry
    lax.fori_loop(0, n // SEG_ALIGN, body, 0)


def _dispatch_kernel(seg_off, dst_row, seg_len, z_row, z_len, xs_ref, xb_ref, zero_ref, sem, zsem):
    i = pl.program_id(0)

    def zero_fill(action):
        def body(j, carry):
            def unit(offset):
                d0 = pl.multiple_of(z_row[j] + offset, SEG_ALIGN)
                action(pltpu.make_async_copy(zero_ref, xb_ref.at[pl.ds(d0, SEG_ALIGN), :], zsem))
            _for_each_unit(z_len[j], unit)
            return carry
        lax.fori_loop(0, z_row.shape[0], body, 0)

    @pl.when(i == 0)
    def _():
        zero_ref[...] = jnp.zeros_like(zero_ref)
        zero_fill(lambda c: c.start())

    for action in (lambda c: c.start(), lambda c: c.wait()):
        for e in range(N_EXPERTS):
            j = i * N_EXPERTS + e

            def unit(offset, j=j):
                s0 = pl.multiple_of(seg_off[j] + offset, SEG_ALIGN)
                d0 = pl.multiple_of(dst_row[j] + offset, SEG_ALIGN)
                action(pltpu.make_async_copy(xs_ref.at[pl.ds(s0, SEG_ALIGN), :],
                                             xb_ref.at[pl.ds(d0, SEG_ALIGN), :], sem))
            _for_each_unit(seg_len[j], unit)

    pl.when(i == 0)(lambda: zero_fill(lambda c: c.wait()))


def _dispatch(xs, seg_off, dst_row, seg_len, z_row, z_len, n_slots, *, tile):
    d = xs.shape[1]
    xs_rows = _xs_rows(tile)
    flat = lambda a: a.reshape(-1).astype(jnp.int32)
    return pl.pallas_call(
        _dispatch_kernel,
        grid_spec=pltpu.PrefetchScalarGridSpec(
            num_scalar_prefetch=5,
            grid=(xs.shape[0] // xs_rows,),
            in_specs=[pl.BlockSpec((xs_rows, d), lambda i, *_: (i, 0))],
            out_specs=pl.BlockSpec(memory_space=pl.ANY),
            scratch_shapes=[pltpu.VMEM((SEG_ALIGN, d), xs.dtype), pltpu.SemaphoreType.DMA(()),
                            pltpu.SemaphoreType.DMA(())],
        ),
        out_shape=jax.ShapeDtypeStruct((n_slots, d), xs.dtype),
        compiler_params=_cparams(1),
        name="moe_dispatch",
    )(flat(seg_off), flat(dst_row), flat(seg_len), flat(z_row), flat(z_len), xs)


def _expert_kernel(be_ref, nu_ref, xb_ref, w1_ref, w3_ref, w2_ref, yb_ref):
    del be_ref
    used = pl.program_id(0) < nu_ref[0]

    @pl.when(used)
    def _():
        x = xb_ref[...]
        a = _dot(x, w1_ref[...])
        b = _dot(x, w3_ref[...])
        yb_ref[...] = _dot((_silu(a) * b).astype(BF16), w2_ref[...]).astype(BF16)

    @pl.when(jnp.logical_not(used))
    def _():
        yb_ref[...] = jnp.zeros_like(yb_ref)


def _experts(xb, block_e, n_used, w1, w3, w2):
    n_slots, d = xb.shape
    de = w1.shape[2]
    nb = n_slots // MOE_BM
    blk = lambda i, be, nu: (jnp.minimum(i, nu[0] - 1), 0)
    return pl.pallas_call(
        _expert_kernel,
        grid_spec=pltpu.PrefetchScalarGridSpec(
            num_scalar_prefetch=2,
            grid=(nb,),
            in_specs=[
                pl.BlockSpec((MOE_BM, d), blk),
                pl.BlockSpec((None, d, de), lambda i, be, nu: (be[i], 0, 0)),
                pl.BlockSpec((None, d, de), lambda i, be, nu: (be[i], 0, 0)),
                pl.BlockSpec((None, de, d), lambda i, be, nu: (be[i], 0, 0)),
            ],
            out_specs=pl.BlockSpec((MOE_BM, d), lambda i, be, nu: (i, 0)),
        ),
        out_shape=jax.ShapeDtypeStruct((n_slots, d), BF16),
        compiler_params=_cparams(1),
        name="moe_experts",
    )(block_e, n_used, xb, w1, w3, w2)


def _combine_kernel(seg_off, src_row, seg_len, x_ref, slot_ref, wt_ref, gate_ref, gainf_ref, yb_ref, o_ref,
                    buf, sem, *, final_norm):
    i = pl.program_id(0)
    n = pl.num_programs(0)

    def fetch(tile_idx, half, action):
        for e in range(N_EXPERTS):
            j = tile_idx * N_EXPERTS + e

            def unit(offset, j=j):
                s0 = pl.multiple_of(src_row[j] + offset, SEG_ALIGN)
                d0 = pl.multiple_of(seg_off[j] + offset, SEG_ALIGN)
                action(pltpu.make_async_copy(yb_ref.at[pl.ds(s0, SEG_ALIGN), :],
                                             buf.at[half, pl.ds(d0, SEG_ALIGN), :], sem.at[half]))
            _for_each_unit(seg_len[j], unit)

    @pl.when(i == 0)
    def _():
        buf[...] = jnp.zeros_like(buf)
        fetch(i, 0, lambda c: c.start())

    half = i % 2
    pl.when(i + 1 < n)(lambda: fetch(i + 1, 1 - half, lambda c: c.start()))
    fetch(i, half, lambda c: c.wait())
    slot = slot_ref[...]
    wt = wt_ref[...]
    li = lax.broadcasted_iota(jnp.int32, (slot.shape[0], buf.shape[1]), 1)
    mix = jnp.where(li == slot[:, 0:1], wt[:, 0:1], 0.0) + jnp.where(li == slot[:, 1:2], wt[:, 1:2], 0.0)
    y = _dot(mix.astype(BF16), buf[half])
    x2 = x_ref[...] + gate_ref[...] * y
    if final_norm:
        x2 = x2 * lax.rsqrt(jnp.mean(x2 * x2, axis=-1, keepdims=True) + EPS) * gainf_ref[...]
    o_ref[...] = x2


def _combine(x1, yb, seg_off, src_row, seg_len, slot_cols, wt_cols, modl, gain_f, *, tile, mod_row,
             first_tile, final_norm):
    t, d = x1.shape
    flat = lambda a: a.reshape(-1).astype(jnp.int32)
    return pl.pallas_call(
        functools.partial(_combine_kernel, final_norm=final_norm),
        grid_spec=pltpu.PrefetchScalarGridSpec(
            num_scalar_prefetch=3,
            grid=(t // tile,),
            in_specs=[
                pl.BlockSpec((tile, d), lambda i, *_: (i, 0)),
                pl.BlockSpec((tile, 2), lambda i, *_: (i, 0)),
                pl.BlockSpec((tile, 2), lambda i, *_: (i, 0)),
                pl.BlockSpec((None, None, 1, d), lambda i, *_: (mod_row(i + first_tile), 5, 0, 0)),
                pl.BlockSpec((1, d), lambda i, *_: (0, 0)),
                pl.BlockSpec(memory_space=pl.ANY),
            ],
            out_specs=pl.BlockSpec((tile, d), lambda i, *_: (i, 0)),
            scratch_shapes=[pltpu.VMEM((2, _xs_rows(tile), d), yb.dtype), pltpu.SemaphoreType.DMA((2,))],
        ),
        out_shape=jax.ShapeDtypeStruct((t, d), F32),
        compiler_params=_cparams(1),
        name="moe_combine",
    )(flat(seg_off), flat(src_row), flat(seg_len), x1, slot_cols, wt_cols, modl, gain_f, yb)


def _slot_plan(seg_len, *, tile, n_slots):
    n_tiles = seg_len.shape[0]
    xs_rows = _xs_rows(tile)
    nb = n_slots // MOE_BM
    seg_off = jnp.cumsum(seg_len, axis=1) - seg_len
    tile_rows = jnp.sum(seg_len, axis=1)
    rows_e = jnp.sum(seg_len, axis=0)
    padded = (rows_e + MOE_BM - 1) // MOE_BM * MOE_BM
    pend = jnp.cumsum(padded)
    pstart = pend - padded
    xb_row = pstart[None, :] + jnp.cumsum(seg_len, axis=0) - seg_len
    n_used = pend[-1] // MOE_BM
    blk = jnp.minimum(jnp.arange(nb, dtype=jnp.int32), n_used - 1)
    block_e = jnp.sum((pend[None, :] <= (blk * MOE_BM)[:, None]).astype(jnp.int32), axis=1)
    block_e = jnp.minimum(block_e, N_EXPERTS - 1).astype(jnp.int32)
    n_spare = nb - (2 * n_tiles * tile) // MOE_BM
    spare = n_used + jnp.arange(n_spare, dtype=jnp.int32)
    xb_zero_row = jnp.concatenate([pstart + rows_e, jnp.minimum(spare, nb - 1) * MOE_BM])
    xb_zero_len = jnp.concatenate([padded - rows_e, jnp.where(spare < nb, MOE_BM, 0)])
    return seg_off, xb_row, xb_zero_row, xb_zero_len, block_e, n_used.astype(jnp.int32).reshape(1)


def kernel(x, c, ctx, c_ctx, w_mod, b_mod, norm1, norm2, w_in, w_gk2, b_gk, w_pool, pool_scale, gla_gain,
           w_out, w_router, router_bias, w1, w3, w2, norm_f):
    batch, seq, d = x.shape
    ctx_len = ctx.shape[1]
    depth = w_mod.shape[0]
    n_ctx = batch * ctx_len
    t_all = n_ctx + batch * seq
    def row_tile(limit):
        return max(t for t in (1024, 512, 256) if t <= limit and n_ctx % t == 0 and seq % t == 0)

    tile_a = row_tile(1024)
    tile_b = row_tile(512)

    def mod_row_for(tile):
        nct, per_b = n_ctx // tile, seq // tile
        return lambda i: jnp.where(i < nct, 0, 1 + (i - nct) // per_b)

    xf = jnp.concatenate([ctx.reshape(n_ctx, d), x.reshape(batch * seq, d)], axis=0)

    cvecs = jnp.zeros((16, d), F32).at[0].set(c_ctx).at[1:1 + batch].set(c)
    mods = _modulation(cvecs, w_mod, b_mod).reshape(depth, 16, N_MOD, 1, d)

    w2g = jnp.zeros((depth, 128, 2 * GLA_QK), F32)
    w2g = w2g.at[:, 0:GATE_RANK, 0:GLA_QK].set(w_gk2[:, 0])
    w2g = w2g.at[:, GATE_RANK:2 * GATE_RANK, GLA_QK:].set(w_gk2[:, 1])
    bgk = b_gk.reshape(depth, 1, 2 * GLA_QK)
    w_out_b = w_out.astype(BF16)
    w_pool_b = w_pool.astype(BF16)
    w1b, w3b, w2b = w1.astype(BF16), w3.astype(BF16), w2.astype(BF16)
    w_router_t = w_router.T
    rbias = router_bias.reshape(N_EXPERTS, 1)
    g1 = lambda a: a.reshape(1, -1)

    out = None
    for l in range(depth):
        last = l == depth - 1
        pbf, lg = _inproj(xf, mods[l], g1(norm1[l]), w_in, l, w2g[l], bgk[l],
                          tile=tile_a, mod_row=mod_row_for(tile_a))
        att_c, att_x = _gla(pbf, lg, gla_gain[l], batch=batch, ctx_len=ctx_len, seq=seq)
        pool_c, pool_x = _pool(pbf, w_pool_b[l], pool_scale[l], batch=batch, ctx_len=ctx_len, seq=seq)
        first_tile = n_ctx // tile_b if last else 0
        n_tiles = t_all // tile_b - first_tile
        x1, xs, slots, wts, seg_len = _outproj(
            xf, pool_c, pool_x, att_c, att_x, mods[l], g1(norm2[l]), w_out_b[l], w_router_t, rbias,
            tile=tile_b, mod_row=mod_row_for(tile_b), first_tile=first_tile, n_tiles=n_tiles)
        seg_len = seg_len[:, :, 0]
        max_rows = n_tiles * (2 * tile_b + N_EXPERTS * (SEG_ALIGN - 1)) + N_EXPERTS * (MOE_BM - 1)
        n_slots = -(-max_rows // MOE_BM) * MOE_BM
        seg_off, xb_row, xb_zero_row, xb_zero_len, block_e, n_used = _slot_plan(
            seg_len, tile=tile_b, n_slots=n_slots)
        xb = _dispatch(xs, seg_off, xb_row, seg_len, xb_zero_row, xb_zero_len, n_slots, tile=tile_b)
        yb = _experts(xb, block_e, n_used, w1b[l], w3b[l], w2b[l])
        res = _combine(x1, yb, seg_off, xb_row, seg_len, slots.T, wts.T, mods[l], g1(norm_f), tile=tile_b,
                       mod_row=mod_row_for(tile_b), first_tile=first_tile, final_norm=last)
        if last:
            out = res
        else:
            xf = res
    return out.reshape(batch, seq, d)
```

```python
import functools

import numpy as np
import jax
import jax.numpy as jnp
from jax import lax
from jax.experimental import pallas as pl
from jax.experimental.pallas import tpu as pltpu

F32 = jnp.float32
BF16 = jnp.bfloat16

GRID_W = 64
POOL_GROUPS = 4
POOL_WINDOWS = (2, 4, 8, 16)
GLA_HEADS = 4
GLA_DK = 64
GLA_DV = 128
GLA_QK = GLA_HEADS * GLA_DK
GATE_RANK = 16
GATE_NORM = 16.0
N_EXPERTS = 16
N_GROUPS = 4
EXPERTS_PER_GROUP = 4
EPS = 1e-6
N_MOD = 6

VMEM_LIMIT_BYTES = 56 * 1024 * 1024
SPAN = 256
SUB = 32
N_SUB = SPAN // SUB
LOG2_E = 1.4426950408889634
EXP2_CLAMP = 115.0
DIAG_W = 128
GLA_KW = DIAG_W + N_SUB * GLA_DK
MOE_BM = 256
SEG_ALIGN = 16


def _cparams(n_axes):
    return pltpu.CompilerParams(
        dimension_semantics=("arbitrary",) * n_axes, vmem_limit_bytes=VMEM_LIMIT_BYTES)


def _split(a):
    hi = a.astype(BF16)
    lo = (a - hi.astype(F32)).astype(BF16)
    return hi, lo


def _dot(a, b):
    return jnp.dot(a, b, preferred_element_type=F32)


def _dot_nt(a, b):
    return lax.dot_general(a, b, (((1,), (1,)), ((), ())), preferred_element_type=F32)


def _dot_tn(a, b):
    return lax.dot_general(a, b, (((0,), (0,)), ((), ())), preferred_element_type=F32)


def _dot3(a, b):
    ah, al = _split(a)
    bh, bl = _split(b)
    return _dot(ah, bh) + (_dot(ah, bl) + _dot(al, bh))


def _dot3_nt(a, b):
    ah, al = _split(a)
    bh, bl = _split(b)
    return _dot_nt(ah, bh) + (_dot_nt(ah, bl) + _dot_nt(al, bh))


def _dot_exact_lhs(m_bf16, b, terms=3):
    out = None
    rest = b
    for _ in range(terms):
        part = rest.astype(BF16)
        rest = rest - part.astype(F32)
        prod = _dot(m_bf16, part)
        out = prod if out is None else out + prod
    return out


def _silu(x):
    return x * jax.nn.sigmoid(x)


def _mod_kernel(c_ref, w_ref, b_ref, o_ref):
    o_ref[0] = _dot3(_silu(c_ref[...]), w_ref[0]) + b_ref[0]


def _modulation(cvecs, w_mod, b_mod):
    depth, d, _ = w_mod.shape
    rows = cvecs.shape[0]
    return pl.pallas_call(
        _mod_kernel,
        grid=(depth, N_MOD),
        in_specs=[
            pl.BlockSpec((rows, d), lambda l, j: (0, 0)),
            pl.BlockSpec((1, d, d), lambda l, j: (l, 0, j)),
            pl.BlockSpec((1, 1, d), lambda l, j: (l, 0, j)),
        ],
        out_specs=pl.BlockSpec((1, rows, d), lambda l, j: (l, 0, j)),
        out_shape=jax.ShapeDtypeStruct((depth, rows, N_MOD * d), F32),
        compiler_params=_cparams(2),
        name="adaln_vectors",
    )(cvecs, w_mod, b_mod.reshape(depth, 1, N_MOD * d))


def _norm_mod(x, gain, scale, shift):
    y = x * lax.rsqrt(jnp.mean(x * x, axis=-1, keepdims=True) + EPS)
    return (y * gain) * (1.0 + scale) + shift


def _inproj_kernel(x_ref, scale_ref, shift_ref, gain_ref, win_ref, w2_ref, bgk_ref,
                   p_ref, lg_ref, w_ref, wr_ref, *, n_chunk):
    n_main = p_ref.shape[1]
    width = n_main // n_chunk

    @pl.when(pl.program_id(0) == 0)
    def _():
        for n in range(n_chunk):
            cols = slice(n * width, (n + 1) * width)
            w_ref[:, cols] = win_ref[:, cols].astype(BF16)
        wr_ref[...] = jnp.zeros_like(wr_ref)
        wr_ref[:, 0:2 * GATE_RANK] = win_ref[:, n_main:].astype(BF16)

    h = _norm_mod(x_ref[...], gain_ref[...], scale_ref[...], shift_ref[...])
    hb = h.astype(BF16)
    for n in range(n_chunk):
        cols = slice(n * width, (n + 1) * width)
        p_ref[:, cols] = _dot(hb, w_ref[:, cols]).astype(BF16)
    r = _dot(hb, wr_ref[...])
    z = _dot3(r, w2_ref[...]) + bgk_ref[...]
    log_sig = jnp.minimum(z, 0.0) - jnp.log(1.0 + jnp.exp(-jnp.abs(z)))
    lg_ref[...] = log_sig * (LOG2_E / GATE_NORM)


def _inproj(xf, modl, gain, w_in_all, layer, w2, bgk, *, tile, mod_row):
    t, d = xf.shape
    n_main = w_in_all.shape[2] - 2 * GATE_RANK
    n_gate = w2.shape[1]
    mod_spec = lambda m: pl.BlockSpec((None, None, 1, d), lambda i: (mod_row(i), m, 0, 0))
    full = lambda a: pl.BlockSpec(a.shape, lambda i: (0,) * a.ndim)
    return pl.pallas_call(
        functools.partial(_inproj_kernel, n_chunk=4),
        grid=(t // tile,),
        in_specs=[
            pl.BlockSpec((tile, d), lambda i: (i, 0)),
            mod_spec(1), mod_spec(0), full(gain),
            pl.BlockSpec((None,) + w_in_all.shape[1:], lambda i: (layer, 0, 0)), full(w2), full(bgk),
        ],
        out_specs=[
            pl.BlockSpec((tile, n_main), lambda i: (i, 0)),
            pl.BlockSpec((tile, n_gate), lambda i: (i, 0)),
        ],
        out_shape=[
            jax.ShapeDtypeStruct((t, n_main), BF16),
            jax.ShapeDtypeStruct((t, n_gate), F32),
        ],
        scratch_shapes=[pltpu.VMEM((d, n_main), BF16), pltpu.VMEM((d, 128), BF16)],
        compiler_params=_cparams(1),
        name="norm_inproj",
    )(xf, modl, modl, gain, w_in_all, w2, bgk)


def _store_heads(dst_ref, rows, col0, val):
    half = (col0 // GLA_DK) & 1
    plain = val.astype(BF16)
    turned = pltpu.roll(val, GLA_DK, axis=1).astype(BF16)
    for h in range(GLA_HEADS):
        if (h & 1) == half:
            src = plain[:, h * GLA_DK:(h + 1) * GLA_DK]
        else:
            hh = (h + 1) % GLA_HEADS
            src = turned[:, hh * GLA_DK:(hh + 1) * GLA_DK]
        dst_ref[h, rows, col0:col0 + GLA_DK] = src


def _gla_span(q_ref, k_ref, v_ref, lg_ref, o_ref, row0, st_ref, qh_ref, kh_ref, b_ref, qf_ref, kf_ref,
              tri_ref, dmask_ref, smask_ref, *, rev):
    rows = pl.ds(row0, SPAN)
    dcol = slice(rev * GLA_QK, (rev + 1) * GLA_QK)
    b_ref[...] = _dot_exact_lhs(tri_ref[rev], lg_ref[rows, dcol])
    qf_ref[...] = q_ref[rows, :].astype(F32) * (GLA_DK ** -0.5)
    kf_ref[...] = k_ref[rows, :].astype(F32)
    e_tot = b_ref[0:1, :] if rev else b_ref[SPAN - 1:SPAN, :]

    for s in range(N_SUB - 1):
        j = s + 1 if rev else s
        ref_row = j * SUB if rev else j * SUB + SUB - 1
        e_j = b_ref[ref_row:ref_row + 1, :]
        blk = slice(j * SUB, (j + 1) * SUB)
        qrows = slice(0, j * SUB) if rev else slice((j + 1) * SUB, SPAN)
        col0 = DIAG_W + s * GLA_DK
        _store_heads(qh_ref, qrows, col0, qf_ref[qrows, :] * jnp.exp2(b_ref[qrows, :] - e_j))
        _store_heads(kh_ref, blk, col0, kf_ref[blk, :] * jnp.exp2(e_j - b_ref[blk, :]))
    for i in range(N_SUB):
        blk = slice(i * SUB, (i + 1) * SUB)
        mid = i * SUB + SUB // 2
        e_m = b_ref[mid:mid + 1, :]
        _store_heads(qh_ref, blk, 0, qf_ref[blk, :] * jnp.exp2(jnp.minimum(b_ref[blk, :] - e_m, EXP2_CLAMP)))
        _store_heads(kh_ref, blk, 0, kf_ref[blk, :] * jnp.exp2(jnp.minimum(e_m - b_ref[blk, :], EXP2_CLAMP)))

    st = st_ref[...]
    qd = (qf_ref[...] * jnp.exp2(b_ref[...])).astype(BF16)
    o_state = _dot_nt(qd, st.astype(BF16))
    in_block = dmask_ref[rev] != 0.0
    for h in range(GLA_HEADS):
        vc = slice(h * GLA_DV, (h + 1) * GLA_DV)
        a_diag = _dot_nt(qh_ref[h, :, 0:DIAG_W], kh_ref[h, :, 0:DIAG_W])
        a_cross = _dot_nt(qh_ref[h, :, DIAG_W:], kh_ref[h, :, DIAG_W:])
        a = (a_cross + jnp.where(in_block, a_diag, 0.0)).astype(BF16)
        o_ref[rows, vc] += _dot(a, v_ref[rows, vc]) + o_state[:, vc]
    kd =(kf_ref[...] * jnp.exp2(e_tot - b_ref[...])).astype(BF16)
    upd = _dot_tn(v_ref[rows, :], kd)
    st_ref[...] = st * jnp.exp2(e_tot) + upd * smask_ref[...]


def _gla_finish(o_ref, g_ref, gain, out_ref, n_rows):
    def body(c, carry):
        rows = pl.ds(pl.multiple_of(c * SPAN, SPAN), SPAN)
        for h in range(GLA_HEADS):
            vc = slice(h * GLA_DV, (h + 1) * GLA_DV)
            o = o_ref[rows, vc]
            y = o * lax.rsqrt(jnp.mean(o * o, axis=-1, keepdims=True) + EPS) * gain
            g = g_ref[rows, vc].astype(F32)
            out_ref[rows, vc] = (y * _silu(g)).astype(BF16)
        return carry
    lax.fori_loop(0, n_rows // SPAN, body, 0)


def _gla_kernel(qc, kc, vc, gc, lgc, qx, kx, vx, gx, lgx, gain_ref, tri, dmask, smask, outc, outx,
                oc, ox, *per_direction):
    n_c = qc.shape[0] // SPAN
    n_x = qx.shape[0] // SPAN
    names = ("st_ref", "qh_ref", "kh_ref", "b_ref", "qf_ref", "kf_ref")
    scratch = [dict(zip(names, per_direction[r * len(names):(r + 1) * len(names)])) for r in (0, 1)]
    for ref in (oc, ox) + tuple(s[n] for s in scratch for n in ("st_ref", "qh_ref", "kh_ref")):
        ref[...] = jnp.zeros_like(ref)
    spans = [functools.partial(_gla_span, tri_ref=tri, dmask_ref=dmask, smask_ref=smask, rev=rev,
                               **scratch[rev]) for rev in (0, 1)]
    for s in range(n_c):
        spans[0](qc, kc, vc, lgc, oc, s * SPAN)
        spans[1](qc, kc, vc, lgc, oc, (n_c - 1 - s) * SPAN)

    def body(s, carry):
        spans[0](qx, kx, vx, lgx, ox, pl.multiple_of(s * SPAN, SPAN))
        spans[1](qx, kx, vx, lgx, ox, pl.multiple_of((n_x - 1 - s) * SPAN, SPAN))
        return carry
    lax.fori_loop(0, n_x, body, 0)
    gain = gain_ref[...]
    _gla_finish(oc, gc, gain, outc, qc.shape[0])
    _gla_finish(ox, gx, gain, outx, qx.shape[0])


def _gla(pbf, lg, gla_gain, *, batch, ctx_len, seq):
    nxb = (batch * ctx_len) // seq
    assert nxb * seq == batch * ctx_len
    hv = GLA_HEADS * GLA_DV
    cspec = lambda w, j: pl.BlockSpec((ctx_len, w), lambda b: (b, j))
    xspec = lambda w, j: pl.BlockSpec((seq, w), lambda b: (nxb + b, j))
    full = lambda a: pl.BlockSpec(a.shape, lambda b: (0,) * a.ndim)
    pos = np.arange(SPAN)
    lower = pos[None, :] <= pos[:, None]
    same = (pos[None, :] // SUB) == (pos[:, None] // SUB)
    tri = jnp.asarray(np.stack([lower, lower.T]).astype(np.float32), BF16)
    dmask = jnp.asarray(np.stack([lower & same, lower.T & same]).astype(np.float32))
    smask = jnp.asarray(((np.arange(hv)[:, None] // GLA_DV) == (np.arange(GLA_QK)[None, :] // GLA_DK))
                        .astype(np.float32))
    return pl.pallas_call(
        _gla_kernel,
        grid=(batch,),
        in_specs=[
            cspec(GLA_QK, 2), cspec(GLA_QK, 3), cspec(hv, 2), cspec(hv, 3), cspec(2 * GLA_QK, 0),
            xspec(GLA_QK, 2), xspec(GLA_QK, 3), xspec(hv, 2), xspec(hv, 3), xspec(2 * GLA_QK, 0),
            pl.BlockSpec((1, GLA_DV), lambda b: (0, 0)), full(tri), full(dmask), full(smask),
        ],
        out_specs=[
            pl.BlockSpec((ctx_len, hv), lambda b: (b, 0)),
            pl.BlockSpec((seq, hv), lambda b: (b, 0)),
        ],
        out_shape=[
            jax.ShapeDtypeStruct((batch * ctx_len, hv), BF16),
            jax.ShapeDtypeStruct((batch * seq, hv), BF16),
        ],
        scratch_shapes=[
            pltpu.VMEM((ctx_len, hv), F32),
            pltpu.VMEM((seq, hv), F32),
        ] + 2 * [
            pltpu.VMEM((hv, GLA_QK), F32),
            pltpu.VMEM((GLA_HEADS, SPAN, GLA_KW), BF16),
            pltpu.VMEM((GLA_HEADS, SPAN, GLA_KW), BF16),
            pltpu.VMEM((SPAN, GLA_QK), F32),
            pltpu.VMEM((SPAN, GLA_QK), F32),
            pltpu.VMEM((SPAN, GLA_QK), F32),
        ],
        compiler_params=_cparams(1),
        name="gla_bidirectional",
    )(pbf, pbf, pbf, pbf, lg, pbf, pbf, pbf, pbf, lg, gla_gain.reshape(1, GLA_DV), tri, dmask, smask)


def _window_offsets(w):
    return range(-(w // 2), w - w // 2)


def _band(w, period):
    ri = lax.broadcasted_iota(jnp.int32, (SPAN, SPAN), 0)
    ci = lax.broadcasted_iota(jnp.int32, (SPAN, SPAN), 1)
    d = ci - ri
    shift = period.bit_length() - 1
    assert period == 1 << shift
    same_row = (ri >> shift) == (ci >> shift)
    inside = jnp.where(d >= -(w // 2), jnp.where(d <= w - w // 2 - 1, 1.0, 0.0), 0.0)
    return jnp.where(same_row, inside, 0.0).astype(BF16)


def _pool_kernel(uc_ref, ux_ref, invc_ref, invx_ref, wp_ref, ps_ref, outc_ref, outx_ref, pad_ref,
                 *, pad_rows):
    n_x = ux_ref.shape[0]
    gw = wp_ref.shape[1]
    pad_ref[0:pad_rows, :] = jnp.zeros((pad_rows, gw), F32)
    pad_ref[pad_rows + n_x:, :] = jnp.zeros((pad_rows, gw), F32)
    for g, w in enumerate(POOL_WINDOWS):
        gc = slice(g * gw, (g + 1) * gw)
        wp = wp_ref[g]
        ps = ps_ref[:, gc]
        band_c = _band(w, SPAN)
        for t in range(uc_ref.shape[0] // SPAN):
            rows = slice(t * SPAN, (t + 1) * SPAN)
            u = uc_ref[rows, gc].astype(F32)
            pooled = _dot_exact_lhs(band_c, u, terms=2) * invc_ref[g, rows, :] - u
            outc_ref[rows, gc] = (_dot(pooled.astype(BF16), wp) * ps).astype(BF16)
        pad_ref[pad_rows:pad_rows + n_x, :] = ux_ref[:, gc].astype(F32)
        band_x = _band(w, GRID_W)

        def body(t, carry):
            r0 = pl.multiple_of(t * SPAN, SPAN)
            acc = jnp.zeros((SPAN, gw), F32)
            for dr in _window_offsets(w):
                acc = acc + pad_ref[pl.ds(pad_rows + r0 + dr * GRID_W, SPAN), :]
            rows = pl.ds(r0, SPAN)
            u = pad_ref[pl.ds(pad_rows + r0, SPAN), :]
            pooled = _dot_exact_lhs(band_x, acc, terms=2) * invx_ref[g, rows, :] - u
            outx_ref[rows, gc] = (_dot(pooled.astype(BF16), wp) * ps).astype(BF16)
            return carry
        lax.fori_loop(0, n_x // SPAN, body, 0, unroll=4)


def _inv_counts(length, rows_of):
    out = []
    for w in POOL_WINDOWS:
        cnt = np.ones((length,), np.float64)
        for axis_len, coord in rows_of(length):
            lo = np.clip(coord - w // 2, 0, axis_len)
            hi = np.clip(coord + w - w // 2, 0, axis_len)
            cnt = cnt * (hi - lo)
        out.append(1.0 / cnt)
    return np.broadcast_to(np.stack(out)[:, :, None], (len(POOL_WINDOWS), length, 128)).astype(np.float32)


def _pool(pbf, w_pool, pool_scale, *, batch, ctx_len, seq):
    nxb = (batch * ctx_len) // seq
    pw = w_pool.shape[0] * w_pool.shape[1]
    pad_rows = (max(POOL_WINDOWS) // 2) * GRID_W
    inv_c = jnp.asarray(_inv_counts(ctx_len, lambda n: [(n, np.arange(n))]))
    inv_x = jnp.asarray(_inv_counts(
        seq, lambda n: [(n // GRID_W, np.arange(n) // GRID_W), (GRID_W, np.arange(n) % GRID_W)]))
    full = lambda a: pl.BlockSpec(a.shape, lambda b: (0,) * a.ndim)
    return pl.pallas_call(
        functools.partial(_pool_kernel, pad_rows=pad_rows),
        grid=(batch,),
        in_specs=[
            pl.BlockSpec((ctx_len, pw), lambda b: (b, 0)),
            pl.BlockSpec((seq, pw), lambda b: (nxb + b, 0)),
            full(inv_c), full(inv_x), full(w_pool),
            pl.BlockSpec((1, pw), lambda b: (0, 0)),
        ],
        out_specs=[
            pl.BlockSpec((ctx_len, pw), lambda b: (b, 0)),
            pl.BlockSpec((seq, pw), lambda b: (b, 0)),
        ],
        out_shape=[
            jax.ShapeDtypeStruct((batch * ctx_len, pw), BF16),
            jax.ShapeDtypeStruct((batch * seq, pw), BF16),
        ],
        scratch_shapes=[pltpu.VMEM((seq + 2 * pad_rows, w_pool.shape[1]), F32)],
        compiler_params=_cparams(1),
        name="pool_mixer",
    )(pbf, pbf, inv_c, inv_x, w_pool, pool_scale.reshape(1, pw))


def _route(sb, s):
    def rank_in_group(vals):
        ranks = []
        for i, vi in enumerate(vals):
            r = jnp.zeros(vi.shape, jnp.int32)
            for j, vj in enumerate(vals):
                if j == i:
                    continue
                ahead = (vj >= vi) if j < i else (vj > vi)
                r = r + jnp.where(ahead, 1, 0)
            ranks.append(r)
        return ranks

    best = None
    for g in range(N_GROUPS):
        vals = sb[g * EXPERTS_PER_GROUP:(g + 1) * EXPERTS_PER_GROUP]
        svals = s[g * EXPERTS_PER_GROUP:(g + 1) * EXPERTS_PER_GROUP]
        score = None
        for i in range(EXPERTS_PER_GROUP):
            for j in range(i + 1, EXPERTS_PER_GROUP):
                pair = vals[i] + vals[j]
                score = pair if score is None else jnp.maximum(score, pair)
        ranks = rank_in_group(vals)
        e1 = jnp.zeros(score.shape, jnp.int32)
        e2 = jnp.zeros(score.shape, jnp.int32)
        w1 = jnp.zeros(score.shape, F32)
        w2 = jnp.zeros(score.shape, F32)
        for i in range(EXPERTS_PER_GROUP):
            e1 = jnp.where(ranks[i] == 0, g * EXPERTS_PER_GROUP + i, e1)
            e2 = jnp.where(ranks[i] == 1, g * EXPERTS_PER_GROUP + i, e2)
            w1 = jnp.where(ranks[i] == 0, svals[i], w1)
            w2 = jnp.where(ranks[i] == 1, svals[i], w2)
        cand = (score, e1, e2, w1, w2)
        if best is None:
            best = cand
        else:
            take = cand[0] > best[0]
            best = tuple(jnp.where(take, c, b) for c, b in zip(cand, best))
    _, e1, e2, w1, w2 = best
    tot = w1 + w2
    return e1, e2, w1 / tot, w2 / tot


def _outproj_kernel(x_ref, pc_ref, px_ref, ac_ref, ax_ref, gate_ref, scale_ref, shift_ref, gain_ref,
                    wo_ref, wrt_ref, rb_ref, before_ref, below_ref, x1_ref, xs_ref, slot_ref, wt_ref, len_ref,
                    *, n_ctx_tiles, half):
    is_ctx = pl.program_id(0) < n_ctx_tiles
    pool = jnp.where(is_ctx, pc_ref[...], px_ref[...])
    att = jnp.where(is_ctx, ac_ref[...], ax_ref[...])
    y = _dot(pool, wo_ref[0:half, :]) + _dot(att, wo_ref[half:, :])
    x1 = x_ref[...] + gate_ref[...] * y
    x1_ref[...] = x1
    h2 = _norm_mod(x1, gain_ref[...], scale_ref[...], shift_ref[...])
    logits = _dot3_nt(wrt_ref[...], h2)
    s = jax.nn.sigmoid(logits)
    sb = s + rb_ref[...]
    rows = lambda a: [a[i:i + 1, :] for i in range(N_EXPERTS)]
    e1, e2, w1, w2 = _route(rows(sb), rows(s))
    wt_ref[0:1, :] = w1
    wt_ref[1:2, :] = w2
    ie = lax.broadcasted_iota(jnp.int32, logits.shape, 0)
    oh1 = jnp.where(ie == e1, 1.0, 0.0)
    oh2 = jnp.where(ie == e2, 1.0, 0.0)
    c1 = jnp.sum(oh1, axis=1, keepdims=True)
    c2 = jnp.sum(oh2, axis=1, keepdims=True)
    seg_len = jnp.ceil((c1 + c2) * (1.0 / SEG_ALIGN)) * SEG_ALIGN
    seg_len_l = jnp.broadcast_to(seg_len, len_ref.shape)
    seg_off = _dot(below_ref[...], seg_len_l.astype(BF16))[:, 0:1]
    before1 = _dot(oh1.astype(BF16), before_ref[...]) + seg_off
    before2 = _dot(oh2.astype(BF16), before_ref[...]) + (seg_off + c1)
    slot1 = jnp.sum(oh1 * before1, axis=0, keepdims=True).astype(jnp.int32)
    slot2 = jnp.sum(oh2 * before2, axis=0, keepdims=True).astype(jnp.int32)
    slot_ref[0:1, :] = slot1
    slot_ref[1:2, :] = slot2
    len_ref[...] = seg_len_l.astype(jnp.int32)
    si = lax.broadcasted_iota(jnp.int32, (xs_ref.shape[0], slot1.shape[1]), 0)
    perm = jnp.where(si == slot1, 1.0, 0.0) + jnp.where(si == slot2, 1.0, 0.0)
    xs_ref[...] = _dot(perm.astype(BF16), h2.astype(BF16)).astype(BF16)


def _outproj(xf, pool_c, pool_x, att_c, att_x, modl, gain2, w_out, w_router_t, router_bias,
             *, tile, mod_row, first_tile, n_tiles):
    t, d = xf.shape
    half = pool_c.shape[1]
    n_ctx_tiles = pool_c.shape[0] // tile
    cidx = lambda i: (jnp.minimum(i + first_tile, n_ctx_tiles - 1), 0)
    xidx = lambda i: (jnp.maximum(i + first_tile - n_ctx_tiles, 0), 0)
    mod_spec = lambda m: pl.BlockSpec((None, None, 1, d), lambda i: (mod_row(i + first_tile), m, 0, 0))
    full = lambda a: pl.BlockSpec(a.shape, lambda i: (0,) * a.ndim)
    tok = lambda w: pl.BlockSpec((tile, w), lambda i: (i, 0))
    lane = pl.BlockSpec((2, tile), lambda i: (0, i))
    t_out = n_tiles * tile
    before = jnp.asarray(np.triu(np.ones((tile, tile), np.float32), 1), BF16)
    below = jnp.asarray(np.tril(np.ones((N_EXPERTS, N_EXPERTS), np.float32), -1), BF16)
    xs_rows = _xs_rows(tile)
    return pl.pallas_call(
        functools.partial(_outproj_kernel, n_ctx_tiles=n_ctx_tiles - first_tile, half=half),
        grid=(n_tiles,),
        in_specs=[
            pl.BlockSpec((tile, d), lambda i: (i + first_tile, 0)),
            pl.BlockSpec((tile, half), cidx), pl.BlockSpec((tile, half), xidx),
            pl.BlockSpec((tile, half), cidx), pl.BlockSpec((tile, half), xidx),
            mod_spec(2), mod_spec(4), mod_spec(3), full(gain2), full(w_out), full(w_router_t),
            full(router_bias), full(before), full(below),
        ],
        out_specs=[tok(d), pl.BlockSpec((xs_rows, d), lambda i: (i, 0)), lane, lane,
                   pl.BlockSpec((None, N_EXPERTS, 128), lambda i: (i, 0, 0))],
        out_shape=[
            jax.ShapeDtypeStruct((t_out, d), F32),
            jax.ShapeDtypeStruct((n_tiles * xs_rows, d), BF16),
            jax.ShapeDtypeStruct((2, t_out), jnp.int32),
            jax.ShapeDtypeStruct((2, t_out), F32),
            jax.ShapeDtypeStruct((n_tiles, N_EXPERTS, 128), jnp.int32),
        ],
        compiler_params=_cparams(1),
        name="outproj_router",
    )(xf, pool_c, pool_x, att_c, att_x, modl, modl, modl, gain2, w_out, w_router_t, router_bias,
      before, below)


def _xs_rows(tile):
    return 2 * tile + N_EXPERTS * SEG_ALIGN


def _for_each_unit(n, fn):
    def body(u, carry):
        fn(u * SEG_ALIGN)
        return carry
    lax.fori_loop(0, n // SEG_ALIGN, body, 0)


def _expert_kernel(be_ref, nu_ref, usrc_ref, xs_ref, w1_ref, w3_ref, w2_ref, yb_ref,
                   xbuf, wb1, wb3, wb2, sem):
    i = pl.program_id(0)
    n_used = nu_ref[0]
    units = MOE_BM // SEG_ALIGN

    def fetch(blk, half, action):
        for u in range(units):
            src = pl.multiple_of(usrc_ref[blk * units + u], SEG_ALIGN)
            action(pltpu.make_async_copy(xs_ref.at[pl.ds(src, SEG_ALIGN), :],
                                         xbuf.at[half, pl.ds(u * SEG_ALIGN, SEG_ALIGN), :], sem.at[half]))

    half = i % 2
    pl.when(i == 0)(lambda: fetch(i, 0, lambda c: c.start()))
    pl.when(i + 1 < n_used)(lambda: fetch(i + 1, 1 - half, lambda c: c.start()))
    used = i < n_used
    new_expert = (i == 0) | (be_ref[i] != be_ref[jnp.maximum(i - 1, 0)])

    @pl.when(used & new_expert)
    def _():
        for src, dst in ((w1_ref, wb1), (w3_ref, wb3), (w2_ref, wb2)):
            step = src.shape[0] // 4
            for r in range(0, src.shape[0], step):
                dst[r:r + step, :] = src[r:r + step, :].astype(BF16)

    @pl.when(used)
    def _():
        fetch(i, half, lambda c: c.wait())
        x = xbuf[half]
        a = _dot(x, wb1[...])
        b = _dot(x, wb3[...])
        yb_ref[...] = _dot((_silu(a) * b).astype(BF16), wb2[...]).astype(BF16)

    @pl.when(jnp.logical_not(used))
    def _():
        yb_ref[...] = jnp.zeros_like(yb_ref)


def _experts(xs, unit_src, block_e, n_used, w1, w3, w2, layer, n_slots):
    d = xs.shape[1]
    de = w1.shape[3]
    nb = n_slots // MOE_BM
    wspec = lambda a: pl.BlockSpec((None, None) + a.shape[2:], lambda i, be, nu, us: (layer, be[i], 0, 0))
    return pl.pallas_call(
        _expert_kernel,
        grid_spec=pltpu.PrefetchScalarGridSpec(
            num_scalar_prefetch=3,
            grid=(nb,),
            in_specs=[pl.BlockSpec(memory_space=pl.ANY), wspec(w1), wspec(w3), wspec(w2)],
            out_specs=pl.BlockSpec((MOE_BM, d), lambda i, be, nu, us: (i, 0)),
            scratch_shapes=[pltpu.VMEM((2, MOE_BM, d), xs.dtype), pltpu.VMEM((d, de), BF16),
                            pltpu.VMEM((d, de), BF16), pltpu.VMEM((de, d), BF16),
                            pltpu.SemaphoreType.DMA((2,))],
        ),
        out_shape=jax.ShapeDtypeStruct((n_slots, d), BF16),
        compiler_params=_cparams(1),
        name="moe_experts",
    )(block_e, n_used, unit_src.astype(jnp.int32), xs, w1, w3, w2)


def _combine_kernel(seg_off, src_row, seg_len, x_ref, slot_ref, wt_ref, gate_ref, gainf_ref, yb_ref, o_ref,
                    buf, sem, *, final_norm):
    i = pl.program_id(0)
    n = pl.num_programs(0)

    def fetch(tile_idx, half, action):
        for e in range(N_EXPERTS):
            j = tile_idx * N_EXPERTS + e

            def unit(offset, j=j):
                s0 = pl.multiple_of(src_row[j] + offset, SEG_ALIGN)
                d0 = pl.multiple_of(seg_off[j] + offset, SEG_ALIGN)
                action(pltpu.make_async_copy(yb_ref.at[pl.ds(s0, SEG_ALIGN), :],
                                             buf.at[half, pl.ds(d0, SEG_ALIGN), :], sem.at[half]))
            _for_each_unit(seg_len[j], unit)

    @pl.when(i == 0)
    def _():
        buf[...] = jnp.zeros_like(buf)
        fetch(i, 0, lambda c: c.start())

    half = i % 2
    pl.when(i + 1 < n)(lambda: fetch(i + 1, 1 - half, lambda c: c.start()))
    fetch(i, half, lambda c: c.wait())
    slot = slot_ref[...]
    wt = wt_ref[...]
    li = lax.broadcasted_iota(jnp.int32, (slot.shape[0], buf.shape[1]), 1)
    mix = jnp.where(li == slot[:, 0:1], wt[:, 0:1], 0.0) + jnp.where(li == slot[:, 1:2], wt[:, 1:2], 0.0)
    y = _dot(mix.astype(BF16), buf[half])
    x2 = x_ref[...] + gate_ref[...] * y
    if final_norm:
        x2 = x2 * lax.rsqrt(jnp.mean(x2 * x2, axis=-1, keepdims=True) + EPS) * gainf_ref[...]
    o_ref[...] = x2


def _combine(x1, yb, seg_off, src_row, seg_len, slot_cols, wt_cols, modl, gain_f, *, tile, mod_row,
             first_tile, final_norm):
    t, d = x1.shape
    flat = lambda a: a.reshape(-1).astype(jnp.int32)
    return pl.pallas_call(
        functools.partial(_combine_kernel, final_norm=final_norm),
        grid_spec=pltpu.PrefetchScalarGridSpec(
            num_scalar_prefetch=3,
            grid=(t // tile,),
            in_specs=[
                pl.BlockSpec((tile, d), lambda i, *_: (i, 0)),
                pl.BlockSpec((tile, 2), lambda i, *_: (i, 0)),
                pl.BlockSpec((tile, 2), lambda i, *_: (i, 0)),
                pl.BlockSpec((None, None, 1, d), lambda i, *_: (mod_row(i + first_tile), 5, 0, 0)),
                pl.BlockSpec((1, d), lambda i, *_: (0, 0)),
                pl.BlockSpec(memory_space=pl.ANY),
            ],
            out_specs=pl.BlockSpec((tile, d), lambda i, *_: (i, 0)),
            scratch_shapes=[pltpu.VMEM((2, _xs_rows(tile), d), yb.dtype), pltpu.SemaphoreType.DMA((2,))],
        ),
        out_shape=jax.ShapeDtypeStruct((t, d), F32),
        compiler_params=_cparams(1),
        name="moe_combine",
    )(flat(seg_off), flat(src_row), flat(seg_len), x1, slot_cols, wt_cols, modl, gain_f, yb)


def _slot_plan(seg_len, *, tile, n_slots):
    n_tiles = seg_len.shape[0]
    xs_rows = _xs_rows(tile)
    nb = n_slots // MOE_BM
    seg_off = jnp.cumsum(seg_len, axis=1) - seg_len
    rows_e = jnp.sum(seg_len, axis=0)
    padded = (rows_e + MOE_BM - 1) // MOE_BM * MOE_BM
    pend = jnp.cumsum(padded)
    pstart = pend - padded
    xb_row = pstart[None, :] + jnp.cumsum(seg_len, axis=0) - seg_len
    n_used = pend[-1] // MOE_BM
    blk = jnp.minimum(jnp.arange(nb, dtype=jnp.int32), n_used - 1)
    block_e = jnp.sum((pend[None, :] <= (blk * MOE_BM)[:, None]).astype(jnp.int32), axis=1)
    block_e = jnp.minimum(block_e, N_EXPERTS - 1).astype(jnp.int32)
    starts = xb_row.T.reshape(1, -1)
    lens = seg_len.T.reshape(1, -1)
    srcs = (jnp.arange(n_tiles, dtype=jnp.int32)[:, None] * xs_rows + seg_off).T.reshape(1, -1)
    unit_row = (jnp.arange(n_slots // SEG_ALIGN, dtype=jnp.int32) * SEG_ALIGN)[:, None]
    inside = (unit_row >= starts) & (unit_row < starts + lens)
    unit_src = jnp.sum(jnp.where(inside, srcs + unit_row - starts, 0), axis=1)
    unit_src = jnp.where(jnp.any(inside, axis=1), unit_src, xs_rows - SEG_ALIGN)
    return seg_off, xb_row, unit_src, block_e, n_used.astype(jnp.int32).reshape(1)


def kernel(x, c, ctx, c_ctx, w_mod, b_mod, norm1, norm2, w_in, w_gk2, b_gk, w_pool, pool_scale, gla_gain,
           w_out, w_router, router_bias, w1, w3, w2, norm_f):
    batch, seq, d = x.shape
    ctx_len = ctx.shape[1]
    depth = w_mod.shape[0]
    n_ctx = batch * ctx_len
    t_all = n_ctx + batch * seq
    def row_tile(limit):
        return max(t for t in (1024, 512, 256) if t <= limit and n_ctx % t == 0 and seq % t == 0)

    tile_a = row_tile(1024)
    tile_b = row_tile(512)

    def mod_row_for(tile):
        nct, per_b = n_ctx // tile, seq // tile
        return lambda i: jnp.where(i < nct, 0, 1 + (i - nct) // per_b)

    xf = jnp.concatenate([ctx.reshape(n_ctx, d), x.reshape(batch * seq, d)], axis=0)

    cvecs = jnp.zeros((16, d), F32).at[0].set(c_ctx).at[1:1 + batch].set(c)
    mods = _modulation(cvecs, w_mod, b_mod).reshape(depth, 16, N_MOD, 1, d)

    w2g = jnp.zeros((depth, 128, 2 * GLA_QK), F32)
    w2g = w2g.at[:, 0:GATE_RANK, 0:GLA_QK].set(w_gk2[:, 0])
    w2g = w2g.at[:, GATE_RANK:2 * GATE_RANK, GLA_QK:].set(w_gk2[:, 1])
    bgk = b_gk.reshape(depth, 1, 2 * GLA_QK)
    w_out_b = w_out.astype(BF16)
    w_pool_b = w_pool.astype(BF16)
    w_router_t = w_router.T
    rbias = router_bias.reshape(N_EXPERTS, 1)
    g1 = lambda a: a.reshape(1, -1)

    out = None
    for l in range(depth):
        last = l == depth - 1
        pbf, lg = _inproj(xf, mods[l], g1(norm1[l]), w_in, l, w2g[l], bgk[l],
                          tile=tile_a, mod_row=mod_row_for(tile_a))
        att_c, att_x = _gla(pbf, lg, gla_gain[l], batch=batch, ctx_len=ctx_len, seq=seq)
        pool_c, pool_x = _pool(pbf, w_pool_b[l], pool_scale[l], batch=batch, ctx_len=ctx_len, seq=seq)
        first_tile = n_ctx // tile_b if last else 0
        n_tiles = t_all // tile_b - first_tile
        x1, xs, slots, wts, seg_len = _outproj(
            xf, pool_c, pool_x, att_c, att_x, mods[l], g1(norm2[l]), w_out_b[l], w_router_t, rbias,
            tile=tile_b, mod_row=mod_row_for(tile_b), first_tile=first_tile, n_tiles=n_tiles)
        seg_len = seg_len[:, :, 0]
        max_rows = n_tiles * (2 * tile_b + N_EXPERTS * (SEG_ALIGN - 1)) + N_EXPERTS * (MOE_BM - 1)
        n_slots = -(-max_rows // MOE_BM) * MOE_BM
        seg_off, xb_row, unit_src, block_e, n_used = _slot_plan(seg_len, tile=tile_b, n_slots=n_slots)
        yb = _experts(xs, unit_src, block_e, n_used, w1, w3, w2, l, n_slots)
        res = _combine(x1, yb, seg_off, xb_row, seg_len, slots.T, wts.T, mods[l], g1(norm_f), tile=tile_b,
                       mod_row=mod_row_for(tile_b), first_tile=first_tile, final_norm=last)
        if last:
            out = res
        else:
            xf = res
    return out.reshape(batch, seq, d)
```

```python
import functools

import numpy as np
import jax
import jax.numpy as jnp
from jax import lax
from jax.experimental import pallas as pl
from jax.experimental.pallas import tpu as pltpu

F32 = jnp.float32
BF16 = jnp.bfloat16

GRID_W = 64
POOL_GROUPS = 4
POOL_WINDOWS = (2, 4, 8, 16)
GLA_HEADS = 4
GLA_DK = 64
GLA_DV = 128
GLA_QK = GLA_HEADS * GLA_DK
GATE_RANK = 16
GATE_NORM = 16.0
N_EXPERTS = 16
N_GROUPS = 4
EXPERTS_PER_GROUP = 4
EPS = 1e-6
N_MOD = 6

VMEM_LIMIT_BYTES = 56 * 1024 * 1024
SPAN = 256
SUB = 32
N_SUB = SPAN // SUB
LOG2_E = 1.4426950408889634
EXP2_CLAMP = 115.0
DIAG_W = 128
GLA_KW = DIAG_W + N_SUB * GLA_DK
MOE_BM = 512
EXPERT_ROWS = 256
SEG_ALIGN = 16


def _cparams(n_axes):
    return pltpu.CompilerParams(
        dimension_semantics=("arbitrary",) * n_axes, vmem_limit_bytes=VMEM_LIMIT_BYTES)


def _split(a):
    hi = a.astype(BF16)
    lo = (a - hi.astype(F32)).astype(BF16)
    return hi, lo


def _dot(a, b):
    return jnp.dot(a, b, preferred_element_type=F32)


def _dot_nt(a, b):
    return lax.dot_general(a, b, (((1,), (1,)), ((), ())), preferred_element_type=F32)


def _dot_tn(a, b):
    return lax.dot_general(a, b, (((0,), (0,)), ((), ())), preferred_element_type=F32)


def _dot3(a, b):
    ah, al = _split(a)
    bh, bl = _split(b)
    return _dot(ah, bh) + (_dot(ah, bl) + _dot(al, bh))


def _dot3_nt(a, b):
    ah, al = _split(a)
    bh, bl = _split(b)
    return _dot_nt(ah, bh) + (_dot_nt(ah, bl) + _dot_nt(al, bh))


def _dot_exact_lhs(m_bf16, b, terms=3):
    out = None
    rest = b
    for _ in range(terms):
        part = rest.astype(BF16)
        rest = rest - part.astype(F32)
        prod = _dot(m_bf16, part)
        out = prod if out is None else out + prod
    return out


def _silu(x):
    return x * jax.nn.sigmoid(x)


def _mod_kernel(c_ref, w_ref, b_ref, o_ref):
    o_ref[0] = _dot3(_silu(c_ref[...]), w_ref[0]) + b_ref[0]


def _modulation(cvecs, w_mod, b_mod):
    depth, d, _ = w_mod.shape
    rows = cvecs.shape[0]
    return pl.pallas_call(
        _mod_kernel,
        grid=(depth, N_MOD),
        in_specs=[
            pl.BlockSpec((rows, d), lambda l, j: (0, 0)),
            pl.BlockSpec((1, d, d), lambda l, j: (l, 0, j)),
            pl.BlockSpec((1, 1, d), lambda l, j: (l, 0, j)),
        ],
        out_specs=pl.BlockSpec((1, rows, d), lambda l, j: (l, 0, j)),
        out_shape=jax.ShapeDtypeStruct((depth, rows, N_MOD * d), F32),
        compiler_params=_cparams(2),
        name="adaln_vectors",
    )(cvecs, w_mod, b_mod.reshape(depth, 1, N_MOD * d))


def _norm_mod(x, gain, scale, shift):
    y = x * lax.rsqrt(jnp.mean(x * x, axis=-1, keepdims=True) + EPS)
    return (y * gain) * (1.0 + scale) + shift


def _row_specs(rows, tile, first_tile=0):
    xc, xx, first_latent_row = rows
    d = xc.shape[1]
    x_off = first_latent_row // tile
    nct = (xc.shape[0] // tile) if first_latent_row == 0 else x_off
    cspec = pl.BlockSpec((tile, d), lambda i, *_: (jnp.minimum(i + first_tile, nct - 1), 0))
    xspec = pl.BlockSpec((tile, d), lambda i, *_: (jnp.maximum(i + first_tile - nct, 0) + x_off, 0))
    return [cspec, xspec], [xc, xx], nct - first_tile


def _inproj_kernel(xc_ref, xx_ref, scale_ref, shift_ref, gain_ref, win_ref, w2_ref, bgk_ref,
                   p_ref, lg_ref, w_ref, wr_ref, *, n_chunk, n_ctx_tiles):
    n_main = p_ref.shape[1]
    width = n_main // n_chunk

    @pl.when(pl.program_id(0) == 0)
    def _():
        for n in range(n_chunk):
            cols = slice(n * width, (n + 1) * width)
            w_ref[:, cols] = win_ref[:, cols].astype(BF16)
        wr_ref[...] = jnp.zeros_like(wr_ref)
        wr_ref[:, 0:2 * GATE_RANK] = win_ref[:, n_main:].astype(BF16)

    x = jnp.where(pl.program_id(0) < n_ctx_tiles, xc_ref[...], xx_ref[...])
    h = _norm_mod(x, gain_ref[...], scale_ref[...], shift_ref[...])
    hb = h.astype(BF16)
    for n in range(n_chunk):
        cols = slice(n * width, (n + 1) * width)
        p_ref[:, cols] = _dot(hb, w_ref[:, cols]).astype(BF16)
    r = _dot(hb, wr_ref[...])
    z = _dot3(r, w2_ref[...]) + bgk_ref[...]
    log_sig = jnp.minimum(z, 0.0) - jnp.log(1.0 + jnp.exp(-jnp.abs(z)))
    lg_ref[...] = log_sig * (LOG2_E / GATE_NORM)


def _inproj(rows, t, modl, gain, w_in_all, layer, w2, bgk, *, tile, mod_row):
    d = rows[0].shape[1]
    n_main = w_in_all.shape[2] - 2 * GATE_RANK
    n_gate = w2.shape[1]
    mod_spec = lambda m: pl.BlockSpec((None, None, 1, d), lambda i: (mod_row(i), m, 0, 0))
    full = lambda a: pl.BlockSpec(a.shape, lambda i: (0,) * a.ndim)
    row_specs, row_args, n_ctx_tiles = _row_specs(rows, tile)
    return pl.pallas_call(
        functools.partial(_inproj_kernel, n_chunk=4, n_ctx_tiles=n_ctx_tiles),
        grid=(t // tile,),
        in_specs=row_specs + [
            mod_spec(1), mod_spec(0), full(gain),
            pl.BlockSpec((None,) + w_in_all.shape[1:], lambda i: (layer, 0, 0)), full(w2), full(bgk),
        ],
        out_specs=[
            pl.BlockSpec((tile, n_main), lambda i: (i, 0)),
            pl.BlockSpec((tile, n_gate), lambda i: (i, 0)),
        ],
        out_shape=[
            jax.ShapeDtypeStruct((t, n_main), BF16),
            jax.ShapeDtypeStruct((t, n_gate), F32),
        ],
        scratch_shapes=[pltpu.VMEM((d, n_main), BF16), pltpu.VMEM((d, 128), BF16)],
        compiler_params=_cparams(1),
        name="norm_inproj",
    )(*row_args, modl, modl, gain, w_in_all, w2, bgk)


def _store_heads(dst_ref, rows, col0, val):
    half = (col0 // GLA_DK) & 1
    plain = val.astype(BF16)
    turned = pltpu.roll(val, GLA_DK, axis=1).astype(BF16)
    for h in range(GLA_HEADS):
        if (h & 1) == half:
            src = plain[:, h * GLA_DK:(h + 1) * GLA_DK]
        else:
            hh = (h + 1) % GLA_HEADS
            src = turned[:, hh * GLA_DK:(hh + 1) * GLA_DK]
        dst_ref[h, rows, col0:col0 + GLA_DK] = src


def _gla_span(q_ref, k_ref, v_ref, lg_ref, o_ref, row0, st_ref, qh_ref, kh_ref, b_ref, qf_ref, kf_ref,
              tri_ref, dmask_ref, smask_ref, *, rev):
    rows = pl.ds(row0, SPAN)
    dcol = slice(rev * GLA_QK, (rev + 1) * GLA_QK)
    b_ref[...] = _dot_exact_lhs(tri_ref[rev], lg_ref[rows, dcol])
    qf_ref[...] = q_ref[rows, :].astype(F32) * (GLA_DK ** -0.5)
    kf_ref[...] = k_ref[rows, :].astype(F32)
    e_tot = b_ref[0:1, :] if rev else b_ref[SPAN - 1:SPAN, :]

    for s in range(N_SUB - 1):
        j = s + 1 if rev else s
        ref_row = j * SUB if rev else j * SUB + SUB - 1
        e_j = b_ref[ref_row:ref_row + 1, :]
        blk = slice(j * SUB, (j + 1) * SUB)
        qrows = slice(0, j * SUB) if rev else slice((j + 1) * SUB, SPAN)
        col0 = DIAG_W + s * GLA_DK
        _store_heads(qh_ref, qrows, col0, qf_ref[qrows, :] * jnp.exp2(b_ref[qrows, :] - e_j))
        _store_heads(kh_ref, blk, col0, kf_ref[blk, :] * jnp.exp2(e_j - b_ref[blk, :]))
    for i in range(N_SUB):
        blk = slice(i * SUB, (i + 1) * SUB)
        mid = i * SUB + SUB // 2
        e_m = b_ref[mid:mid + 1, :]
        _store_heads(qh_ref, blk, 0, qf_ref[blk, :] * jnp.exp2(jnp.minimum(b_ref[blk, :] - e_m, EXP2_CLAMP)))
        _store_heads(kh_ref, blk, 0, kf_ref[blk, :] * jnp.exp2(jnp.minimum(e_m - b_ref[blk, :], EXP2_CLAMP)))

    st = st_ref[...]
    qd = (qf_ref[...] * jnp.exp2(b_ref[...])).astype(BF16)
    o_state = _dot_nt(qd, st.astype(BF16))
    in_block = dmask_ref[rev] != 0.0
    for h in range(GLA_HEADS):
        vc = slice(h * GLA_DV, (h + 1) * GLA_DV)
        a_diag = _dot_nt(qh_ref[h, :, 0:DIAG_W], kh_ref[h, :, 0:DIAG_W])
        a_cross = _dot_nt(qh_ref[h, :, DIAG_W:], kh_ref[h, :, DIAG_W:])
        a = (a_cross + jnp.where(in_block, a_diag, 0.0)).astype(BF16)
        o_ref[rows, vc] += _dot(a, v_ref[rows, vc]) + o_state[:, vc]
    kd =(kf_ref[...] * jnp.exp2(e_tot - b_ref[...])).astype(BF16)
    upd = _dot_tn(v_ref[rows, :], kd)
    st_ref[...] = st * jnp.exp2(e_tot) + upd * smask_ref[...]


def _gla_finish(o_ref, g_ref, gain, out_ref, n_rows):
    def body(c, carry):
        rows = pl.ds(pl.multiple_of(c * SPAN, SPAN), SPAN)
        for h in range(GLA_HEADS):
            vc = slice(h * GLA_DV, (h + 1) * GLA_DV)
            o = o_ref[rows, vc]
            y = o * lax.rsqrt(jnp.mean(o * o, axis=-1, keepdims=True) + EPS) * gain
            g = g_ref[rows, vc].astype(F32)
            out_ref[rows, vc] = (y * _silu(g)).astype(BF16)
        return carry
    lax.fori_loop(0, n_rows // SPAN, body, 0)


def _gla_kernel(qc, kc, vc, gc, lgc, qx, kx, vx, gx, lgx, gain_ref, tri, dmask, smask, outc, outx,
                oc, ox, *per_direction):
    n_c = qc.shape[0] // SPAN
    n_x = qx.shape[0] // SPAN
    names = ("st_ref", "qh_ref", "kh_ref", "b_ref", "qf_ref", "kf_ref")
    scratch = [dict(zip(names, per_direction[r * len(names):(r + 1) * len(names)])) for r in (0, 1)]
    for ref in (oc, ox) + tuple(s[n] for s in scratch for n in ("st_ref", "qh_ref", "kh_ref")):
        ref[...] = jnp.zeros_like(ref)
    spans = [functools.partial(_gla_span, tri_ref=tri, dmask_ref=dmask, smask_ref=smask, rev=rev,
                               **scratch[rev]) for rev in (0, 1)]
    for s in range(n_c):
        spans[0](qc, kc, vc, lgc, oc, s * SPAN)
        spans[1](qc, kc, vc, lgc, oc, (n_c - 1 - s) * SPAN)

    def body(s, carry):
        spans[0](qx, kx, vx, lgx, ox, pl.multiple_of(s * SPAN, SPAN))
        spans[1](qx, kx, vx, lgx, ox, pl.multiple_of((n_x - 1 - s) * SPAN, SPAN))
        return carry
    lax.fori_loop(0, n_x, body, 0)
    gain = gain_ref[...]
    _gla_finish(oc, gc, gain, outc, qc.shape[0])
    _gla_finish(ox, gx, gain, outx, qx.shape[0])


def _gla(pbf, lg, gla_gain, *, batch, ctx_len, seq):
    nxb = (batch * ctx_len) // seq
    assert nxb * seq == batch * ctx_len
    hv = GLA_HEADS * GLA_DV
    cspec = lambda w, j: pl.BlockSpec((ctx_len, w), lambda b: (b, j))
    xspec = lambda w, j: pl.BlockSpec((seq, w), lambda b: (nxb + b, j))
    full = lambda a: pl.BlockSpec(a.shape, lambda b: (0,) * a.ndim)
    pos = np.arange(SPAN)
    lower = pos[None, :] <= pos[:, None]
    same = (pos[None, :] // SUB) == (pos[:, None] // SUB)
    tri = jnp.asarray(np.stack([lower, lower.T]).astype(np.float32), BF16)
    dmask = jnp.asarray(np.stack([lower & same, lower.T & same]).astype(np.float32))
    smask = jnp.asarray(((np.arange(hv)[:, None] // GLA_DV) == (np.arange(GLA_QK)[None, :] // GLA_DK))
                        .astype(np.float32))
    return pl.pallas_call(
        _gla_kernel,
        grid=(batch,),
        in_specs=[
            cspec(GLA_QK, 2), cspec(GLA_QK, 3), cspec(hv, 2), cspec(hv, 3), cspec(2 * GLA_QK, 0),
            xspec(GLA_QK, 2), xspec(GLA_QK, 3), xspec(hv, 2), xspec(hv, 3), xspec(2 * GLA_QK, 0),
            pl.BlockSpec((1, GLA_DV), lambda b: (0, 0)), full(tri), full(dmask), full(smask),
        ],
        out_specs=[
            pl.BlockSpec((ctx_len, hv), lambda b: (b, 0)),
            pl.BlockSpec((seq, hv), lambda b: (b, 0)),
        ],
        out_shape=[
            jax.ShapeDtypeStruct((batch * ctx_len, hv), BF16),
            jax.ShapeDtypeStruct((batch * seq, hv), BF16),
        ],
        scratch_shapes=[
            pltpu.VMEM((ctx_len, hv), F32),
            pltpu.VMEM((seq, hv), F32),
        ] + 2 * [
            pltpu.VMEM((hv, GLA_QK), F32),
            pltpu.VMEM((GLA_HEADS, SPAN, GLA_KW), BF16),
            pltpu.VMEM((GLA_HEADS, SPAN, GLA_KW), BF16),
            pltpu.VMEM((SPAN, GLA_QK), F32),
            pltpu.VMEM((SPAN, GLA_QK), F32),
            pltpu.VMEM((SPAN, GLA_QK), F32),
        ],
        compiler_params=_cparams(1),
        name="gla_bidirectional",
    )(pbf, pbf, pbf, pbf, lg, pbf, pbf, pbf, pbf, lg, gla_gain.reshape(1, GLA_DV), tri, dmask, smask)


def _window_offsets(w):
    return range(-(w // 2), w - w // 2)


def _band(w, period):
    ri = lax.broadcasted_iota(jnp.int32, (SPAN, SPAN), 0)
    ci = lax.broadcasted_iota(jnp.int32, (SPAN, SPAN), 1)
    d = ci - ri
    shift = period.bit_length() - 1
    assert period == 1 << shift
    same_row = (ri >> shift) == (ci >> shift)
    inside = jnp.where(d >= -(w // 2), jnp.where(d <= w - w // 2 - 1, 1.0, 0.0), 0.0)
    return jnp.where(same_row, inside, 0.0).astype(BF16)


def _pool_kernel(uc_ref, ux_ref, invc_ref, invx_ref, wp_ref, ps_ref, outc_ref, outx_ref, pad_ref,
                 *, pad_rows):
    n_x = ux_ref.shape[0]
    gw = wp_ref.shape[1]
    pad_ref[0:pad_rows, :] = jnp.zeros((pad_rows, gw), F32)
    pad_ref[pad_rows + n_x:, :] = jnp.zeros((pad_rows, gw), F32)
    for g, w in enumerate(POOL_WINDOWS):
        gc = slice(g * gw, (g + 1) * gw)
        wp = wp_ref[g]
        ps = ps_ref[:, gc]
        band_c = _band(w, SPAN)
        for t in range(uc_ref.shape[0] // SPAN):
            rows = slice(t * SPAN, (t + 1) * SPAN)
            u = uc_ref[rows, gc].astype(F32)
            pooled = _dot_exact_lhs(band_c, u, terms=2) * invc_ref[g, rows, :] - u
            outc_ref[rows, gc] = (_dot(pooled.astype(BF16), wp) * ps).astype(BF16)
        pad_ref[pad_rows:pad_rows + n_x, :] = ux_ref[:, gc].astype(F32)
        band_x = _band(w, GRID_W)

        def body(t, carry):
            r0 = pl.multiple_of(t * SPAN, SPAN)
            acc = jnp.zeros((SPAN, gw), F32)
            for dr in _window_offsets(w):
                acc = acc + pad_ref[pl.ds(pad_rows + r0 + dr * GRID_W, SPAN), :]
            rows = pl.ds(r0, SPAN)
            u = pad_ref[pl.ds(pad_rows + r0, SPAN), :]
            pooled = _dot_exact_lhs(band_x, acc, terms=2) * invx_ref[g, rows, :] - u
            outx_ref[rows, gc] = (_dot(pooled.astype(BF16), wp) * ps).astype(BF16)
            return carry
        lax.fori_loop(0, n_x // SPAN, body, 0, unroll=4)


def _inv_counts(length, rows_of):
    out = []
    for w in POOL_WINDOWS:
        cnt = np.ones((length,), np.float64)
        for axis_len, coord in rows_of(length):
            lo = np.clip(coord - w // 2, 0, axis_len)
            hi = np.clip(coord + w - w // 2, 0, axis_len)
            cnt = cnt * (hi - lo)
        out.append(1.0 / cnt)
    return np.broadcast_to(np.stack(out)[:, :, None], (len(POOL_WINDOWS), length, 128)).astype(np.float32)


def _pool(pbf, w_pool, pool_scale, *, batch, ctx_len, seq):
    nxb = (batch * ctx_len) // seq
    pw = w_pool.shape[0] * w_pool.shape[1]
    pad_rows = (max(POOL_WINDOWS) // 2) * GRID_W
    inv_c = jnp.asarray(_inv_counts(ctx_len, lambda n: [(n, np.arange(n))]))
    inv_x = jnp.asarray(_inv_counts(
        seq, lambda n: [(n // GRID_W, np.arange(n) // GRID_W), (GRID_W, np.arange(n) % GRID_W)]))
    full = lambda a: pl.BlockSpec(a.shape, lambda b: (0,) * a.ndim)
    return pl.pallas_call(
        functools.partial(_pool_kernel, pad_rows=pad_rows),
        grid=(batch,),
        in_specs=[
            pl.BlockSpec((ctx_len, pw), lambda b: (b, 0)),
            pl.BlockSpec((seq, pw), lambda b: (nxb + b, 0)),
            full(inv_c), full(inv_x), full(w_pool),
            pl.BlockSpec((1, pw), lambda b: (0, 0)),
        ],
        out_specs=[
            pl.BlockSpec((ctx_len, pw), lambda b: (b, 0)),
            pl.BlockSpec((seq, pw), lambda b: (b, 0)),
        ],
        out_shape=[
            jax.ShapeDtypeStruct((batch * ctx_len, pw), BF16),
            jax.ShapeDtypeStruct((batch * seq, pw), BF16),
        ],
        scratch_shapes=[pltpu.VMEM((seq + 2 * pad_rows, w_pool.shape[1]), F32)],
        compiler_params=_cparams(1),
        name="pool_mixer",
    )(pbf, pbf, inv_c, inv_x, w_pool, pool_scale.reshape(1, pw))


def _route(sb, s):
    def rank_in_group(vals):
        ranks = []
        for i, vi in enumerate(vals):
            r = jnp.zeros(vi.shape, jnp.int32)
            for j, vj in enumerate(vals):
                if j == i:
                    continue
                ahead = (vj >= vi) if j < i else (vj > vi)
                r = r + jnp.where(ahead, 1, 0)
            ranks.append(r)
        return ranks

    best = None
    for g in range(N_GROUPS):
        vals = sb[g * EXPERTS_PER_GROUP:(g + 1) * EXPERTS_PER_GROUP]
        svals = s[g * EXPERTS_PER_GROUP:(g + 1) * EXPERTS_PER_GROUP]
        score = None
        for i in range(EXPERTS_PER_GROUP):
            for j in range(i + 1, EXPERTS_PER_GROUP):
                pair = vals[i] + vals[j]
                score = pair if score is None else jnp.maximum(score, pair)
        ranks = rank_in_group(vals)
        e1 = jnp.zeros(score.shape, jnp.int32)
        e2 = jnp.zeros(score.shape, jnp.int32)
        w1 = jnp.zeros(score.shape, F32)
        w2 = jnp.zeros(score.shape, F32)
        for i in range(EXPERTS_PER_GROUP):
            e1 = jnp.where(ranks[i] == 0, g * EXPERTS_PER_GROUP + i, e1)
            e2 = jnp.where(ranks[i] == 1, g * EXPERTS_PER_GROUP + i, e2)
            w1 = jnp.where(ranks[i] == 0, svals[i], w1)
            w2 = jnp.where(ranks[i] == 1, svals[i], w2)
        cand = (score, e1, e2, w1, w2)
        if best is None:
            best = cand
        else:
            take = cand[0] > best[0]
            best = tuple(jnp.where(take, c, b) for c, b in zip(cand, best))
    _, e1, e2, w1, w2 = best
    tot = w1 + w2
    return e1, e2, w1 / tot, w2 / tot


def _outproj_kernel(xc_ref, xx_ref, pc_ref, px_ref, ac_ref, ax_ref, gate_ref, scale_ref, shift_ref, gain_ref,
                    wo_ref, wrt_ref, rb_ref, before_ref, below_ref, x1_ref, xs_ref, slot_ref, wt_ref, len_ref,
                    *, n_ctx_tiles, half):
    is_ctx = pl.program_id(0) < n_ctx_tiles
    pool = jnp.where(is_ctx, pc_ref[...], px_ref[...])
    att = jnp.where(is_ctx, ac_ref[...], ax_ref[...])
    y = _dot(pool, wo_ref[0:half, :]) + _dot(att, wo_ref[half:, :])
    x1 = jnp.where(is_ctx, xc_ref[...], xx_ref[...]) + gate_ref[...] * y
    x1_ref[...] = x1
    h2 = _norm_mod(x1, gain_ref[...], scale_ref[...], shift_ref[...])
    logits = _dot3_nt(wrt_ref[...], h2)
    s = jax.nn.sigmoid(logits)
    sb = s + rb_ref[...]
    rows = lambda a: [a[i:i + 1, :] for i in range(N_EXPERTS)]
    e1, e2, w1, w2 = _route(rows(sb), rows(s))
    wt_ref[0:1, :] = w1
    wt_ref[1:2, :] = w2
    ie = lax.broadcasted_iota(jnp.int32, logits.shape, 0)
    oh1 = jnp.where(ie == e1, 1.0, 0.0)
    oh2 = jnp.where(ie == e2, 1.0, 0.0)
    c1 = jnp.sum(oh1, axis=1, keepdims=True)
    c2 = jnp.sum(oh2, axis=1, keepdims=True)
    seg_len = jnp.ceil((c1 + c2) * (1.0 / SEG_ALIGN)) * SEG_ALIGN
    seg_len_l = jnp.broadcast_to(seg_len, len_ref.shape)
    seg_off = _dot(below_ref[...], seg_len_l.astype(BF16))[:, 0:1]
    before1 = _dot(oh1.astype(BF16), before_ref[...]) + seg_off
    before2 = _dot(oh2.astype(BF16), before_ref[...]) + (seg_off + c1)
    slot1 = jnp.sum(oh1 * before1, axis=0, keepdims=True).astype(jnp.int32)
    slot2 = jnp.sum(oh2 * before2, axis=0, keepdims=True).astype(jnp.int32)
    slot_ref[0:1, :] = slot1
    slot_ref[1:2, :] = slot2
    len_ref[...] = seg_len_l.astype(jnp.int32)
    si = lax.broadcasted_iota(jnp.int32, (xs_ref.shape[0], slot1.shape[1]), 0)
    perm = jnp.where(si == slot1, 1.0, 0.0) + jnp.where(si == slot2, 1.0, 0.0)
    xs_ref[...] = _dot(perm.astype(BF16), h2.astype(BF16)).astype(BF16)


def _outproj(rows, pool_c, pool_x, att_c, att_x, modl, gain2, w_out, w_router_t, router_bias,
             *, tile, mod_row, first_tile, n_tiles):
    d = rows[0].shape[1]
    row_specs, row_args, _ = _row_specs(rows, tile, first_tile)
    half = pool_c.shape[1]
    n_ctx_tiles = pool_c.shape[0] // tile
    cidx = lambda i: (jnp.minimum(i + first_tile, n_ctx_tiles - 1), 0)
    xidx = lambda i: (jnp.maximum(i + first_tile - n_ctx_tiles, 0), 0)
    mod_spec = lambda m: pl.BlockSpec((None, None, 1, d), lambda i: (mod_row(i + first_tile), m, 0, 0))
    full = lambda a: pl.BlockSpec(a.shape, lambda i: (0,) * a.ndim)
    tok = lambda w: pl.BlockSpec((tile, w), lambda i: (i, 0))
    lane = pl.BlockSpec((2, tile), lambda i: (0, i))
    t_out = n_tiles * tile
    before = jnp.asarray(np.triu(np.ones((tile, tile), np.float32), 1), BF16)
    below = jnp.asarray(np.tril(np.ones((N_EXPERTS, N_EXPERTS), np.float32), -1), BF16)
    xs_rows = _xs_rows(tile)
    return pl.pallas_call(
        functools.partial(_outproj_kernel, n_ctx_tiles=n_ctx_tiles - first_tile, half=half),
        grid=(n_tiles,),
        in_specs=row_specs + [
            pl.BlockSpec((tile, half), cidx), pl.BlockSpec((tile, half), xidx),
            pl.BlockSpec((tile, half), cidx), pl.BlockSpec((tile, half), xidx),
            mod_spec(2), mod_spec(4), mod_spec(3), full(gain2), full(w_out), full(w_router_t),
            full(router_bias), full(before), full(below),
        ],
        out_specs=[tok(d), pl.BlockSpec((xs_rows, d), lambda i: (i, 0)), lane, lane,
                   pl.BlockSpec((None, N_EXPERTS, 128), lambda i: (i, 0, 0))],
        out_shape=[
            jax.ShapeDtypeStruct((t_out, d), F32),
            jax.ShapeDtypeStruct((n_tiles * xs_rows, d), BF16),
            jax.ShapeDtypeStruct((2, t_out), jnp.int32),
            jax.ShapeDtypeStruct((2, t_out), F32),
            jax.ShapeDtypeStruct((n_tiles, N_EXPERTS, 128), jnp.int32),
        ],
        compiler_params=_cparams(1),
        name="outproj_router",
    )(*row_args, pool_c, pool_x, att_c, att_x, modl, modl, modl, gain2, w_out, w_router_t, router_bias,
      before, below)


def _xs_rows(tile):
    return 2 * tile + N_EXPERTS * SEG_ALIGN


def _for_each_unit(n, fn):
    def body(u, carry):
        fn(u * SEG_ALIGN)
        return carry
    lax.fori_loop(0, n // SEG_ALIGN, body, 0)


def _expert_kernel(be_ref, nu_ref, usrc_ref, xs_ref, w1_ref, w3_ref, w2_ref, yb_ref,
                   xbuf, wb1, wb3, wb2, sem):
    i = pl.program_id(0)
    n_used = nu_ref[0]
    units = MOE_BM // SEG_ALIGN

    def fetch(blk, half, action):
        def body(u, carry):
            src = pl.multiple_of(usrc_ref[blk * units + u], SEG_ALIGN)
            dst = pl.multiple_of(u * SEG_ALIGN, SEG_ALIGN)
            action(pltpu.make_async_copy(xs_ref.at[pl.ds(src, SEG_ALIGN), :],
                                         xbuf.at[half, pl.ds(dst, SEG_ALIGN), :], sem.at[half]))
            return carry
        lax.fori_loop(0, units, body, 0, unroll=8)

    half = i % 2
    pl.when(i == 0)(lambda: fetch(i, 0, lambda c: c.start()))
    pl.when(i + 1 < n_used)(lambda: fetch(i + 1, 1 - half, lambda c: c.start()))
    used = i < n_used
    new_expert = (i == 0) | (be_ref[i] != be_ref[jnp.maximum(i - 1, 0)])

    @pl.when(used & new_expert)
    def _():
        for src, dst in ((w1_ref, wb1), (w3_ref, wb3), (w2_ref, wb2)):
            step = src.shape[0] // 4
            for r in range(0, src.shape[0], step):
                dst[r:r + step, :] = src[r:r + step, :].astype(BF16)

    @pl.when(used)
    def _():
        fetch(i, half, lambda c: c.wait())
        for r in range(0, MOE_BM, EXPERT_ROWS):
            x = xbuf[half, r:r + EXPERT_ROWS, :]
            a = _dot(x, wb1[...])
            b = _dot(x, wb3[...])
            yb_ref[r:r + EXPERT_ROWS, :] = _dot((_silu(a) * b).astype(BF16), wb2[...]).astype(BF16)

    @pl.when(jnp.logical_not(used))
    def _():
        yb_ref[...] = jnp.zeros_like(yb_ref)


def _experts(xs, unit_src, block_e, n_used, w1, w3, w2, layer, n_slots):
    d = xs.shape[1]
    de = w1.shape[3]
    nb = n_slots // MOE_BM
    wspec = lambda a: pl.BlockSpec((None, None) + a.shape[2:], lambda i, be, nu, us: (layer, be[i], 0, 0))
    return pl.pallas_call(
        _expert_kernel,
        grid_spec=pltpu.PrefetchScalarGridSpec(
            num_scalar_prefetch=3,
            grid=(nb,),
            in_specs=[pl.BlockSpec(memory_space=pl.ANY), wspec(w1), wspec(w3), wspec(w2)],
            out_specs=pl.BlockSpec((MOE_BM, d), lambda i, be, nu, us: (i, 0)),
            scratch_shapes=[pltpu.VMEM((2, MOE_BM, d), xs.dtype), pltpu.VMEM((d, de), BF16),
                            pltpu.VMEM((d, de), BF16), pltpu.VMEM((de, d), BF16),
                            pltpu.SemaphoreType.DMA((2,))],
        ),
        out_shape=jax.ShapeDtypeStruct((n_slots, d), BF16),
        compiler_params=_cparams(1),
        name="moe_experts",
    )(block_e, n_used, unit_src.astype(jnp.int32), xs, w1, w3, w2)


def _combine_kernel(seg_off, src_row, seg_len, x_ref, slot_ref, wt_ref, gate_ref, gainf_ref, yb_ref, o_ref,
                    buf, sem, *, final_norm):
    i = pl.program_id(0)
    n = pl.num_programs(0)

    def fetch(tile_idx, half, action):
        for e in range(N_EXPERTS):
            j = tile_idx * N_EXPERTS + e

            def unit(offset, j=j):
                s0 = pl.multiple_of(src_row[j] + offset, SEG_ALIGN)
                d0 = pl.multiple_of(seg_off[j] + offset, SEG_ALIGN)
                action(pltpu.make_async_copy(yb_ref.at[pl.ds(s0, SEG_ALIGN), :],
                                             buf.at[half, pl.ds(d0, SEG_ALIGN), :], sem.at[half]))
            _for_each_unit(seg_len[j], unit)

    @pl.when(i == 0)
    def _():
        buf[...] = jnp.zeros_like(buf)
        fetch(i, 0, lambda c: c.start())

    half = i % 2
    pl.when(i + 1 < n)(lambda: fetch(i + 1, 1 - half, lambda c: c.start()))
    fetch(i, half, lambda c: c.wait())
    slot = slot_ref[...]
    wt = wt_ref[...]
    li = lax.broadcasted_iota(jnp.int32, (slot.shape[0], buf.shape[1]), 1)
    mix = jnp.where(li == slot[:, 0:1], wt[:, 0:1], 0.0) + jnp.where(li == slot[:, 1:2], wt[:, 1:2], 0.0)
    y = _dot(mix.astype(BF16), buf[half])
    x2 = x_ref[...] + gate_ref[...] * y
    if final_norm:
        x2 = x2 * lax.rsqrt(jnp.mean(x2 * x2, axis=-1, keepdims=True) + EPS) * gainf_ref[...]
    o_ref[...] = x2


def _combine(x1, yb, seg_off, src_row, seg_len, slot_cols, wt_cols, modl, gain_f, *, tile, mod_row,
             first_tile, final_norm):
    t, d = x1.shape
    flat = lambda a: a.reshape(-1).astype(jnp.int32)
    return pl.pallas_call(
        functools.partial(_combine_kernel, final_norm=final_norm),
        grid_spec=pltpu.PrefetchScalarGridSpec(
            num_scalar_prefetch=3,
            grid=(t // tile,),
            in_specs=[
                pl.BlockSpec((tile, d), lambda i, *_: (i, 0)),
                pl.BlockSpec((tile, 2), lambda i, *_: (i, 0)),
                pl.BlockSpec((tile, 2), lambda i, *_: (i, 0)),
                pl.BlockSpec((None, None, 1, d), lambda i, *_: (mod_row(i + first_tile), 5, 0, 0)),
                pl.BlockSpec((1, d), lambda i, *_: (0, 0)),
                pl.BlockSpec(memory_space=pl.ANY),
            ],
            out_specs=pl.BlockSpec((tile, d), lambda i, *_: (i, 0)),
            scratch_shapes=[pltpu.VMEM((2, _xs_rows(tile), d), yb.dtype), pltpu.SemaphoreType.DMA((2,))],
        ),
        out_shape=jax.ShapeDtypeStruct((t, d), F32),
        compiler_params=_cparams(1),
        name="moe_combine",
    )(flat(seg_off), flat(src_row), flat(seg_len), x1, slot_cols, wt_cols, modl, gain_f, yb)


def _slot_plan(seg_len, *, tile, n_slots):
    n_tiles = seg_len.shape[0]
    xs_rows = _xs_rows(tile)
    nb = n_slots // MOE_BM
    seg_off = jnp.cumsum(seg_len, axis=1) - seg_len
    rows_e = jnp.sum(seg_len, axis=0)
    padded = (rows_e + MOE_BM - 1) // MOE_BM * MOE_BM
    pend = jnp.cumsum(padded)
    pstart = pend - padded
    xb_row = pstart[None, :] + jnp.cumsum(seg_len, axis=0) - seg_len
    n_used = pend[-1] // MOE_BM
    blk = jnp.minimum(jnp.arange(nb, dtype=jnp.int32), n_used - 1)
    block_e = jnp.sum((pend[None, :] <= (blk * MOE_BM)[:, None]).astype(jnp.int32), axis=1)
    block_e = jnp.minimum(block_e, N_EXPERTS - 1).astype(jnp.int32)
    starts = xb_row.T.reshape(1, -1)
    lens = seg_len.T.reshape(1, -1)
    srcs = (jnp.arange(n_tiles, dtype=jnp.int32)[:, None] * xs_rows + seg_off).T.reshape(1, -1)
    unit_row = (jnp.arange(n_slots // SEG_ALIGN, dtype=jnp.int32) * SEG_ALIGN)[:, None]
    inside = (unit_row >= starts) & (unit_row < starts + lens)
    unit_src = jnp.sum(jnp.where(inside, srcs + unit_row - starts, 0), axis=1)
    unit_src = jnp.where(jnp.any(inside, axis=1), unit_src, xs_rows - SEG_ALIGN)
    return seg_off, xb_row, unit_src, block_e, n_used.astype(jnp.int32).reshape(1)


def kernel(x, c, ctx, c_ctx, w_mod, b_mod, norm1, norm2, w_in, w_gk2, b_gk, w_pool, pool_scale, gla_gain,
           w_out, w_router, router_bias, w1, w3, w2, norm_f):
    batch, seq, d = x.shape
    ctx_len = ctx.shape[1]
    depth = w_mod.shape[0]
    n_ctx = batch * ctx_len
    t_all = n_ctx + batch * seq
    def row_tile(limit):
        return max(t for t in (1024, 512, 256) if t <= limit and n_ctx % t == 0 and seq % t == 0)

    tile_a = row_tile(1024)
    tile_b = row_tile(512)

    def mod_row_for(tile):
        nct, per_b = n_ctx // tile, seq // tile
        return lambda i: jnp.where(i < nct, 0, 1 + (i - nct) // per_b)

    rows = (ctx.reshape(n_ctx, d), x.reshape(batch * seq, d), 0)

    cvecs = jnp.zeros((16, d), F32).at[0].set(c_ctx).at[1:1 + batch].set(c)
    mods = _modulation(cvecs, w_mod, b_mod).reshape(depth, 16, N_MOD, 1, d)

    w2g = jnp.zeros((depth, 128, 2 * GLA_QK), F32)
    w2g = w2g.at[:, 0:GATE_RANK, 0:GLA_QK].set(w_gk2[:, 0])
    w2g = w2g.at[:, GATE_RANK:2 * GATE_RANK, GLA_QK:].set(w_gk2[:, 1])
    bgk = b_gk.reshape(depth, 1, 2 * GLA_QK)
    w_out_b = w_out.astype(BF16)
    w_pool_b = w_pool.astype(BF16)
    w_router_t = w_router.T
    rbias = router_bias.reshape(N_EXPERTS, 1)
    g1 = lambda a: a.reshape(1, -1)

    out = None
    for l in range(depth):
        last = l == depth - 1
        pbf, lg = _inproj(rows, t_all, mods[l], g1(norm1[l]), w_in, l, w2g[l], bgk[l],
                          tile=tile_a, mod_row=mod_row_for(tile_a))
        att_c, att_x = _gla(pbf, lg, gla_gain[l], batch=batch, ctx_len=ctx_len, seq=seq)
        pool_c, pool_x = _pool(pbf, w_pool_b[l], pool_scale[l], batch=batch, ctx_len=ctx_len, seq=seq)
        first_tile = n_ctx // tile_b if last else 0
        n_tiles = t_all // tile_b - first_tile
        x1, xs, slots, wts, seg_len = _outproj(
            rows, pool_c, pool_x, att_c, att_x, mods[l], g1(norm2[l]), w_out_b[l], w_router_t, rbias,
            tile=tile_b, mod_row=mod_row_for(tile_b), first_tile=first_tile, n_tiles=n_tiles)
        seg_len = seg_len[:, :, 0]
        max_rows = n_tiles * (2 * tile_b + N_EXPERTS * (SEG_ALIGN - 1)) + N_EXPERTS * (MOE_BM - 1)
        n_slots = -(-max_rows // MOE_BM) * MOE_BM
        seg_off, xb_row, unit_src, block_e, n_used = _slot_plan(seg_len, tile=tile_b, n_slots=n_slots)
        yb = _experts(xs, unit_src, block_e, n_used, w1, w3, w2, l, n_slots)
        res = _combine(x1, yb, seg_off, xb_row, seg_len, slots.T, wts.T, mods[l], g1(norm_f), tile=tile_b,
                       mod_row=mod_row_for(tile_b), first_tile=first_tile, final_norm=last)
        if last:
            out = res
        else:
            rows = (res, res, n_ctx)
    return out.reshape(batch, seq, d)
```

```python
import functools

import numpy as np
import jax
import jax.numpy as jnp
from jax import lax
from jax.experimental import pallas as pl
from jax.experimental.pallas import tpu as pltpu

F32 = jnp.float32
BF16 = jnp.bfloat16

GRID_W = 64
POOL_GROUPS = 4
POOL_WINDOWS = (2, 4, 8, 16)
GLA_HEADS = 4
GLA_DK = 64
GLA_DV = 128
GLA_QK = GLA_HEADS * GLA_DK
GATE_RANK = 16
GATE_NORM = 16.0
N_EXPERTS = 16
N_GROUPS = 4
EXPERTS_PER_GROUP = 4
EPS = 1e-6
N_MOD = 6

VMEM_LIMIT_BYTES = 56 * 1024 * 1024
SPAN = 256
SUB = 32
N_SUB = SPAN // SUB
LOG2_E = 1.4426950408889634
EXP2_CLAMP = 115.0
DIAG_W = 128
GLA_KW = DIAG_W + N_SUB * GLA_DK
MOE_BM = 512
EXPERT_ROWS = 256
SEG_ALIGN = 16


def _cparams(n_axes):
    return pltpu.CompilerParams(
        dimension_semantics=("arbitrary",) * n_axes, vmem_limit_bytes=VMEM_LIMIT_BYTES)


def _split(a):
    hi = a.astype(BF16)
    lo = (a - hi.astype(F32)).astype(BF16)
    return hi, lo


def _dot(a, b):
    return jnp.dot(a, b, preferred_element_type=F32)


def _dot_nt(a, b):
    return lax.dot_general(a, b, (((1,), (1,)), ((), ())), preferred_element_type=F32)


def _dot_tn(a, b):
    return lax.dot_general(a, b, (((0,), (0,)), ((), ())), preferred_element_type=F32)


def _dot3(a, b):
    ah, al = _split(a)
    bh, bl = _split(b)
    return _dot(ah, bh) + (_dot(ah, bl) + _dot(al, bh))


def _dot3_nt(a, b):
    ah, al = _split(a)
    bh, bl = _split(b)
    return _dot_nt(ah, bh) + (_dot_nt(ah, bl) + _dot_nt(al, bh))


def _dot_exact_lhs(m_bf16, b, terms=3):
    out = None
    rest = b
    for _ in range(terms):
        part = rest.astype(BF16)
        rest = rest - part.astype(F32)
        prod = _dot(m_bf16, part)
        out = prod if out is None else out + prod
    return out


def _silu(x):
    return x * jax.nn.sigmoid(x)


def _mod_kernel(c_ref, w_ref, b_ref, o_ref):
    o_ref[0] = _dot3(_silu(c_ref[...]), w_ref[0]) + b_ref[0]


def _modulation(cvecs, w_mod, b_mod):
    depth, d, _ = w_mod.shape
    rows = cvecs.shape[0]
    return pl.pallas_call(
        _mod_kernel,
        grid=(depth, N_MOD),
        in_specs=[
            pl.BlockSpec((rows, d), lambda l, j: (0, 0)),
            pl.BlockSpec((1, d, d), lambda l, j: (l, 0, j)),
            pl.BlockSpec((1, 1, d), lambda l, j: (l, 0, j)),
        ],
        out_specs=pl.BlockSpec((1, rows, d), lambda l, j: (l, 0, j)),
        out_shape=jax.ShapeDtypeStruct((depth, rows, N_MOD * d), F32),
        compiler_params=_cparams(2),
        name="adaln_vectors",
    )(cvecs, w_mod, b_mod.reshape(depth, 1, N_MOD * d))


def _norm_mod(x, gain, scale, shift):
    y = x * lax.rsqrt(jnp.mean(x * x, axis=-1, keepdims=True) + EPS)
    return (y * gain) * (1.0 + scale) + shift


def _row_specs(rows, tile, first_tile=0):
    xc, xx, first_latent_row = rows
    d = xc.shape[1]
    x_off = first_latent_row // tile
    nct = (xc.shape[0] // tile) if first_latent_row == 0 else x_off
    cspec = pl.BlockSpec((tile, d), lambda i, *_: (jnp.minimum(i + first_tile, nct - 1), 0))
    xspec = pl.BlockSpec((tile, d), lambda i, *_: (jnp.maximum(i + first_tile - nct, 0) + x_off, 0))
    return [cspec, xspec], [xc, xx], nct - first_tile


def _row_groups(n_rows, groups=2):
    size = n_rows // groups
    return [slice(g * size, (g + 1) * size) for g in range(groups)]


def _inproj_kernel(xc_ref, xx_ref, scale_ref, shift_ref, gain_ref, win_ref, w2_ref, bgk_ref,
                   p_ref, lg_ref, w_ref, wr_ref, *, n_chunk, n_ctx_tiles):
    n_main = p_ref.shape[1]
    width = n_main // n_chunk

    @pl.when(pl.program_id(0) == 0)
    def _():
        for n in range(n_chunk):
            cols = slice(n * width, (n + 1) * width)
            w_ref[:, cols] = win_ref[:, cols].astype(BF16)
        wr_ref[...] = jnp.zeros_like(wr_ref)
        wr_ref[:, 0:2 * GATE_RANK] = win_ref[:, n_main:].astype(BF16)

    is_ctx = pl.program_id(0) < n_ctx_tiles
    for rows in _row_groups(p_ref.shape[0]):
        x = jnp.where(is_ctx, xc_ref[rows, :], xx_ref[rows, :])
        h = _norm_mod(x, gain_ref[...], scale_ref[...], shift_ref[...])
        hb = h.astype(BF16)
        for n in range(n_chunk):
            cols = slice(n * width, (n + 1) * width)
            p_ref[rows, cols] = _dot(hb, w_ref[:, cols]).astype(BF16)
        r = _dot(hb, wr_ref[...])
        z = _dot3(r, w2_ref[...]) + bgk_ref[...]
        log_sig = jnp.minimum(z, 0.0) - jnp.log(1.0 + jnp.exp(-jnp.abs(z)))
        lg_ref[rows, :] = log_sig * (LOG2_E / GATE_NORM)


def _inproj(rows, t, modl, gain, w_in_all, layer, w2, bgk, *, tile, mod_row):
    d = rows[0].shape[1]
    n_main = w_in_all.shape[2] - 2 * GATE_RANK
    n_gate = w2.shape[1]
    mod_spec = lambda m: pl.BlockSpec((None, None, 1, d), lambda i: (mod_row(i), m, 0, 0))
    full = lambda a: pl.BlockSpec(a.shape, lambda i: (0,) * a.ndim)
    row_specs, row_args, n_ctx_tiles = _row_specs(rows, tile)
    return pl.pallas_call(
        functools.partial(_inproj_kernel, n_chunk=4, n_ctx_tiles=n_ctx_tiles),
        grid=(t // tile,),
        in_specs=row_specs + [
            mod_spec(1), mod_spec(0), full(gain),
            pl.BlockSpec((None,) + w_in_all.shape[1:], lambda i: (layer, 0, 0)), full(w2), full(bgk),
        ],
        out_specs=[
            pl.BlockSpec((tile, n_main), lambda i: (i, 0)),
            pl.BlockSpec((tile, n_gate), lambda i: (i, 0)),
        ],
        out_shape=[
            jax.ShapeDtypeStruct((t, n_main), BF16),
            jax.ShapeDtypeStruct((t, n_gate), F32),
        ],
        scratch_shapes=[pltpu.VMEM((d, n_main), BF16), pltpu.VMEM((d, 128), BF16)],
        compiler_params=_cparams(1),
        name="norm_inproj",
    )(*row_args, modl, modl, gain, w_in_all, w2, bgk)


def _store_heads(dst_ref, rows, col0, val):
    half = (col0 // GLA_DK) & 1
    plain = val.astype(BF16)
    turned = pltpu.roll(val, GLA_DK, axis=1).astype(BF16)
    for h in range(GLA_HEADS):
        if (h & 1) == half:
            src = plain[:, h * GLA_DK:(h + 1) * GLA_DK]
        else:
            hh = (h + 1) % GLA_HEADS
            src = turned[:, hh * GLA_DK:(hh + 1) * GLA_DK]
        dst_ref[h, rows, col0:col0 + GLA_DK] = src


def _gla_span(q_ref, k_ref, v_ref, lg_ref, o_ref, row0, st_ref, qh_ref, kh_ref, b_ref, qf_ref, kf_ref,
              tri_ref, dmask_ref, smask_ref, *, rev):
    rows = pl.ds(row0, SPAN)
    dcol = slice(rev * GLA_QK, (rev + 1) * GLA_QK)
    b_ref[...] = _dot_exact_lhs(tri_ref[rev], lg_ref[rows, dcol])
    qf_ref[...] = q_ref[rows, :].astype(F32) * (GLA_DK ** -0.5)
    kf_ref[...] = k_ref[rows, :].astype(F32)
    e_tot = b_ref[0:1, :] if rev else b_ref[SPAN - 1:SPAN, :]

    for s in range(N_SUB - 1):
        j = s + 1 if rev else s
        ref_row = j * SUB if rev else j * SUB + SUB - 1
        e_j = b_ref[ref_row:ref_row + 1, :]
        blk = slice(j * SUB, (j + 1) * SUB)
        qrows = slice(0, j * SUB) if rev else slice((j + 1) * SUB, SPAN)
        col0 = DIAG_W + s * GLA_DK
        _store_heads(qh_ref, qrows, col0, qf_ref[qrows, :] * jnp.exp2(b_ref[qrows, :] - e_j))
        _store_heads(kh_ref, blk, col0, kf_ref[blk, :] * jnp.exp2(e_j - b_ref[blk, :]))
    for i in range(N_SUB):
        blk = slice(i * SUB, (i + 1) * SUB)
        mid = i * SUB + SUB // 2
        e_m = b_ref[mid:mid + 1, :]
        _store_heads(qh_ref, blk, 0, qf_ref[blk, :] * jnp.exp2(jnp.minimum(b_ref[blk, :] - e_m, EXP2_CLAMP)))
        _store_heads(kh_ref, blk, 0, kf_ref[blk, :] * jnp.exp2(jnp.minimum(e_m - b_ref[blk, :], EXP2_CLAMP)))

    st = st_ref[...]
    qd = (qf_ref[...] * jnp.exp2(b_ref[...])).astype(BF16)
    o_state = _dot_nt(qd, st.astype(BF16))
    in_block = dmask_ref[rev] != 0.0
    for h in range(GLA_HEADS):
        vc = slice(h * GLA_DV, (h + 1) * GLA_DV)
        a_diag = _dot_nt(qh_ref[h, :, 0:DIAG_W], kh_ref[h, :, 0:DIAG_W])
        a_cross = _dot_nt(qh_ref[h, :, DIAG_W:], kh_ref[h, :, DIAG_W:])
        a = (a_cross + jnp.where(in_block, a_diag, 0.0)).astype(BF16)
        o_ref[rows, vc] += _dot(a, v_ref[rows, vc]) + o_state[:, vc]
    kd =(kf_ref[...] * jnp.exp2(e_tot - b_ref[...])).astype(BF16)
    upd = _dot_tn(v_ref[rows, :], kd)
    st_ref[...] = st * jnp.exp2(e_tot) + upd * smask_ref[...]


def _gla_finish(o_ref, g_ref, gain, out_ref, n_rows):
    def body(c, carry):
        rows = pl.ds(pl.multiple_of(c * SPAN, SPAN), SPAN)
        for h in range(GLA_HEADS):
            vc = slice(h * GLA_DV, (h + 1) * GLA_DV)
            o = o_ref[rows, vc]
            y = o * lax.rsqrt(jnp.mean(o * o, axis=-1, keepdims=True) + EPS) * gain
            g = g_ref[rows, vc].astype(F32)
            out_ref[rows, vc] = (y * _silu(g)).astype(BF16)
        return carry
    lax.fori_loop(0, n_rows // SPAN, body, 0)


def _gla_kernel(qc, kc, vc, gc, lgc, qx, kx, vx, gx, lgx, gain_ref, tri, dmask, smask, outc, outx,
                oc, ox, *per_direction):
    n_c = qc.shape[0] // SPAN
    n_x = qx.shape[0] // SPAN
    names = ("st_ref", "qh_ref", "kh_ref", "b_ref", "qf_ref", "kf_ref")
    scratch = [dict(zip(names, per_direction[r * len(names):(r + 1) * len(names)])) for r in (0, 1)]
    for ref in (oc, ox) + tuple(s[n] for s in scratch for n in ("st_ref", "qh_ref", "kh_ref")):
        ref[...] = jnp.zeros_like(ref)
    spans = [functools.partial(_gla_span, tri_ref=tri, dmask_ref=dmask, smask_ref=smask, rev=rev,
                               **scratch[rev]) for rev in (0, 1)]
    for s in range(n_c):
        spans[0](qc, kc, vc, lgc, oc, s * SPAN)
        spans[1](qc, kc, vc, lgc, oc, (n_c - 1 - s) * SPAN)

    def body(s, carry):
        spans[0](qx, kx, vx, lgx, ox, pl.multiple_of(s * SPAN, SPAN))
        spans[1](qx, kx, vx, lgx, ox, pl.multiple_of((n_x - 1 - s) * SPAN, SPAN))
        return carry
    lax.fori_loop(0, n_x, body, 0)
    gain = gain_ref[...]
    _gla_finish(oc, gc, gain, outc, qc.shape[0])
    _gla_finish(ox, gx, gain, outx, qx.shape[0])


def _gla(pbf, lg, gla_gain, *, batch, ctx_len, seq):
    nxb = (batch * ctx_len) // seq
    assert nxb * seq == batch * ctx_len
    hv = GLA_HEADS * GLA_DV
    cspec = lambda w, j: pl.BlockSpec((ctx_len, w), lambda b: (b, j))
    xspec = lambda w, j: pl.BlockSpec((seq, w), lambda b: (nxb + b, j))
    full = lambda a: pl.BlockSpec(a.shape, lambda b: (0,) * a.ndim)
    pos = np.arange(SPAN)
    lower = pos[None, :] <= pos[:, None]
    same = (pos[None, :] // SUB) == (pos[:, None] // SUB)
    tri = jnp.asarray(np.stack([lower, lower.T]).astype(np.float32), BF16)
    dmask = jnp.asarray(np.stack([lower & same, lower.T & same]).astype(np.float32))
    smask = jnp.asarray(((np.arange(hv)[:, None] // GLA_DV) == (np.arange(GLA_QK)[None, :] // GLA_DK))
                        .astype(np.float32))
    return pl.pallas_call(
        _gla_kernel,
        grid=(batch,),
        in_specs=[
            cspec(GLA_QK, 2), cspec(GLA_QK, 3), cspec(hv, 2), cspec(hv, 3), cspec(2 * GLA_QK, 0),
            xspec(GLA_QK, 2), xspec(GLA_QK, 3), xspec(hv, 2), xspec(hv, 3), xspec(2 * GLA_QK, 0),
            pl.BlockSpec((1, GLA_DV), lambda b: (0, 0)), full(tri), full(dmask), full(smask),
        ],
        out_specs=[
            pl.BlockSpec((ctx_len, hv), lambda b: (b, 0)),
            pl.BlockSpec((seq, hv), lambda b: (b, 0)),
        ],
        out_shape=[
            jax.ShapeDtypeStruct((batch * ctx_len, hv), BF16),
            jax.ShapeDtypeStruct((batch * seq, hv), BF16),
        ],
        scratch_shapes=[
            pltpu.VMEM((ctx_len, hv), F32),
            pltpu.VMEM((seq, hv), F32),
        ] + 2 * [
            pltpu.VMEM((hv, GLA_QK), F32),
            pltpu.VMEM((GLA_HEADS, SPAN, GLA_KW), BF16),
            pltpu.VMEM((GLA_HEADS, SPAN, GLA_KW), BF16),
            pltpu.VMEM((SPAN, GLA_QK), F32),
            pltpu.VMEM((SPAN, GLA_QK), F32),
            pltpu.VMEM((SPAN, GLA_QK), F32),
        ],
        compiler_params=_cparams(1),
        name="gla_bidirectional",
    )(pbf, pbf, pbf, pbf, lg, pbf, pbf, pbf, pbf, lg, gla_gain.reshape(1, GLA_DV), tri, dmask, smask)


def _window_offsets(w):
    return range(-(w // 2), w - w // 2)


def _band(w, period):
    ri = lax.broadcasted_iota(jnp.int32, (SPAN, SPAN), 0)
    ci = lax.broadcasted_iota(jnp.int32, (SPAN, SPAN), 1)
    d = ci - ri
    shift = period.bit_length() - 1
    assert period == 1 << shift
    same_row = (ri >> shift) == (ci >> shift)
    inside = jnp.where(d >= -(w // 2), jnp.where(d <= w - w // 2 - 1, 1.0, 0.0), 0.0)
    return jnp.where(same_row, inside, 0.0).astype(BF16)


def _pool_kernel(uc_ref, ux_ref, invc_ref, invx_ref, wp_ref, ps_ref, outc_ref, outx_ref, pad_ref,
                 *, pad_rows):
    n_x = ux_ref.shape[0]
    gw = wp_ref.shape[1]
    pad_ref[0:pad_rows, :] = jnp.zeros((pad_rows, gw), F32)
    pad_ref[pad_rows + n_x:, :] = jnp.zeros((pad_rows, gw), F32)
    for g, w in enumerate(POOL_WINDOWS):
        gc = slice(g * gw, (g + 1) * gw)
        wp = wp_ref[g]
        ps = ps_ref[:, gc]
        band_c = _band(w, SPAN)
        for t in range(uc_ref.shape[0] // SPAN):
            rows = slice(t * SPAN, (t + 1) * SPAN)
            u = uc_ref[rows, gc].astype(F32)
            pooled = _dot_exact_lhs(band_c, u, terms=2) * invc_ref[g, rows, :] - u
            outc_ref[rows, gc] = (_dot(pooled.astype(BF16), wp) * ps).astype(BF16)
        pad_ref[pad_rows:pad_rows + n_x, :] = ux_ref[:, gc].astype(F32)
        band_x = _band(w, GRID_W)

        def body(t, carry):
            r0 = pl.multiple_of(t * SPAN, SPAN)
            acc = jnp.zeros((SPAN, gw), F32)
            for dr in _window_offsets(w):
                acc = acc + pad_ref[pl.ds(pad_rows + r0 + dr * GRID_W, SPAN), :]
            rows = pl.ds(r0, SPAN)
            u = pad_ref[pl.ds(pad_rows + r0, SPAN), :]
            pooled = _dot_exact_lhs(band_x, acc, terms=2) * invx_ref[g, rows, :] - u
            outx_ref[rows, gc] = (_dot(pooled.astype(BF16), wp) * ps).astype(BF16)
            return carry
        lax.fori_loop(0, n_x // SPAN, body, 0, unroll=8)


def _inv_counts(length, rows_of):
    out = []
    for w in POOL_WINDOWS:
        cnt = np.ones((length,), np.float64)
        for axis_len, coord in rows_of(length):
            lo = np.clip(coord - w // 2, 0, axis_len)
            hi = np.clip(coord + w - w // 2, 0, axis_len)
            cnt = cnt * (hi - lo)
        out.append(1.0 / cnt)
    return np.broadcast_to(np.stack(out)[:, :, None], (len(POOL_WINDOWS), length, 128)).astype(np.float32)


def _pool(pbf, w_pool, pool_scale, *, batch, ctx_len, seq):
    nxb = (batch * ctx_len) // seq
    pw = w_pool.shape[0] * w_pool.shape[1]
    pad_rows = (max(POOL_WINDOWS) // 2) * GRID_W
    inv_c = jnp.asarray(_inv_counts(ctx_len, lambda n: [(n, np.arange(n))]))
    inv_x = jnp.asarray(_inv_counts(
        seq, lambda n: [(n // GRID_W, np.arange(n) // GRID_W), (GRID_W, np.arange(n) % GRID_W)]))
    full = lambda a: pl.BlockSpec(a.shape, lambda b: (0,) * a.ndim)
    return pl.pallas_call(
        functools.partial(_pool_kernel, pad_rows=pad_rows),
        grid=(batch,),
        in_specs=[
            pl.BlockSpec((ctx_len, pw), lambda b: (b, 0)),
            pl.BlockSpec((seq, pw), lambda b: (nxb + b, 0)),
            full(inv_c), full(inv_x), full(w_pool),
            pl.BlockSpec((1, pw), lambda b: (0, 0)),
        ],
        out_specs=[
            pl.BlockSpec((ctx_len, pw), lambda b: (b, 0)),
            pl.BlockSpec((seq, pw), lambda b: (b, 0)),
        ],
        out_shape=[
            jax.ShapeDtypeStruct((batch * ctx_len, pw), BF16),
            jax.ShapeDtypeStruct((batch * seq, pw), BF16),
        ],
        scratch_shapes=[pltpu.VMEM((seq + 2 * pad_rows, w_pool.shape[1]), F32)],
        compiler_params=_cparams(1),
        name="pool_mixer",
    )(pbf, pbf, inv_c, inv_x, w_pool, pool_scale.reshape(1, pw))


def _route(sb, s):
    def rank_in_group(vals):
        ranks = []
        for i, vi in enumerate(vals):
            r = jnp.zeros(vi.shape, jnp.int32)
            for j, vj in enumerate(vals):
                if j == i:
                    continue
                ahead = (vj >= vi) if j < i else (vj > vi)
                r = r + jnp.where(ahead, 1, 0)
            ranks.append(r)
        return ranks

    best = None
    for g in range(N_GROUPS):
        vals = sb[g * EXPERTS_PER_GROUP:(g + 1) * EXPERTS_PER_GROUP]
        svals = s[g * EXPERTS_PER_GROUP:(g + 1) * EXPERTS_PER_GROUP]
        score = None
        for i in range(EXPERTS_PER_GROUP):
            for j in range(i + 1, EXPERTS_PER_GROUP):
                pair = vals[i] + vals[j]
                score = pair if score is None else jnp.maximum(score, pair)
        ranks = rank_in_group(vals)
        e1 = jnp.zeros(score.shape, jnp.int32)
        e2 = jnp.zeros(score.shape, jnp.int32)
        w1 = jnp.zeros(score.shape, F32)
        w2 = jnp.zeros(score.shape, F32)
        for i in range(EXPERTS_PER_GROUP):
            e1 = jnp.where(ranks[i] == 0, g * EXPERTS_PER_GROUP + i, e1)
            e2 = jnp.where(ranks[i] == 1, g * EXPERTS_PER_GROUP + i, e2)
            w1 = jnp.where(ranks[i] == 0, svals[i], w1)
            w2 = jnp.where(ranks[i] == 1, svals[i], w2)
        cand = (score, e1, e2, w1, w2)
        if best is None:
            best = cand
        else:
            take = cand[0] > best[0]
            best = tuple(jnp.where(take, c, b) for c, b in zip(cand, best))
    _, e1, e2, w1, w2 = best
    tot = w1 + w2
    return e1, e2, w1 / tot, w2 / tot


def _outproj_kernel(xc_ref, xx_ref, pc_ref, px_ref, ac_ref, ax_ref, gate_ref, scale_ref, shift_ref, gain_ref,
                    wo_ref, wrt_ref, rb_ref, before_ref, below_ref, x1_ref, xs_ref, slot_ref, wt_ref, len_ref,
                    *, n_ctx_tiles, half):
    is_ctx = pl.program_id(0) < n_ctx_tiles
    pool = jnp.where(is_ctx, pc_ref[...], px_ref[...])
    att = jnp.where(is_ctx, ac_ref[...], ax_ref[...])
    y = _dot(pool, wo_ref[0:half, :]) + _dot(att, wo_ref[half:, :])
    x1 = jnp.where(is_ctx, xc_ref[...], xx_ref[...]) + gate_ref[...] * y
    x1_ref[...] = x1
    h2 = _norm_mod(x1, gain_ref[...], scale_ref[...], shift_ref[...])
    logits = _dot3_nt(wrt_ref[...], h2)
    s = jax.nn.sigmoid(logits)
    sb = s + rb_ref[...]
    rows = lambda a: [a[i:i + 1, :] for i in range(N_EXPERTS)]
    e1, e2, w1, w2 = _route(rows(sb), rows(s))
    wt_ref[0:1, :] = w1
    wt_ref[1:2, :] = w2
    ie = lax.broadcasted_iota(jnp.int32, logits.shape, 0)
    oh1 = jnp.where(ie == e1, 1.0, 0.0)
    oh2 = jnp.where(ie == e2, 1.0, 0.0)
    c1 = jnp.sum(oh1, axis=1, keepdims=True)
    c2 = jnp.sum(oh2, axis=1, keepdims=True)
    seg_len = jnp.ceil((c1 + c2) * (1.0 / SEG_ALIGN)) * SEG_ALIGN
    seg_len_l = jnp.broadcast_to(seg_len, len_ref.shape)
    seg_off = _dot(below_ref[...], seg_len_l.astype(BF16))[:, 0:1]
    before1 = _dot(oh1.astype(BF16), before_ref[...]) + seg_off
    before2 = _dot(oh2.astype(BF16), before_ref[...]) + (seg_off + c1)
    slot1 = jnp.sum(oh1 * before1, axis=0, keepdims=True).astype(jnp.int32)
    slot2 = jnp.sum(oh2 * before2, axis=0, keepdims=True).astype(jnp.int32)
    slot_ref[0:1, :] = slot1
    slot_ref[1:2, :] = slot2
    len_ref[...] = seg_len_l.astype(jnp.int32)
    si = lax.broadcasted_iota(jnp.int32, (xs_ref.shape[0], slot1.shape[1]), 0)
    perm = jnp.where(si == slot1, 1.0, 0.0) + jnp.where(si == slot2, 1.0, 0.0)
    xs_ref[...] = _dot(perm.astype(BF16), h2.astype(BF16)).astype(BF16)


def _outproj(rows, pool_c, pool_x, att_c, att_x, modl, gain2, w_out, w_router_t, router_bias,
             *, tile, mod_row, first_tile, n_tiles):
    d = rows[0].shape[1]
    row_specs, row_args, _ = _row_specs(rows, tile, first_tile)
    half = pool_c.shape[1]
    n_ctx_tiles = pool_c.shape[0] // tile
    cidx = lambda i: (jnp.minimum(i + first_tile, n_ctx_tiles - 1), 0)
    xidx = lambda i: (jnp.maximum(i + first_tile - n_ctx_tiles, 0), 0)
    mod_spec = lambda m: pl.BlockSpec((None, None, 1, d), lambda i: (mod_row(i + first_tile), m, 0, 0))
    full = lambda a: pl.BlockSpec(a.shape, lambda i: (0,) * a.ndim)
    tok = lambda w: pl.BlockSpec((tile, w), lambda i: (i, 0))
    lane = pl.BlockSpec((2, tile), lambda i: (0, i))
    t_out = n_tiles * tile
    before = jnp.asarray(np.triu(np.ones((tile, tile), np.float32), 1), BF16)
    below = jnp.asarray(np.tril(np.ones((N_EXPERTS, N_EXPERTS), np.float32), -1), BF16)
    xs_rows = _xs_rows(tile)
    return pl.pallas_call(
        functools.partial(_outproj_kernel, n_ctx_tiles=n_ctx_tiles - first_tile, half=half),
        grid=(n_tiles,),
        in_specs=row_specs + [
            pl.BlockSpec((tile, half), cidx), pl.BlockSpec((tile, half), xidx),
            pl.BlockSpec((tile, half), cidx), pl.BlockSpec((tile, half), xidx),
            mod_spec(2), mod_spec(4), mod_spec(3), full(gain2), full(w_out), full(w_router_t),
            full(router_bias), full(before), full(below),
        ],
        out_specs=[tok(d), pl.BlockSpec((xs_rows, d), lambda i: (i, 0)), lane, lane,
                   pl.BlockSpec((None, N_EXPERTS, 128), lambda i: (i, 0, 0))],
        out_shape=[
            jax.ShapeDtypeStruct((t_out, d), F32),
            jax.ShapeDtypeStruct((n_tiles * xs_rows, d), BF16),
            jax.ShapeDtypeStruct((2, t_out), jnp.int32),
            jax.ShapeDtypeStruct((2, t_out), F32),
            jax.ShapeDtypeStruct((n_tiles, N_EXPERTS, 128), jnp.int32),
        ],
        compiler_params=_cparams(1),
        name="outproj_router",
    )(*row_args, pool_c, pool_x, att_c, att_x, modl, modl, modl, gain2, w_out, w_router_t, router_bias,
      before, below)


def _xs_rows(tile):
    return 2 * tile + N_EXPERTS * SEG_ALIGN


def _for_each_unit(n, fn):
    def body(u, carry):
        fn(u * SEG_ALIGN)
        return carry
    lax.fori_loop(0, n // SEG_ALIGN, body, 0)


def _expert_kernel(be_ref, nu_ref, usrc_ref, nval_ref, xs_ref, w1_ref, w3_ref, w2_ref, yb_ref,
                   xbuf, wb1, wb3, wb2, sem):
    i = pl.program_id(0)
    n_used = nu_ref[0]
    units = MOE_BM // SEG_ALIGN

    def fetch(blk, half, action):
        def body(u, carry):
            src = pl.multiple_of(usrc_ref[blk * units + u], SEG_ALIGN)
            dst = pl.multiple_of(u * SEG_ALIGN, SEG_ALIGN)
            action(pltpu.make_async_copy(xs_ref.at[pl.ds(src, SEG_ALIGN), :],
                                         xbuf.at[half, pl.ds(dst, SEG_ALIGN), :], sem.at[half]))
            return carry
        lax.fori_loop(0, units, body, 0, unroll=8)

    half = i % 2
    pl.when(i == 0)(lambda: fetch(i, 0, lambda c: c.start()))
    pl.when(i + 1 < n_used)(lambda: fetch(i + 1, 1 - half, lambda c: c.start()))
    used = i < n_used
    new_expert = (i == 0) | (be_ref[i] != be_ref[jnp.maximum(i - 1, 0)])

    @pl.when(used & new_expert)
    def _():
        for src, dst in ((w1_ref, wb1), (w3_ref, wb3), (w2_ref, wb2)):
            step = src.shape[0] // 4
            for r in range(0, src.shape[0], step):
                dst[r:r + step, :] = src[r:r + step, :].astype(BF16)

    @pl.when(used)
    def _():
        fetch(i, half, lambda c: c.wait())
        def groups(n_live):
            for g, r in enumerate(range(0, MOE_BM, EXPERT_ROWS)):
                if g < n_live:
                    x = xbuf[half, r:r + EXPERT_ROWS, :]
                    a = _dot(x, wb1[...])
                    b = _dot(x, wb3[...])
                    y = _dot((_silu(a) * b).astype(BF16), wb2[...]).astype(BF16)
                else:
                    y = jnp.zeros((EXPERT_ROWS, yb_ref.shape[1]), yb_ref.dtype)
                yb_ref[r:r + EXPERT_ROWS, :] = y

        n_groups = MOE_BM // EXPERT_ROWS
        live = (nval_ref[i] + EXPERT_ROWS - 1) // EXPERT_ROWS
        for n_live in range(1, n_groups + 1):
            pl.when(live == n_live)(functools.partial(groups, n_live))

    @pl.when(jnp.logical_not(used))
    def _():
        yb_ref[...] = jnp.zeros_like(yb_ref)


def _experts(xs, unit_src, block_e, n_used, n_valid, w1, w3, w2, layer, n_slots):
    d = xs.shape[1]
    de = w1.shape[3]
    nb = n_slots // MOE_BM
    wspec = lambda a: pl.BlockSpec((None, None) + a.shape[2:], lambda i, be, *_: (layer, be[i], 0, 0))
    return pl.pallas_call(
        _expert_kernel,
        grid_spec=pltpu.PrefetchScalarGridSpec(
            num_scalar_prefetch=4,
            grid=(nb,),
            in_specs=[pl.BlockSpec(memory_space=pl.ANY), wspec(w1), wspec(w3), wspec(w2)],
            out_specs=pl.BlockSpec((MOE_BM, d), lambda i, *_: (i, 0)),
            scratch_shapes=[pltpu.VMEM((2, MOE_BM, d), xs.dtype), pltpu.VMEM((d, de), BF16),
                            pltpu.VMEM((d, de), BF16), pltpu.VMEM((de, d), BF16),
                            pltpu.SemaphoreType.DMA((2,))],
        ),
        out_shape=jax.ShapeDtypeStruct((n_slots, d), BF16),
        compiler_params=_cparams(1),
        name="moe_experts",
    )(block_e, n_used, unit_src.astype(jnp.int32), n_valid.astype(jnp.int32), xs, w1, w3, w2)


def _combine_kernel(seg_off, src_row, seg_len, x_ref, slot_ref, wt_ref, gate_ref, gainf_ref, yb_ref, o_ref,
                    buf, sem, *, final_norm):
    i = pl.program_id(0)
    n = pl.num_programs(0)

    def fetch(tile_idx, half, action):
        for e in range(N_EXPERTS):
            j = tile_idx * N_EXPERTS + e

            def unit(offset, j=j):
                s0 = pl.multiple_of(src_row[j] + offset, SEG_ALIGN)
                d0 = pl.multiple_of(seg_off[j] + offset, SEG_ALIGN)
                action(pltpu.make_async_copy(yb_ref.at[pl.ds(s0, SEG_ALIGN), :],
                                             buf.at[half, pl.ds(d0, SEG_ALIGN), :], sem.at[half]))
            _for_each_unit(seg_len[j], unit)

    @pl.when(i == 0)
    def _():
        buf[...] = jnp.zeros_like(buf)
        fetch(i, 0, lambda c: c.start())

    half = i % 2
    pl.when(i + 1 < n)(lambda: fetch(i + 1, 1 - half, lambda c: c.start()))
    fetch(i, half, lambda c: c.wait())
    for rows in _row_groups(o_ref.shape[0]):
        slot = slot_ref[rows, :]
        wt = wt_ref[rows, :]
        li = lax.broadcasted_iota(jnp.int32, (slot.shape[0], buf.shape[1]), 1)
        mix = (jnp.where(li == slot[:, 0:1], wt[:, 0:1], 0.0)
               + jnp.where(li == slot[:, 1:2], wt[:, 1:2], 0.0))
        y = _dot(mix.astype(BF16), buf[half])
        x2 = x_ref[rows, :] + gate_ref[...] * y
        if final_norm:
            x2 = x2 * lax.rsqrt(jnp.mean(x2 * x2, axis=-1, keepdims=True) + EPS) * gainf_ref[...]
        o_ref[rows, :] = x2


def _combine(x1, yb, seg_off, src_row, seg_len, slot_cols, wt_cols, modl, gain_f, *, tile, mod_row,
             first_tile, final_norm):
    t, d = x1.shape
    flat = lambda a: a.reshape(-1).astype(jnp.int32)
    return pl.pallas_call(
        functools.partial(_combine_kernel, final_norm=final_norm),
        grid_spec=pltpu.PrefetchScalarGridSpec(
            num_scalar_prefetch=3,
            grid=(t // tile,),
            in_specs=[
                pl.BlockSpec((tile, d), lambda i, *_: (i, 0)),
                pl.BlockSpec((tile, 2), lambda i, *_: (i, 0)),
                pl.BlockSpec((tile, 2), lambda i, *_: (i, 0)),
                pl.BlockSpec((None, None, 1, d), lambda i, *_: (mod_row(i + first_tile), 5, 0, 0)),
                pl.BlockSpec((1, d), lambda i, *_: (0, 0)),
                pl.BlockSpec(memory_space=pl.ANY),
            ],
            out_specs=pl.BlockSpec((tile, d), lambda i, *_: (i, 0)),
            scratch_shapes=[pltpu.VMEM((2, _xs_rows(tile), d), yb.dtype), pltpu.SemaphoreType.DMA((2,))],
        ),
        out_shape=jax.ShapeDtypeStruct((t, d), F32),
        compiler_params=_cparams(1),
        name="moe_combine",
    )(flat(seg_off), flat(src_row), flat(seg_len), x1, slot_cols, wt_cols, modl, gain_f, yb)


def _slot_plan(seg_len, *, tile, n_slots):
    n_tiles = seg_len.shape[0]
    xs_rows = _xs_rows(tile)
    nb = n_slots // MOE_BM
    seg_off = jnp.cumsum(seg_len, axis=1) - seg_len
    rows_e = jnp.sum(seg_len, axis=0)
    padded = (rows_e + MOE_BM - 1) // MOE_BM * MOE_BM
    pend = jnp.cumsum(padded)
    pstart = pend - padded
    xb_row = pstart[None, :] + jnp.cumsum(seg_len, axis=0) - seg_len
    n_used = pend[-1] // MOE_BM
    blk = jnp.minimum(jnp.arange(nb, dtype=jnp.int32), n_used - 1)
    block_e = jnp.sum((pend[None, :] <= (blk * MOE_BM)[:, None]).astype(jnp.int32), axis=1)
    block_e = jnp.minimum(block_e, N_EXPERTS - 1).astype(jnp.int32)
    starts = xb_row.T.reshape(1, -1)
    lens = seg_len.T.reshape(1, -1)
    srcs = (jnp.arange(n_tiles, dtype=jnp.int32)[:, None] * xs_rows + seg_off).T.reshape(1, -1)
    unit_row = (jnp.arange(n_slots // SEG_ALIGN, dtype=jnp.int32) * SEG_ALIGN)[:, None]
    inside = (unit_row >= starts) & (unit_row < starts + lens)
    unit_src = jnp.sum(jnp.where(inside, srcs + unit_row - starts, 0), axis=1)
    unit_src = jnp.where(jnp.any(inside, axis=1), unit_src, xs_rows - SEG_ALIGN)
    own = block_e[:, None] == jnp.arange(N_EXPERTS, dtype=jnp.int32)[None, :]
    n_valid = jnp.clip(jnp.sum(jnp.where(own, (pstart + rows_e)[None, :], 0), axis=1) - blk * MOE_BM,
                       0, MOE_BM)
    return seg_off, xb_row, unit_src, block_e, n_used.astype(jnp.int32).reshape(1), n_valid


def kernel(x, c, ctx, c_ctx, w_mod, b_mod, norm1, norm2, w_in, w_gk2, b_gk, w_pool, pool_scale, gla_gain,
           w_out, w_router, router_bias, w1, w3, w2, norm_f):
    batch, seq, d = x.shape
    ctx_len = ctx.shape[1]
    depth = w_mod.shape[0]
    n_ctx = batch * ctx_len
    t_all = n_ctx + batch * seq
    def row_tile(limit):
        return max(t for t in (1024, 512, 256) if t <= limit and n_ctx % t == 0 and seq % t == 0)

    tile_a = row_tile(1024)
    tile_b = row_tile(512)

    def mod_row_for(tile):
        nct, per_b = n_ctx // tile, seq // tile
        return lambda i: jnp.where(i < nct, 0, 1 + (i - nct) // per_b)

    rows = (ctx.reshape(n_ctx, d), x.reshape(batch * seq, d), 0)

    cvecs = jnp.zeros((16, d), F32).at[0].set(c_ctx).at[1:1 + batch].set(c)
    mods = _modulation(cvecs, w_mod, b_mod).reshape(depth, 16, N_MOD, 1, d)

    w2g = jnp.zeros((depth, 128, 2 * GLA_QK), F32)
    w2g = w2g.at[:, 0:GATE_RANK, 0:GLA_QK].set(w_gk2[:, 0])
    w2g = w2g.at[:, GATE_RANK:2 * GATE_RANK, GLA_QK:].set(w_gk2[:, 1])
    bgk = b_gk.reshape(depth, 1, 2 * GLA_QK)
    w_out_b = w_out.astype(BF16)
    w_pool_b = w_pool.astype(BF16)
    w_router_t = w_router.T
    rbias = router_bias.reshape(N_EXPERTS, 1)
    g1 = lambda a: a.reshape(1, -1)

    out = None
    for l in range(depth):
        last = l == depth - 1
        pbf, lg = _inproj(rows, t_all, mods[l], g1(norm1[l]), w_in, l, w2g[l], bgk[l],
                          tile=tile_a, mod_row=mod_row_for(tile_a))
        att_c, att_x = _gla(pbf, lg, gla_gain[l], batch=batch, ctx_len=ctx_len, seq=seq)
        pool_c, pool_x = _pool(pbf, w_pool_b[l], pool_scale[l], batch=batch, ctx_len=ctx_len, seq=seq)
        first_tile = n_ctx // tile_b if last else 0
        n_tiles = t_all // tile_b - first_tile
        x1, xs, slots, wts, seg_len = _outproj(
            rows, pool_c, pool_x, att_c, att_x, mods[l], g1(norm2[l]), w_out_b[l], w_router_t, rbias,
            tile=tile_b, mod_row=mod_row_for(tile_b), first_tile=first_tile, n_tiles=n_tiles)
        seg_len = seg_len[:, :, 0]
        max_rows = n_tiles * (2 * tile_b + N_EXPERTS * (SEG_ALIGN - 1)) + N_EXPERTS * (MOE_BM - 1)
        n_slots = -(-max_rows // MOE_BM) * MOE_BM
        seg_off, xb_row, unit_src, block_e, n_used, n_valid = _slot_plan(
            seg_len, tile=tile_b, n_slots=n_slots)
        yb = _experts(xs, unit_src, block_e, n_used, n_valid, w1, w3, w2, l, n_slots)
        res = _combine(x1, yb, seg_off, xb_row, seg_len, slots.T, wts.T, mods[l], g1(norm_f), tile=tile_b,
                       mod_row=mod_row_for(tile_b), first_tile=first_tile, final_norm=last)
        if last:
            out = res
        else:
            rows = (res, res, n_ctx)
    return out.reshape(batch, seq, d)
```

```python
import functools

import numpy as np
import jax
import jax.numpy as jnp
from jax import lax
from jax.experimental import pallas as pl
from jax.experimental.pallas import tpu as pltpu

F32 = jnp.float32
BF16 = jnp.bfloat16

GRID_W = 64
POOL_GROUPS = 4
POOL_WINDOWS = (2, 4, 8, 16)
GLA_HEADS = 4
GLA_DK = 64
GLA_DV = 128
GLA_QK = GLA_HEADS * GLA_DK
GATE_RANK = 16
GATE_NORM = 16.0
N_EXPERTS = 16
N_GROUPS = 4
EXPERTS_PER_GROUP = 4
EPS = 1e-6
N_MOD = 6

VMEM_LIMIT_BYTES = 56 * 1024 * 1024
SPAN = 256
SUB = 32
N_SUB = SPAN // SUB
LOG2_E = 1.4426950408889634
EXP2_CLAMP = 115.0
BIG = 2 * SUB
N_BIG = SPAN // BIG
CROSS0 = 256
GLA_KW = 2 * CROSS0
MOE_BM = 512
EXPERT_ROWS = 256
SEG_ALIGN = 16


def _cparams(n_axes):
    return pltpu.CompilerParams(
        dimension_semantics=("arbitrary",) * n_axes, vmem_limit_bytes=VMEM_LIMIT_BYTES)


def _split(a):
    hi = a.astype(BF16)
    lo = (a - hi.astype(F32)).astype(BF16)
    return hi, lo


def _dot(a, b):
    return jnp.dot(a, b, preferred_element_type=F32)


def _dot_nt(a, b):
    return lax.dot_general(a, b, (((1,), (1,)), ((), ())), preferred_element_type=F32)


def _dot_tn(a, b):
    return lax.dot_general(a, b, (((0,), (0,)), ((), ())), preferred_element_type=F32)


def _dot3(a, b):
    ah, al = _split(a)
    bh, bl = _split(b)
    return _dot(ah, bh) + (_dot(ah, bl) + _dot(al, bh))


def _dot3_nt(a, b):
    ah, al = _split(a)
    bh, bl = _split(b)
    return _dot_nt(ah, bh) + (_dot_nt(ah, bl) + _dot_nt(al, bh))


def _dot_exact_lhs(m_bf16, b, terms=3):
    out = None
    rest = b
    for _ in range(terms):
        part = rest.astype(BF16)
        rest = rest - part.astype(F32)
        prod = _dot(m_bf16, part)
        out = prod if out is None else out + prod
    return out


def _silu(x):
    return x * jax.nn.sigmoid(x)


def _mod_kernel(c_ref, w_ref, b_ref, o_ref):
    o_ref[0] = _dot3(_silu(c_ref[...]), w_ref[0]) + b_ref[0]


def _modulation(cvecs, w_mod, b_mod):
    depth, d, _ = w_mod.shape
    rows = cvecs.shape[0]
    return pl.pallas_call(
        _mod_kernel,
        grid=(depth, N_MOD),
        in_specs=[
            pl.BlockSpec((rows, d), lambda l, j: (0, 0)),
            pl.BlockSpec((1, d, d), lambda l, j: (l, 0, j)),
            pl.BlockSpec((1, 1, d), lambda l, j: (l, 0, j)),
        ],
        out_specs=pl.BlockSpec((1, rows, d), lambda l, j: (l, 0, j)),
        out_shape=jax.ShapeDtypeStruct((depth, rows, N_MOD * d), F32),
        compiler_params=_cparams(2),
        name="adaln_vectors",
    )(cvecs, w_mod, b_mod.reshape(depth, 1, N_MOD * d))


def _norm_mod(x, gain, scale, shift):
    y = x * lax.rsqrt(jnp.mean(x * x, axis=-1, keepdims=True) + EPS)
    return (y * gain) * (1.0 + scale) + shift


def _row_specs(rows, tile, first_tile=0):
    xc, xx, first_latent_row = rows
    d = xc.shape[1]
    x_off = first_latent_row // tile
    nct = (xc.shape[0] // tile) if first_latent_row == 0 else x_off
    cspec = pl.BlockSpec((tile, d), lambda i, *_: (jnp.minimum(i + first_tile, nct - 1), 0))
    xspec = pl.BlockSpec((tile, d), lambda i, *_: (jnp.maximum(i + first_tile - nct, 0) + x_off, 0))
    return [cspec, xspec], [xc, xx], nct - first_tile


def _row_groups(n_rows, groups=2):
    size = n_rows // groups
    return [slice(g * size, (g + 1) * size) for g in range(groups)]


def _inproj_kernel(xc_ref, xx_ref, scale_ref, shift_ref, gain_ref, win_ref, w2_ref, bgk_ref,
                   p_ref, lg_ref, w_ref, wr_ref, *, n_chunk, n_ctx_tiles):
    n_main = p_ref.shape[1]
    width = n_main // n_chunk

    @pl.when(pl.program_id(0) == 0)
    def _():
        for n in range(n_chunk):
            cols = slice(n * width, (n + 1) * width)
            w_ref[:, cols] = win_ref[:, cols].astype(BF16)
        wr_ref[...] = jnp.zeros_like(wr_ref)
        wr_ref[:, 0:2 * GATE_RANK] = win_ref[:, n_main:].astype(BF16)

    is_ctx = pl.program_id(0) < n_ctx_tiles
    for rows in _row_groups(p_ref.shape[0]):
        x = jnp.where(is_ctx, xc_ref[rows, :], xx_ref[rows, :])
        h = _norm_mod(x, gain_ref[...], scale_ref[...], shift_ref[...])
        hb = h.astype(BF16)
        for n in range(n_chunk):
            cols = slice(n * width, (n + 1) * width)
            p_ref[rows, cols] = _dot(hb, w_ref[:, cols]).astype(BF16)
        r = _dot(hb, wr_ref[...])
        z = _dot3(r, w2_ref[...]) + bgk_ref[...]
        log_sig = jnp.minimum(z, 0.0) - jnp.log(1.0 + jnp.exp(-jnp.abs(z)))
        lg_ref[rows, :] = log_sig * (LOG2_E / GATE_NORM)


def _inproj(rows, t, modl, gain, w_in_all, layer, w2, bgk, *, tile, mod_row):
    d = rows[0].shape[1]
    n_main = w_in_all.shape[2] - 2 * GATE_RANK
    n_gate = w2.shape[1]
    mod_spec = lambda m: pl.BlockSpec((None, None, 1, d), lambda i: (mod_row(i), m, 0, 0))
    full = lambda a: pl.BlockSpec(a.shape, lambda i: (0,) * a.ndim)
    row_specs, row_args, n_ctx_tiles = _row_specs(rows, tile)
    return pl.pallas_call(
        functools.partial(_inproj_kernel, n_chunk=4, n_ctx_tiles=n_ctx_tiles),
        grid=(t // tile,),
        in_specs=row_specs + [
            mod_spec(1), mod_spec(0), full(gain),
            pl.BlockSpec((None,) + w_in_all.shape[1:], lambda i: (layer, 0, 0)), full(w2), full(bgk),
        ],
        out_specs=[
            pl.BlockSpec((tile, n_main), lambda i: (i, 0)),
            pl.BlockSpec((tile, n_gate), lambda i: (i, 0)),
        ],
        out_shape=[
            jax.ShapeDtypeStruct((t, n_main), BF16),
            jax.ShapeDtypeStruct((t, n_gate), F32),
        ],
        scratch_shapes=[pltpu.VMEM((d, n_main), BF16), pltpu.VMEM((d, 128), BF16)],
        compiler_params=_cparams(1),
        name="norm_inproj",
    )(*row_args, modl, modl, gain, w_in_all, w2, bgk)


def _store_heads(dst_ref, rows, col0, val):
    half = (col0 // GLA_DK) & 1
    plain = val.astype(BF16)
    turned = pltpu.roll(val, GLA_DK, axis=1).astype(BF16)
    for h in range(GLA_HEADS):
        if (h & 1) == half:
            src = plain[:, h * GLA_DK:(h + 1) * GLA_DK]
        else:
            hh = (h + 1) % GLA_HEADS
            src = turned[:, hh * GLA_DK:(hh + 1) * GLA_DK]
        dst_ref[h, rows, col0:col0 + GLA_DK] = src


def _gla_span(q_ref, k_ref, v_ref, lg_ref, o_ref, row0, st_ref, qh_ref, kh_ref, b_ref, qf_ref, kf_ref,
              tri_ref, dmask_ref, smask_ref, *, rev):
    rows = pl.ds(row0, SPAN)
    dcol = slice(rev * GLA_QK, (rev + 1) * GLA_QK)
    b_ref[...] = _dot_exact_lhs(tri_ref[rev], lg_ref[rows, dcol])
    qf_ref[...] = q_ref[rows, :].astype(F32) * (GLA_DK ** -0.5)
    kf_ref[...] = k_ref[rows, :].astype(F32)
    e_tot = b_ref[0:1, :] if rev else b_ref[SPAN - 1:SPAN, :]

    def slab(col0, q_rows, k_rows, ref_row, clamp=None):
        e_r = b_ref[ref_row:ref_row + 1, :]
        dq = b_ref[q_rows, :] - e_r
        dk = e_r - b_ref[k_rows, :]
        if clamp is not None:
            dq, dk = jnp.minimum(dq, clamp), jnp.minimum(dk, clamp)
        _store_heads(qh_ref, q_rows, col0, qf_ref[q_rows, :] * jnp.exp2(dq))
        _store_heads(kh_ref, k_rows, col0, kf_ref[k_rows, :] * jnp.exp2(dk))

    for s in range(N_BIG - 1):
        j = s + 1 if rev else s
        blk = slice(j * BIG, (j + 1) * BIG)
        beyond = slice(0, j * BIG) if rev else slice((j + 1) * BIG, SPAN)
        slab(CROSS0 + s * GLA_DK, beyond, blk, j * BIG if rev else j * BIG + BIG - 1)
    for i in range(N_SUB):
        blk = slice(i * SUB, (i + 1) * SUB)
        slab((i & 1) * GLA_DK, blk, blk, i * SUB + SUB // 2, clamp=EXP2_CLAMP)
    for g in range(N_BIG):
        first = slice(g * BIG, g * BIG + SUB)
        second = slice(g * BIG + SUB, (g + 1) * BIG)
        if rev:
            slab(2 * GLA_DK, first, second, g * BIG + SUB)
        else:
            slab(2 * GLA_DK, second, first, g * BIG + SUB - 1)

    st = st_ref[...]
    qd = (qf_ref[...] * jnp.exp2(b_ref[...])).astype(BF16)
    o_state = _dot_nt(qd, st.astype(BF16))
    in_block = dmask_ref[rev] != 0.0
    for h in range(GLA_HEADS):
        vc = slice(h * GLA_DV, (h + 1) * GLA_DV)
        a_diag = _dot_nt(qh_ref[h, :, 0:CROSS0], kh_ref[h, :, 0:CROSS0])
        a_cross = _dot_nt(qh_ref[h, :, CROSS0:], kh_ref[h, :, CROSS0:])
        a = (a_cross + jnp.where(in_block, a_diag, 0.0)).astype(BF16)
        o_ref[rows, vc] += _dot(a, v_ref[rows, vc]) + o_state[:, vc]
    kd =(kf_ref[...] * jnp.exp2(e_tot - b_ref[...])).astype(BF16)
    upd = _dot_tn(v_ref[rows, :], kd)
    st_ref[...] = st * jnp.exp2(e_tot) + upd * smask_ref[...]


def _gla_finish(o_ref, g_ref, gain, out_ref, n_rows):
    def body(c, carry):
        rows = pl.ds(pl.multiple_of(c * SPAN, SPAN), SPAN)
        for h in range(GLA_HEADS):
            vc = slice(h * GLA_DV, (h + 1) * GLA_DV)
            o = o_ref[rows, vc]
            y = o * lax.rsqrt(jnp.mean(o * o, axis=-1, keepdims=True) + EPS) * gain
            g = g_ref[rows, vc].astype(F32)
            out_ref[rows, vc] = (y * _silu(g)).astype(BF16)
        return carry
    lax.fori_loop(0, n_rows // SPAN, body, 0)


def _gla_kernel(qc, kc, vc, gc, lgc, qx, kx, vx, gx, lgx, gain_ref, tri, dmask, smask, outc, outx,
                oc, ox, *per_direction):
    n_c = qc.shape[0] // SPAN
    n_x = qx.shape[0] // SPAN
    names = ("st_ref", "qh_ref", "kh_ref", "b_ref", "qf_ref", "kf_ref")
    scratch = [dict(zip(names, per_direction[r * len(names):(r + 1) * len(names)])) for r in (0, 1)]
    for ref in (oc, ox) + tuple(s[n] for s in scratch for n in ("st_ref", "qh_ref", "kh_ref")):
        ref[...] = jnp.zeros_like(ref)
    spans = [functools.partial(_gla_span, tri_ref=tri, dmask_ref=dmask, smask_ref=smask, rev=rev,
                               **scratch[rev]) for rev in (0, 1)]
    for s in range(n_c):
        spans[0](qc, kc, vc, lgc, oc, s * SPAN)
        spans[1](qc, kc, vc, lgc, oc, (n_c - 1 - s) * SPAN)

    def body(s, carry):
        spans[0](qx, kx, vx, lgx, ox, pl.multiple_of(s * SPAN, SPAN))
        spans[1](qx, kx, vx, lgx, ox, pl.multiple_of((n_x - 1 - s) * SPAN, SPAN))
        return carry
    lax.fori_loop(0, n_x, body, 0)
    gain = gain_ref[...]
    _gla_finish(oc, gc, gain, outc, qc.shape[0])
    _gla_finish(ox, gx, gain, outx, qx.shape[0])


def _gla(pbf, lg, gla_gain, *, batch, ctx_len, seq):
    nxb = (batch * ctx_len) // seq
    assert nxb * seq == batch * ctx_len
    hv = GLA_HEADS * GLA_DV
    cspec = lambda w, j: pl.BlockSpec((ctx_len, w), lambda b: (b, j))
    xspec = lambda w, j: pl.BlockSpec((seq, w), lambda b: (nxb + b, j))
    full = lambda a: pl.BlockSpec(a.shape, lambda b: (0,) * a.ndim)
    pos = np.arange(SPAN)
    lower = pos[None, :] <= pos[:, None]
    same = (pos[None, :] // BIG) == (pos[:, None] // BIG)
    tri = jnp.asarray(np.stack([lower, lower.T]).astype(np.float32), BF16)
    dmask = jnp.asarray(np.stack([lower & same, lower.T & same]).astype(np.float32))
    smask = jnp.asarray(((np.arange(hv)[:, None] // GLA_DV) == (np.arange(GLA_QK)[None, :] // GLA_DK))
                        .astype(np.float32))
    return pl.pallas_call(
        _gla_kernel,
        grid=(batch,),
        in_specs=[
            cspec(GLA_QK, 2), cspec(GLA_QK, 3), cspec(hv, 2), cspec(hv, 3), cspec(2 * GLA_QK, 0),
            xspec(GLA_QK, 2), xspec(GLA_QK, 3), xspec(hv, 2), xspec(hv, 3), xspec(2 * GLA_QK, 0),
            pl.BlockSpec((1, GLA_DV), lambda b: (0, 0)), full(tri), full(dmask), full(smask),
        ],
        out_specs=[
            pl.BlockSpec((ctx_len, hv), lambda b: (b, 0)),
            pl.BlockSpec((seq, hv), lambda b: (b, 0)),
        ],
        out_shape=[
            jax.ShapeDtypeStruct((batch * ctx_len, hv), BF16),
            jax.ShapeDtypeStruct((batch * seq, hv), BF16),
        ],
        scratch_shapes=[
            pltpu.VMEM((ctx_len, hv), F32),
            pltpu.VMEM((seq, hv), F32),
        ] + 2 * [
            pltpu.VMEM((hv, GLA_QK), F32),
            pltpu.VMEM((GLA_HEADS, SPAN, GLA_KW), BF16),
            pltpu.VMEM((GLA_HEADS, SPAN, GLA_KW), BF16),
            pltpu.VMEM((SPAN, GLA_QK), F32),
            pltpu.VMEM((SPAN, GLA_QK), F32),
            pltpu.VMEM((SPAN, GLA_QK), F32),
        ],
        compiler_params=_cparams(1),
        name="gla_bidirectional",
    )(pbf, pbf, pbf, pbf, lg, pbf, pbf, pbf, pbf, lg, gla_gain.reshape(1, GLA_DV), tri, dmask, smask)


def _window_offsets(w):
    return range(-(w // 2), w - w // 2)


def _band(w, period):
    ri = lax.broadcasted_iota(jnp.int32, (SPAN, SPAN), 0)
    ci = lax.broadcasted_iota(jnp.int32, (SPAN, SPAN), 1)
    d = ci - ri
    shift = period.bit_length() - 1
    assert period == 1 << shift
    same_row = (ri >> shift) == (ci >> shift)
    inside = jnp.where(d >= -(w // 2), jnp.where(d <= w - w // 2 - 1, 1.0, 0.0), 0.0)
    return jnp.where(same_row, inside, 0.0).astype(BF16)


def _pool_kernel(uc_ref, ux_ref, invc_ref, invx_ref, wp_ref, ps_ref, outc_ref, outx_ref, pad_ref,
                 *, pad_rows):
    n_x = ux_ref.shape[0]
    gw = wp_ref.shape[1]
    pad_ref[0:pad_rows, :] = jnp.zeros((pad_rows, gw), F32)
    pad_ref[pad_rows + n_x:, :] = jnp.zeros((pad_rows, gw), F32)
    for g, w in enumerate(POOL_WINDOWS):
        gc = slice(g * gw, (g + 1) * gw)
        wp = wp_ref[g]
        ps = ps_ref[:, gc]
        band_c = _band(w, SPAN)
        for t in range(uc_ref.shape[0] // SPAN):
            rows = slice(t * SPAN, (t + 1) * SPAN)
            u = uc_ref[rows, gc].astype(F32)
            pooled = _dot_exact_lhs(band_c, u, terms=2) * invc_ref[g, rows, :] - u
            outc_ref[rows, gc] = (_dot(pooled.astype(BF16), wp) * ps).astype(BF16)
        pad_ref[pad_rows:pad_rows + n_x, :] = ux_ref[:, gc].astype(F32)
        band_x = _band(w, GRID_W)

        def body(t, carry):
            r0 = pl.multiple_of(t * SPAN, SPAN)
            acc = jnp.zeros((SPAN, gw), F32)
            for dr in _window_offsets(w):
                acc = acc + pad_ref[pl.ds(pad_rows + r0 + dr * GRID_W, SPAN), :]
            rows = pl.ds(r0, SPAN)
            u = pad_ref[pl.ds(pad_rows + r0, SPAN), :]
            pooled = _dot_exact_lhs(band_x, acc, terms=2) * invx_ref[g, rows, :] - u
            outx_ref[rows, gc] = (_dot(pooled.astype(BF16), wp) * ps).astype(BF16)
            return carry
        lax.fori_loop(0, n_x // SPAN, body, 0, unroll=8)


def _inv_counts(length, rows_of):
    out = []
    for w in POOL_WINDOWS:
        cnt = np.ones((length,), np.float64)
        for axis_len, coord in rows_of(length):
            lo = np.clip(coord - w // 2, 0, axis_len)
            hi = np.clip(coord + w - w // 2, 0, axis_len)
            cnt = cnt * (hi - lo)
        out.append(1.0 / cnt)
    return np.broadcast_to(np.stack(out)[:, :, None], (len(POOL_WINDOWS), length, 128)).astype(np.float32)


def _pool(pbf, w_pool, pool_scale, *, batch, ctx_len, seq):
    nxb = (batch * ctx_len) // seq
    pw = w_pool.shape[0] * w_pool.shape[1]
    pad_rows = (max(POOL_WINDOWS) // 2) * GRID_W
    inv_c = jnp.asarray(_inv_counts(ctx_len, lambda n: [(n, np.arange(n))]))
    inv_x = jnp.asarray(_inv_counts(
        seq, lambda n: [(n // GRID_W, np.arange(n) // GRID_W), (GRID_W, np.arange(n) % GRID_W)]))
    full = lambda a: pl.BlockSpec(a.shape, lambda b: (0,) * a.ndim)
    return pl.pallas_call(
        functools.partial(_pool_kernel, pad_rows=pad_rows),
        grid=(batch,),
        in_specs=[
            pl.BlockSpec((ctx_len, pw), lambda b: (b, 0)),
            pl.BlockSpec((seq, pw), lambda b: (nxb + b, 0)),
            full(inv_c), full(inv_x), full(w_pool),
            pl.BlockSpec((1, pw), lambda b: (0, 0)),
        ],
        out_specs=[
            pl.BlockSpec((ctx_len, pw), lambda b: (b, 0)),
            pl.BlockSpec((seq, pw), lambda b: (b, 0)),
        ],
        out_shape=[
            jax.ShapeDtypeStruct((batch * ctx_len, pw), BF16),
            jax.ShapeDtypeStruct((batch * seq, pw), BF16),
        ],
        scratch_shapes=[pltpu.VMEM((seq + 2 * pad_rows, w_pool.shape[1]), F32)],
        compiler_params=_cparams(1),
        name="pool_mixer",
    )(pbf, pbf, inv_c, inv_x, w_pool, pool_scale.reshape(1, pw))


def _route(sb, s):
    def rank_in_group(vals):
        ranks = []
        for i, vi in enumerate(vals):
            r = jnp.zeros(vi.shape, jnp.int32)
            for j, vj in enumerate(vals):
                if j == i:
                    continue
                ahead = (vj >= vi) if j < i else (vj > vi)
                r = r + jnp.where(ahead, 1, 0)
            ranks.append(r)
        return ranks

    best = None
    for g in range(N_GROUPS):
        vals = sb[g * EXPERTS_PER_GROUP:(g + 1) * EXPERTS_PER_GROUP]
        svals = s[g * EXPERTS_PER_GROUP:(g + 1) * EXPERTS_PER_GROUP]
        score = None
        for i in range(EXPERTS_PER_GROUP):
            for j in range(i + 1, EXPERTS_PER_GROUP):
                pair = vals[i] + vals[j]
                score = pair if score is None else jnp.maximum(score, pair)
        ranks = rank_in_group(vals)
        e1 = jnp.zeros(score.shape, jnp.int32)
        e2 = jnp.zeros(score.shape, jnp.int32)
        w1 = jnp.zeros(score.shape, F32)
        w2 = jnp.zeros(score.shape, F32)
        for i in range(EXPERTS_PER_GROUP):
            e1 = jnp.where(ranks[i] == 0, g * EXPERTS_PER_GROUP + i, e1)
            e2 = jnp.where(ranks[i] == 1, g * EXPERTS_PER_GROUP + i, e2)
            w1 = jnp.where(ranks[i] == 0, svals[i], w1)
            w2 = jnp.where(ranks[i] == 1, svals[i], w2)
        cand = (score, e1, e2, w1, w2)
        if best is None:
            best = cand
        else:
            take = cand[0] > best[0]
            best = tuple(jnp.where(take, c, b) for c, b in zip(cand, best))
    _, e1, e2, w1, w2 = best
    tot = w1 + w2
    return e1, e2, w1 / tot, w2 / tot


def _outproj_kernel(xc_ref, xx_ref, pc_ref, px_ref, ac_ref, ax_ref, gate_ref, scale_ref, shift_ref, gain_ref,
                    wo_ref, wrt_ref, rb_ref, before_ref, below_ref, x1_ref, xs_ref, slot_ref, wt_ref, len_ref,
                    *, n_ctx_tiles, half):
    is_ctx = pl.program_id(0) < n_ctx_tiles
    pool = jnp.where(is_ctx, pc_ref[...], px_ref[...])
    att = jnp.where(is_ctx, ac_ref[...], ax_ref[...])
    y = _dot(pool, wo_ref[0:half, :]) + _dot(att, wo_ref[half:, :])
    x1 = jnp.where(is_ctx, xc_ref[...], xx_ref[...]) + gate_ref[...] * y
    x1_ref[...] = x1
    h2 = _norm_mod(x1, gain_ref[...], scale_ref[...], shift_ref[...])
    logits = _dot3_nt(wrt_ref[...], h2)
    s = jax.nn.sigmoid(logits)
    sb = s + rb_ref[...]
    rows = lambda a: [a[i:i + 1, :] for i in range(N_EXPERTS)]
    e1, e2, w1, w2 = _route(rows(sb), rows(s))
    wt_ref[0:1, :] = w1
    wt_ref[1:2, :] = w2
    ie = lax.broadcasted_iota(jnp.int32, logits.shape, 0)
    oh1 = jnp.where(ie == e1, 1.0, 0.0)
    oh2 = jnp.where(ie == e2, 1.0, 0.0)
    c1 = jnp.sum(oh1, axis=1, keepdims=True)
    c2 = jnp.sum(oh2, axis=1, keepdims=True)
    seg_len = jnp.ceil((c1 + c2) * (1.0 / SEG_ALIGN)) * SEG_ALIGN
    seg_len_l = jnp.broadcast_to(seg_len, len_ref.shape)
    seg_off = _dot(below_ref[...], seg_len_l.astype(BF16))[:, 0:1]
    before1 = _dot(oh1.astype(BF16), before_ref[...]) + seg_off
    before2 = _dot(oh2.astype(BF16), before_ref[...]) + (seg_off + c1)
    slot1 = jnp.sum(oh1 * before1, axis=0, keepdims=True).astype(jnp.int32)
    slot2 = jnp.sum(oh2 * before2, axis=0, keepdims=True).astype(jnp.int32)
    slot_ref[0:1, :] = slot1
    slot_ref[1:2, :] = slot2
    len_ref[...] = seg_len_l.astype(jnp.int32)
    si = lax.broadcasted_iota(jnp.int32, (xs_ref.shape[0], slot1.shape[1]), 0)
    perm = jnp.where(si == slot1, 1.0, 0.0) + jnp.where(si == slot2, 1.0, 0.0)
    xs_ref[...] = _dot(perm.astype(BF16), h2.astype(BF16)).astype(BF16)


def _outproj(rows, pool_c, pool_x, att_c, att_x, modl, gain2, w_out, w_router_t, router_bias,
             *, tile, mod_row, first_tile, n_tiles):
    d = rows[0].shape[1]
    row_specs, row_args, _ = _row_specs(rows, tile, first_tile)
    half = pool_c.shape[1]
    n_ctx_tiles = pool_c.shape[0] // tile
    cidx = lambda i: (jnp.minimum(i + first_tile, n_ctx_tiles - 1), 0)
    xidx = lambda i: (jnp.maximum(i + first_tile - n_ctx_tiles, 0), 0)
    mod_spec = lambda m: pl.BlockSpec((None, None, 1, d), lambda i: (mod_row(i + first_tile), m, 0, 0))
    full = lambda a: pl.BlockSpec(a.shape, lambda i: (0,) * a.ndim)
    tok = lambda w: pl.BlockSpec((tile, w), lambda i: (i, 0))
    lane = pl.BlockSpec((2, tile), lambda i: (0, i))
    t_out = n_tiles * tile
    before = jnp.asarray(np.triu(np.ones((tile, tile), np.float32), 1), BF16)
    below = jnp.asarray(np.tril(np.ones((N_EXPERTS, N_EXPERTS), np.float32), -1), BF16)
    xs_rows = _xs_rows(tile)
    return pl.pallas_call(
        functools.partial(_outproj_kernel, n_ctx_tiles=n_ctx_tiles - first_tile, half=half),
        grid=(n_tiles,),
        in_specs=row_specs + [
            pl.BlockSpec((tile, half), cidx), pl.BlockSpec((tile, half), xidx),
            pl.BlockSpec((tile, half), cidx), pl.BlockSpec((tile, half), xidx),
            mod_spec(2), mod_spec(4), mod_spec(3), full(gain2), full(w_out), full(w_router_t),
            full(router_bias), full(before), full(below),
        ],
        out_specs=[tok(d), pl.BlockSpec((xs_rows, d), lambda i: (i, 0)), lane, lane,
                   pl.BlockSpec((None, N_EXPERTS, 128), lambda i: (i, 0, 0))],
        out_shape=[
            jax.ShapeDtypeStruct((t_out, d), F32),
            jax.ShapeDtypeStruct((n_tiles * xs_rows, d), BF16),
            jax.ShapeDtypeStruct((2, t_out), jnp.int32),
            jax.ShapeDtypeStruct((2, t_out), F32),
            jax.ShapeDtypeStruct((n_tiles, N_EXPERTS, 128), jnp.int32),
        ],
        compiler_params=_cparams(1),
        name="outproj_router",
    )(*row_args, pool_c, pool_x, att_c, att_x, modl, modl, modl, gain2, w_out, w_router_t, router_bias,
      before, below)


def _xs_rows(tile):
    return 2 * tile + N_EXPERTS * SEG_ALIGN


def _for_each_unit(n, fn):
    def body(u, carry):
        fn(u * SEG_ALIGN)
        return carry
    lax.fori_loop(0, n // SEG_ALIGN, body, 0)


def _expert_kernel(be_ref, nu_ref, usrc_ref, nval_ref, xs_ref, w1_ref, w3_ref, w2_ref, yb_ref,
                   xbuf, wb1, wb3, wb2, sem):
    i = pl.program_id(0)
    n_used = nu_ref[0]
    units = MOE_BM // SEG_ALIGN

    def fetch(blk, half, action):
        def body(u, carry):
            src = pl.multiple_of(usrc_ref[blk * units + u], SEG_ALIGN)
            dst = pl.multiple_of(u * SEG_ALIGN, SEG_ALIGN)
            action(pltpu.make_async_copy(xs_ref.at[pl.ds(src, SEG_ALIGN), :],
                                         xbuf.at[half, pl.ds(dst, SEG_ALIGN), :], sem.at[half]))
            return carry
        lax.fori_loop(0, units, body, 0, unroll=8)

    half = i % 2
    pl.when(i == 0)(lambda: fetch(i, 0, lambda c: c.start()))
    pl.when(i + 1 < n_used)(lambda: fetch(i + 1, 1 - half, lambda c: c.start()))
    used = i < n_used
    new_expert = (i == 0) | (be_ref[i] != be_ref[jnp.maximum(i - 1, 0)])

    @pl.when(used & new_expert)
    def _():
        for src, dst in ((w1_ref, wb1), (w3_ref, wb3), (w2_ref, wb2)):
            step = src.shape[0] // 4
            for r in range(0, src.shape[0], step):
                dst[r:r + step, :] = src[r:r + step, :].astype(BF16)

    @pl.when(used)
    def _():
        fetch(i, half, lambda c: c.wait())
        def groups(n_live):
            for g, r in enumerate(range(0, MOE_BM, EXPERT_ROWS)):
                if g < n_live:
                    x = xbuf[half, r:r + EXPERT_ROWS, :]
                    a = _dot(x, wb1[...])
                    b = _dot(x, wb3[...])
                    y = _dot((_silu(a) * b).astype(BF16), wb2[...]).astype(BF16)
                else:
                    y = jnp.zeros((EXPERT_ROWS, yb_ref.shape[1]), yb_ref.dtype)
                yb_ref[r:r + EXPERT_ROWS, :] = y

        n_groups = MOE_BM // EXPERT_ROWS
        live = (nval_ref[i] + EXPERT_ROWS - 1) // EXPERT_ROWS
        for n_live in range(1, n_groups + 1):
            pl.when(live == n_live)(functools.partial(groups, n_live))

    @pl.when(jnp.logical_not(used))
    def _():
        yb_ref[...] = jnp.zeros_like(yb_ref)


def _experts(xs, unit_src, block_e, n_used, n_valid, w1, w3, w2, layer, n_slots):
    d = xs.shape[1]
    de = w1.shape[3]
    nb = n_slots // MOE_BM
    wspec = lambda a: pl.BlockSpec((None, None) + a.shape[2:], lambda i, be, *_: (layer, be[i], 0, 0))
    return pl.pallas_call(
        _expert_kernel,
        grid_spec=pltpu.PrefetchScalarGridSpec(
            num_scalar_prefetch=4,
            grid=(nb,),
            in_specs=[pl.BlockSpec(memory_space=pl.ANY), wspec(w1), wspec(w3), wspec(w2)],
            out_specs=pl.BlockSpec((MOE_BM, d), lambda i, *_: (i, 0)),
            scratch_shapes=[pltpu.VMEM((2, MOE_BM, d), xs.dtype), pltpu.VMEM((d, de), BF16),
                            pltpu.VMEM((d, de), BF16), pltpu.VMEM((de, d), BF16),
                            pltpu.SemaphoreType.DMA((2,))],
        ),
        out_shape=jax.ShapeDtypeStruct((n_slots, d), BF16),
        compiler_params=_cparams(1),
        name="moe_experts",
    )(block_e, n_used, unit_src.astype(jnp.int32), n_valid.astype(jnp.int32), xs, w1, w3, w2)


def _combine_kernel(seg_off, src_row, seg_len, x_ref, slot_ref, wt_ref, gate_ref, gainf_ref, yb_ref, o_ref,
                    buf, sem, *, final_norm):
    i = pl.program_id(0)
    n = pl.num_programs(0)

    def fetch(tile_idx, half, action):
        for e in range(N_EXPERTS):
            j = tile_idx * N_EXPERTS + e

            def unit(offset, j=j):
                s0 = pl.multiple_of(src_row[j] + offset, SEG_ALIGN)
                d0 = pl.multiple_of(seg_off[j] + offset, SEG_ALIGN)
                action(pltpu.make_async_copy(yb_ref.at[pl.ds(s0, SEG_ALIGN), :],
                                             buf.at[half, pl.ds(d0, SEG_ALIGN), :], sem.at[half]))
            _for_each_unit(seg_len[j], unit)

    @pl.when(i == 0)
    def _():
        buf[...] = jnp.zeros_like(buf)
        fetch(i, 0, lambda c: c.start())

    half = i % 2
    pl.when(i + 1 < n)(lambda: fetch(i + 1, 1 - half, lambda c: c.start()))
    fetch(i, half, lambda c: c.wait())
    for rows in _row_groups(o_ref.shape[0]):
        slot = slot_ref[rows, :]
        wt = wt_ref[rows, :]
        li = lax.broadcasted_iota(jnp.int32, (slot.shape[0], buf.shape[1]), 1)
        mix = (jnp.where(li == slot[:, 0:1], wt[:, 0:1], 0.0)
               + jnp.where(li == slot[:, 1:2], wt[:, 1:2], 0.0))
        y = _dot(mix.astype(BF16), buf[half])
        x2 = x_ref[rows, :] + gate_ref[...] * y
        if final_norm:
            x2 = x2 * lax.rsqrt(jnp.mean(x2 * x2, axis=-1, keepdims=True) + EPS) * gainf_ref[...]
        o_ref[rows, :] = x2


def _combine(x1, yb, seg_off, src_row, seg_len, slot_cols, wt_cols, modl, gain_f, *, tile, mod_row,
             first_tile, final_norm):
    t, d = x1.shape
    flat = lambda a: a.reshape(-1).astype(jnp.int32)
    return pl.pallas_call(
        functools.partial(_combine_kernel, final_norm=final_norm),
        grid_spec=pltpu.PrefetchScalarGridSpec(
            num_scalar_prefetch=3,
            grid=(t // tile,),
            in_specs=[
                pl.BlockSpec((tile, d), lambda i, *_: (i, 0)),
                pl.BlockSpec((tile, 2), lambda i, *_: (i, 0)),
                pl.BlockSpec((tile, 2), lambda i, *_: (i, 0)),
                pl.BlockSpec((None, None, 1, d), lambda i, *_: (mod_row(i + first_tile), 5, 0, 0)),
                pl.BlockSpec((1, d), lambda i, *_: (0, 0)),
                pl.BlockSpec(memory_space=pl.ANY),
            ],
            out_specs=pl.BlockSpec((tile, d), lambda i, *_: (i, 0)),
            scratch_shapes=[pltpu.VMEM((2, _xs_rows(tile), d), yb.dtype), pltpu.SemaphoreType.DMA((2,))],
        ),
        out_shape=jax.ShapeDtypeStruct((t, d), F32),
        compiler_params=_cparams(1),
        name="moe_combine",
    )(flat(seg_off), flat(src_row), flat(seg_len), x1, slot_cols, wt_cols, modl, gain_f, yb)


def _slot_plan(seg_len, *, tile, n_slots):
    n_tiles = seg_len.shape[0]
    xs_rows = _xs_rows(tile)
    nb = n_slots // MOE_BM
    seg_off = jnp.cumsum(seg_len, axis=1) - seg_len
    rows_e = jnp.sum(seg_len, axis=0)
    padded = (rows_e + MOE_BM - 1) // MOE_BM * MOE_BM
    pend = jnp.cumsum(padded)
    pstart = pend - padded
    xb_row = pstart[None, :] + jnp.cumsum(seg_len, axis=0) - seg_len
    n_used = pend[-1] // MOE_BM
    blk = jnp.minimum(jnp.arange(nb, dtype=jnp.int32), n_used - 1)
    block_e = jnp.sum((pend[None, :] <= (blk * MOE_BM)[:, None]).astype(jnp.int32), axis=1)
    block_e = jnp.minimum(block_e, N_EXPERTS - 1).astype(jnp.int32)
    starts = xb_row.T.reshape(1, -1)
    lens = seg_len.T.reshape(1, -1)
    srcs = (jnp.arange(n_tiles, dtype=jnp.int32)[:, None] * xs_rows + seg_off).T.reshape(1, -1)
    unit_row = (jnp.arange(n_slots // SEG_ALIGN, dtype=jnp.int32) * SEG_ALIGN)[:, None]
    inside = (unit_row >= starts) & (unit_row < starts + lens)
    unit_src = jnp.sum(jnp.where(inside, srcs + unit_row - starts, 0), axis=1)
    unit_src = jnp.where(jnp.any(inside, axis=1), unit_src, xs_rows - SEG_ALIGN)
    own = block_e[:, None] == jnp.arange(N_EXPERTS, dtype=jnp.int32)[None, :]
    n_valid = jnp.clip(jnp.sum(jnp.where(own, (pstart + rows_e)[None, :], 0), axis=1) - blk * MOE_BM,
                       0, MOE_BM)
    return seg_off, xb_row, unit_src, block_e, n_used.astype(jnp.int32).reshape(1), n_valid


def kernel(x, c, ctx, c_ctx, w_mod, b_mod, norm1, norm2, w_in, w_gk2, b_gk, w_pool, pool_scale, gla_gain,
           w_out, w_router, router_bias, w1, w3, w2, norm_f):
    batch, seq, d = x.shape
    ctx_len = ctx.shape[1]
    depth = w_mod.shape[0]
    n_ctx = batch * ctx_len
    t_all = n_ctx + batch * seq
    def row_tile(limit):
        return max(t for t in (1024, 512, 256) if t <= limit and n_ctx % t == 0 and seq % t == 0)

    tile_a = row_tile(1024)
    tile_b = row_tile(512)

    def mod_row_for(tile):
        nct, per_b = n_ctx // tile, seq // tile
        return lambda i: jnp.where(i < nct, 0, 1 + (i - nct) // per_b)

    rows = (ctx.reshape(n_ctx, d), x.reshape(batch * seq, d), 0)

    cvecs = jnp.zeros((16, d), F32).at[0].set(c_ctx).at[1:1 + batch].set(c)
    mods = _modulation(cvecs, w_mod, b_mod).reshape(depth, 16, N_MOD, 1, d)

    w2g = jnp.zeros((depth, 128, 2 * GLA_QK), F32)
    w2g = w2g.at[:, 0:GATE_RANK, 0:GLA_QK].set(w_gk2[:, 0])
    w2g = w2g.at[:, GATE_RANK:2 * GATE_RANK, GLA_QK:].set(w_gk2[:, 1])
    bgk = b_gk.reshape(depth, 1, 2 * GLA_QK)
    w_out_b = w_out.astype(BF16)
    w_pool_b = w_pool.astype(BF16)
    w_router_t = w_router.T
    rbias = router_bias.reshape(N_EXPERTS, 1)
    g1 = lambda a: a.reshape(1, -1)

    out = None
    for l in range(depth):
        last = l == depth - 1
        pbf, lg = _inproj(rows, t_all, mods[l], g1(norm1[l]), w_in, l, w2g[l], bgk[l],
                          tile=tile_a, mod_row=mod_row_for(tile_a))
        att_c, att_x = _gla(pbf, lg, gla_gain[l], batch=batch, ctx_len=ctx_len, seq=seq)
        pool_c, pool_x = _pool(pbf, w_pool_b[l], pool_scale[l], batch=batch, ctx_len=ctx_len, seq=seq)
        first_tile = n_ctx // tile_b if last else 0
        n_tiles = t_all // tile_b - first_tile
        x1, xs, slots, wts, seg_len = _outproj(
            rows, pool_c, pool_x, att_c, att_x, mods[l], g1(norm2[l]), w_out_b[l], w_router_t, rbias,
            tile=tile_b, mod_row=mod_row_for(tile_b), first_tile=first_tile, n_tiles=n_tiles)
        seg_len = seg_len[:, :, 0]
        max_rows = n_tiles * (2 * tile_b + N_EXPERTS * (SEG_ALIGN - 1)) + N_EXPERTS * (MOE_BM - 1)
        n_slots = -(-max_rows // MOE_BM) * MOE_BM
        seg_off, xb_row, unit_src, block_e, n_used, n_valid = _slot_plan(
            seg_len, tile=tile_b, n_slots=n_slots)
        yb = _experts(xs, unit_src, block_e, n_used, n_valid, w1, w3, w2, l, n_slots)
        res = _combine(x1, yb, seg_off, xb_row, seg_len, slots.T, wts.T, mods[l], g1(norm_f), tile=tile_b,
                       mod_row=mod_row_for(tile_b), first_tile=first_tile, final_norm=last)
        if last:
            out = res
        else:
            rows = (res, res, n_ctx)
    return out.reshape(batch, seq, d)
```

```python
import functools

import numpy as np
import jax
import jax.numpy as jnp
from jax import lax
from jax.experimental import pallas as pl
from jax.experimental.pallas import tpu as pltpu

F32 = jnp.float32
BF16 = jnp.bfloat16

GRID_W = 64
POOL_GROUPS = 4
POOL_WINDOWS = (2, 4, 8, 16)
GLA_HEADS = 4
GLA_DK = 64
GLA_DV = 128
GLA_QK = GLA_HEADS * GLA_DK
GATE_RANK = 16
GATE_NORM = 16.0
N_EXPERTS = 16
N_GROUPS = 4
EXPERTS_PER_GROUP = 4
EPS = 1e-6
N_MOD = 6

VMEM_LIMIT_BYTES = 56 * 1024 * 1024
SPAN = 256
SUB = 32
N_SUB = SPAN // SUB
LOG2_E = 1.4426950408889634
EXP2_CLAMP = 115.0
BIG = 2 * SUB
N_BIG = SPAN // BIG
CROSS0 = 256
GLA_KW = 2 * CROSS0
MOE_BM = 512
EXPERT_ROWS = 256
SEG_ALIGN = 16


def _cparams(n_axes):
    return pltpu.CompilerParams(
        dimension_semantics=("arbitrary",) * n_axes, vmem_limit_bytes=VMEM_LIMIT_BYTES)


def _split(a):
    hi = a.astype(BF16)
    lo = (a - hi.astype(F32)).astype(BF16)
    return hi, lo


def _dot(a, b):
    return jnp.dot(a, b, preferred_element_type=F32)


def _dot_nt(a, b):
    return lax.dot_general(a, b, (((1,), (1,)), ((), ())), preferred_element_type=F32)


def _dot_tn(a, b):
    return lax.dot_general(a, b, (((0,), (0,)), ((), ())), preferred_element_type=F32)


def _dot3(a, b):
    ah, al = _split(a)
    bh, bl = _split(b)
    return _dot(ah, bh) + (_dot(ah, bl) + _dot(al, bh))


def _dot3_nt(a, b):
    ah, al = _split(a)
    bh, bl = _split(b)
    return _dot_nt(ah, bh) + (_dot_nt(ah, bl) + _dot_nt(al, bh))


def _dot_exact_lhs(m_bf16, b, terms=3):
    out = None
    rest = b
    for _ in range(terms):
        part = rest.astype(BF16)
        rest = rest - part.astype(F32)
        prod = _dot(m_bf16, part)
        out = prod if out is None else out + prod
    return out


def _silu(x):
    return x * jax.nn.sigmoid(x)


def _mod_kernel(c_ref, w_ref, b_ref, o_ref):
    o_ref[0] = _dot3(_silu(c_ref[...]), w_ref[0]) + b_ref[0]


def _modulation(cvecs, w_mod, b_mod):
    depth, d, _ = w_mod.shape
    rows = cvecs.shape[0]
    return pl.pallas_call(
        _mod_kernel,
        grid=(depth, N_MOD),
        in_specs=[
            pl.BlockSpec((rows, d), lambda l, j: (0, 0)),
            pl.BlockSpec((1, d, d), lambda l, j: (l, 0, j)),
            pl.BlockSpec((1, 1, d), lambda l, j: (l, 0, j)),
        ],
        out_specs=pl.BlockSpec((1, rows, d), lambda l, j: (l, 0, j)),
        out_shape=jax.ShapeDtypeStruct((depth, rows, N_MOD * d), F32),
        compiler_params=_cparams(2),
        name="adaln_vectors",
    )(cvecs, w_mod, b_mod.reshape(depth, 1, N_MOD * d))


def _norm_mod(x, gain, scale, shift):
    y = x * lax.rsqrt(jnp.mean(x * x, axis=-1, keepdims=True) + EPS)
    return (y * gain) * (1.0 + scale) + shift


def _row_specs(rows, tile, first_tile=0):
    xc, xx, first_latent_row = rows
    d = xc.shape[1]
    x_off = first_latent_row // tile
    nct = (xc.shape[0] // tile) if first_latent_row == 0 else x_off
    cspec = pl.BlockSpec((tile, d), lambda i, *_: (jnp.minimum(i + first_tile, nct - 1), 0))
    xspec = pl.BlockSpec((tile, d), lambda i, *_: (jnp.maximum(i + first_tile - nct, 0) + x_off, 0))
    return [cspec, xspec], [xc, xx], nct - first_tile


def _row_groups(n_rows, groups=2):
    size = n_rows // groups
    return [slice(g * size, (g + 1) * size) for g in range(groups)]


def _inproj_kernel(xc_ref, xx_ref, scale_ref, shift_ref, gain_ref, win_ref, w2_ref, bgk_ref,
                   p_ref, lg_ref, w_ref, wr_ref, *, n_chunk, n_ctx_tiles):
    n_main = p_ref.shape[1]
    width = n_main // n_chunk

    @pl.when(pl.program_id(0) == 0)
    def _():
        for n in range(n_chunk):
            cols = slice(n * width, (n + 1) * width)
            w_ref[:, cols] = win_ref[:, cols].astype(BF16)
        wr_ref[...] = jnp.zeros_like(wr_ref)
        wr_ref[:, 0:2 * GATE_RANK] = win_ref[:, n_main:].astype(BF16)

    is_ctx = pl.program_id(0) < n_ctx_tiles
    for rows in _row_groups(p_ref.shape[0], groups=4):
        x = jnp.where(is_ctx, xc_ref[rows, :], xx_ref[rows, :])
        h = _norm_mod(x, gain_ref[...], scale_ref[...], shift_ref[...])
        hb = h.astype(BF16)
        for n in range(n_chunk):
            cols = slice(n * width, (n + 1) * width)
            p_ref[rows, cols] = _dot(hb, w_ref[:, cols]).astype(BF16)
        r = _dot(hb, wr_ref[...])
        z = _dot3(r, w2_ref[...]) + bgk_ref[...]
        log_sig = jnp.minimum(z, 0.0) - jnp.log(1.0 + jnp.exp(-jnp.abs(z)))
        lg_ref[rows, :] = log_sig * (LOG2_E / GATE_NORM)


def _inproj(rows, t, modl, gain, w_in_all, layer, w2, bgk, *, tile, mod_row):
    d = rows[0].shape[1]
    n_main = w_in_all.shape[2] - 2 * GATE_RANK
    n_gate = w2.shape[1]
    mod_spec = lambda m: pl.BlockSpec((None, None, 1, d), lambda i: (mod_row(i), m, 0, 0))
    full = lambda a: pl.BlockSpec(a.shape, lambda i: (0,) * a.ndim)
    row_specs, row_args, n_ctx_tiles = _row_specs(rows, tile)
    return pl.pallas_call(
        functools.partial(_inproj_kernel, n_chunk=4, n_ctx_tiles=n_ctx_tiles),
        grid=(t // tile,),
        in_specs=row_specs + [
            mod_spec(1), mod_spec(0), full(gain),
            pl.BlockSpec((None,) + w_in_all.shape[1:], lambda i: (layer, 0, 0)), full(w2), full(bgk),
        ],
        out_specs=[
            pl.BlockSpec((tile, n_main), lambda i: (i, 0)),
            pl.BlockSpec((tile, n_gate), lambda i: (i, 0)),
        ],
        out_shape=[
            jax.ShapeDtypeStruct((t, n_main), BF16),
            jax.ShapeDtypeStruct((t, n_gate), F32),
        ],
        scratch_shapes=[pltpu.VMEM((d, n_main), BF16), pltpu.VMEM((d, 128), BF16)],
        compiler_params=_cparams(1),
        name="norm_inproj",
    )(*row_args, modl, modl, gain, w_in_all, w2, bgk)


def _store_heads(dst_ref, rows, col0, val):
    half = (col0 // GLA_DK) & 1
    plain = val.astype(BF16)
    turned = pltpu.roll(val, GLA_DK, axis=1).astype(BF16)
    for h in range(GLA_HEADS):
        if (h & 1) == half:
            src = plain[:, h * GLA_DK:(h + 1) * GLA_DK]
        else:
            hh = (h + 1) % GLA_HEADS
            src = turned[:, hh * GLA_DK:(hh + 1) * GLA_DK]
        dst_ref[h, rows, col0:col0 + GLA_DK] = src


def _gla_span(q_ref, k_ref, v_ref, lg_ref, o_ref, row0, st_ref, qh_ref, kh_ref, b_ref, qf_ref, kf_ref,
              tri_ref, dmask_ref, smask_ref, *, rev, assign):
    rows = pl.ds(row0, SPAN)
    dcol = slice(rev * GLA_QK, (rev + 1) * GLA_QK)
    b_ref[...] = _dot_exact_lhs(tri_ref[rev], lg_ref[rows, dcol])
    qf_ref[...] = q_ref[rows, :].astype(F32) * (GLA_DK ** -0.5)
    kf_ref[...] = k_ref[rows, :].astype(F32)
    e_tot = b_ref[0:1, :] if rev else b_ref[SPAN - 1:SPAN, :]

    def slab(col0, q_rows, k_rows, ref_row, clamp=None):
        e_r = b_ref[ref_row:ref_row + 1, :]
        dq = b_ref[q_rows, :] - e_r
        dk = e_r - b_ref[k_rows, :]
        if clamp is not None:
            dq, dk = jnp.minimum(dq, clamp), jnp.minimum(dk, clamp)
        _store_heads(qh_ref, q_rows, col0, qf_ref[q_rows, :] * jnp.exp2(dq))
        _store_heads(kh_ref, k_rows, col0, kf_ref[k_rows, :] * jnp.exp2(dk))

    for s in range(N_BIG - 1):
        j = s + 1 if rev else s
        blk = slice(j * BIG, (j + 1) * BIG)
        beyond = slice(0, j * BIG) if rev else slice((j + 1) * BIG, SPAN)
        slab(CROSS0 + s * GLA_DK, beyond, blk, j * BIG if rev else j * BIG + BIG - 1)
    for i in range(N_SUB):
        blk = slice(i * SUB, (i + 1) * SUB)
        slab((i & 1) * GLA_DK, blk, blk, i * SUB + SUB // 2, clamp=EXP2_CLAMP)
    for g in range(N_BIG):
        first = slice(g * BIG, g * BIG + SUB)
        second = slice(g * BIG + SUB, (g + 1) * BIG)
        if rev:
            slab(2 * GLA_DK, first, second, g * BIG + SUB)
        else:
            slab(2 * GLA_DK, second, first, g * BIG + SUB - 1)

    st = st_ref[...]
    qd = (qf_ref[...] * jnp.exp2(b_ref[...])).astype(BF16)
    o_state = _dot_nt(qd, st.astype(BF16))
    in_block = dmask_ref[rev] != 0.0
    for h in range(GLA_HEADS):
        vc = slice(h * GLA_DV, (h + 1) * GLA_DV)
        a_diag = _dot_nt(qh_ref[h, :, 0:CROSS0], kh_ref[h, :, 0:CROSS0])
        a_cross = _dot_nt(qh_ref[h, :, CROSS0:], kh_ref[h, :, CROSS0:])
        a = (a_cross + jnp.where(in_block, a_diag, 0.0)).astype(BF16)
        o_h = _dot(a, v_ref[rows, vc]) + o_state[:, vc]
        if assign:
            o_ref[rows, vc] = o_h
        else:
            o_ref[rows, vc] += o_h
    kd =(kf_ref[...] * jnp.exp2(e_tot - b_ref[...])).astype(BF16)
    upd = _dot_tn(v_ref[rows, :], kd)
    st_ref[...] = st * jnp.exp2(e_tot) + upd * smask_ref[...]


def _gla_finish(o_ref, g_ref, gain, out_ref, n_rows):
    def body(c, carry):
        rows = pl.ds(pl.multiple_of(c * SPAN, SPAN), SPAN)
        for h in range(GLA_HEADS):
            vc = slice(h * GLA_DV, (h + 1) * GLA_DV)
            o = o_ref[rows, vc]
            y = o * lax.rsqrt(jnp.mean(o * o, axis=-1, keepdims=True) + EPS) * gain
            g = g_ref[rows, vc].astype(F32)
            out_ref[rows, vc] = (y * _silu(g)).astype(BF16)
        return carry
    lax.fori_loop(0, n_rows // SPAN, body, 0)


def _gla_kernel(qc, kc, vc, gc, lgc, qx, kx, vx, gx, lgx, gain_ref, tri, dmask, smask, outc, outx,
                oc, ox, *per_direction):
    n_c = qc.shape[0] // SPAN
    n_x = qx.shape[0] // SPAN
    names = ("st_ref", "qh_ref", "kh_ref", "b_ref", "qf_ref", "kf_ref")
    scratch = [dict(zip(names, per_direction[r * len(names):(r + 1) * len(names)])) for r in (0, 1)]
    for ref in tuple(s[n] for s in scratch for n in ("st_ref", "qh_ref", "kh_ref")):
        ref[...] = jnp.zeros_like(ref)
    spans = [functools.partial(_gla_span, tri_ref=tri, dmask_ref=dmask, smask_ref=smask, rev=rev,
                               **scratch[rev]) for rev in (0, 1)]

    def step(refs, n, s, fwd_first, bwd_first):
        row = lambda idx: idx * SPAN if isinstance(idx, int) else pl.multiple_of(idx * SPAN, SPAN)
        spans[0](*refs, row(s), assign=fwd_first)
        spans[1](*refs, row(n - 1 - s), assign=bwd_first)

    def segment(refs, n):
        both_first = range(0, n // 2)
        both_second = range(n - n // 2, n)
        loop = lambda rng, first: lax.fori_loop(
            rng.start, rng.stop, lambda s, c: (step(refs, n, s, first, first), c)[1], 0)
        loop(both_first, True)
        if n % 2:
            step(refs, n, n // 2, True, False)
        loop(both_second, False)

    segment((qc, kc, vc, lgc, oc), n_c)
    segment((qx, kx, vx, lgx, ox), n_x)
    gain = gain_ref[...]
    _gla_finish(oc, gc, gain, outc, qc.shape[0])
    _gla_finish(ox, gx, gain, outx, qx.shape[0])


def _gla(pbf, lg, gla_gain, *, batch, ctx_len, seq):
    nxb = (batch * ctx_len) // seq
    assert nxb * seq == batch * ctx_len
    hv = GLA_HEADS * GLA_DV
    cspec = lambda w, j: pl.BlockSpec((ctx_len, w), lambda b: (b, j))
    xspec = lambda w, j: pl.BlockSpec((seq, w), lambda b: (nxb + b, j))
    full = lambda a: pl.BlockSpec(a.shape, lambda b: (0,) * a.ndim)
    pos = np.arange(SPAN)
    lower = pos[None, :] <= pos[:, None]
    same = (pos[None, :] // BIG) == (pos[:, None] // BIG)
    tri = jnp.asarray(np.stack([lower, lower.T]).astype(np.float32), BF16)
    dmask = jnp.asarray(np.stack([lower & same, lower.T & same]).astype(np.float32))
    smask = jnp.asarray(((np.arange(hv)[:, None] // GLA_DV) == (np.arange(GLA_QK)[None, :] // GLA_DK))
                        .astype(np.float32))
    return pl.pallas_call(
        _gla_kernel,
        grid=(batch,),
        in_specs=[
            cspec(GLA_QK, 2), cspec(GLA_QK, 3), cspec(hv, 2), cspec(hv, 3), cspec(2 * GLA_QK, 0),
            xspec(GLA_QK, 2), xspec(GLA_QK, 3), xspec(hv, 2), xspec(hv, 3), xspec(2 * GLA_QK, 0),
            pl.BlockSpec((1, GLA_DV), lambda b: (0, 0)), full(tri), full(dmask), full(smask),
        ],
        out_specs=[
            pl.BlockSpec((ctx_len, hv), lambda b: (b, 0)),
            pl.BlockSpec((seq, hv), lambda b: (b, 0)),
        ],
        out_shape=[
            jax.ShapeDtypeStruct((batch * ctx_len, hv), BF16),
            jax.ShapeDtypeStruct((batch * seq, hv), BF16),
        ],
        scratch_shapes=[
            pltpu.VMEM((ctx_len, hv), F32),
            pltpu.VMEM((seq, hv), F32),
        ] + 2 * [
            pltpu.VMEM((hv, GLA_QK), F32),
            pltpu.VMEM((GLA_HEADS, SPAN, GLA_KW), BF16),
            pltpu.VMEM((GLA_HEADS, SPAN, GLA_KW), BF16),
            pltpu.VMEM((SPAN, GLA_QK), F32),
            pltpu.VMEM((SPAN, GLA_QK), F32),
            pltpu.VMEM((SPAN, GLA_QK), F32),
        ],
        compiler_params=_cparams(1),
        name="gla_bidirectional",
    )(pbf, pbf, pbf, pbf, lg, pbf, pbf, pbf, pbf, lg, gla_gain.reshape(1, GLA_DV), tri, dmask, smask)


def _window_offsets(w):
    return range(-(w // 2), w - w // 2)


def _band(w, period):
    ri = lax.broadcasted_iota(jnp.int32, (SPAN, SPAN), 0)
    ci = lax.broadcasted_iota(jnp.int32, (SPAN, SPAN), 1)
    d = ci - ri
    shift = period.bit_length() - 1
    assert period == 1 << shift
    same_row = (ri >> shift) == (ci >> shift)
    inside = jnp.where(d >= -(w // 2), jnp.where(d <= w - w // 2 - 1, 1.0, 0.0), 0.0)
    return jnp.where(same_row, inside, 0.0).astype(BF16)


def _pool_kernel(uc_ref, ux_ref, invc_ref, invx_ref, wp_ref, ps_ref, outc_ref, outx_ref, pad_ref,
                 *, pad_rows):
    n_x = ux_ref.shape[0]
    gw = wp_ref.shape[1]
    pad_ref[0:pad_rows, :] = jnp.zeros((pad_rows, gw), F32)
    pad_ref[pad_rows + n_x:, :] = jnp.zeros((pad_rows, gw), F32)
    for g, w in enumerate(POOL_WINDOWS):
        gc = slice(g * gw, (g + 1) * gw)
        wp = wp_ref[g]
        ps = ps_ref[:, gc]
        band_c = _band(w, SPAN)
        for t in range(uc_ref.shape[0] // SPAN):
            rows = slice(t * SPAN, (t + 1) * SPAN)
            u = uc_ref[rows, gc].astype(F32)
            pooled = _dot_exact_lhs(band_c, u, terms=2) * invc_ref[g, rows, :] - u
            outc_ref[rows, gc] = (_dot(pooled.astype(BF16), wp) * ps).astype(BF16)
        pad_ref[pad_rows:pad_rows + n_x, :] = ux_ref[:, gc].astype(F32)
        band_x = _band(w, GRID_W)

        def body(t, carry):
            r0 = pl.multiple_of(t * SPAN, SPAN)
            acc = jnp.zeros((SPAN, gw), F32)
            for dr in _window_offsets(w):
                acc = acc + pad_ref[pl.ds(pad_rows + r0 + dr * GRID_W, SPAN), :]
            rows = pl.ds(r0, SPAN)
            u = pad_ref[pl.ds(pad_rows + r0, SPAN), :]
            pooled = _dot_exact_lhs(band_x, acc, terms=2) * invx_ref[g, rows, :] - u
            outx_ref[rows, gc] = (_dot(pooled.astype(BF16), wp) * ps).astype(BF16)
            return carry
        lax.fori_loop(0, n_x // SPAN, body, 0, unroll=8)


def _inv_counts(length, rows_of):
    out = []
    for w in POOL_WINDOWS:
        cnt = np.ones((length,), np.float64)
        for axis_len, coord in rows_of(length):
            lo = np.clip(coord - w // 2, 0, axis_len)
            hi = np.clip(coord + w - w // 2, 0, axis_len)
            cnt = cnt * (hi - lo)
        out.append(1.0 / cnt)
    return np.broadcast_to(np.stack(out)[:, :, None], (len(POOL_WINDOWS), length, 128)).astype(np.float32)


def _pool(pbf, w_pool, pool_scale, *, batch, ctx_len, seq):
    nxb = (batch * ctx_len) // seq
    pw = w_pool.shape[0] * w_pool.shape[1]
    pad_rows = (max(POOL_WINDOWS) // 2) * GRID_W
    inv_c = jnp.asarray(_inv_counts(ctx_len, lambda n: [(n, np.arange(n))]))
    inv_x = jnp.asarray(_inv_counts(
        seq, lambda n: [(n // GRID_W, np.arange(n) // GRID_W), (GRID_W, np.arange(n) % GRID_W)]))
    full = lambda a: pl.BlockSpec(a.shape, lambda b: (0,) * a.ndim)
    return pl.pallas_call(
        functools.partial(_pool_kernel, pad_rows=pad_rows),
        grid=(batch,),
        in_specs=[
            pl.BlockSpec((ctx_len, pw), lambda b: (b, 0)),
            pl.BlockSpec((seq, pw), lambda b: (nxb + b, 0)),
            full(inv_c), full(inv_x), full(w_pool),
            pl.BlockSpec((1, pw), lambda b: (0, 0)),
        ],
        out_specs=[
            pl.BlockSpec((ctx_len, pw), lambda b: (b, 0)),
            pl.BlockSpec((seq, pw), lambda b: (b, 0)),
        ],
        out_shape=[
            jax.ShapeDtypeStruct((batch * ctx_len, pw), BF16),
            jax.ShapeDtypeStruct((batch * seq, pw), BF16),
        ],
        scratch_shapes=[pltpu.VMEM((seq + 2 * pad_rows, w_pool.shape[1]), F32)],
        compiler_params=_cparams(1),
        name="pool_mixer",
    )(pbf, pbf, inv_c, inv_x, w_pool, pool_scale.reshape(1, pw))


def _route(sb, s):
    def rank_in_group(vals):
        ranks = []
        for i, vi in enumerate(vals):
            r = jnp.zeros(vi.shape, jnp.int32)
            for j, vj in enumerate(vals):
                if j == i:
                    continue
                ahead = (vj >= vi) if j < i else (vj > vi)
                r = r + jnp.where(ahead, 1, 0)
            ranks.append(r)
        return ranks

    best = None
    for g in range(N_GROUPS):
        vals = sb[g * EXPERTS_PER_GROUP:(g + 1) * EXPERTS_PER_GROUP]
        svals = s[g * EXPERTS_PER_GROUP:(g + 1) * EXPERTS_PER_GROUP]
        score = None
        for i in range(EXPERTS_PER_GROUP):
            for j in range(i + 1, EXPERTS_PER_GROUP):
                pair = vals[i] + vals[j]
                score = pair if score is None else jnp.maximum(score, pair)
        ranks = rank_in_group(vals)
        e1 = jnp.zeros(score.shape, jnp.int32)
        e2 = jnp.zeros(score.shape, jnp.int32)
        w1 = jnp.zeros(score.shape, F32)
        w2 = jnp.zeros(score.shape, F32)
        for i in range(EXPERTS_PER_GROUP):
            e1 = jnp.where(ranks[i] == 0, g * EXPERTS_PER_GROUP + i, e1)
            e2 = jnp.where(ranks[i] == 1, g * EXPERTS_PER_GROUP + i, e2)
            w1 = jnp.where(ranks[i] == 0, svals[i], w1)
            w2 = jnp.where(ranks[i] == 1, svals[i], w2)
        cand = (score, e1, e2, w1, w2)
        if best is None:
            best = cand
        else:
            take = cand[0] > best[0]
            best = tuple(jnp.where(take, c, b) for c, b in zip(cand, best))
    _, e1, e2, w1, w2 = best
    tot = w1 + w2
    return e1, e2, w1 / tot, w2 / tot


def _outproj_kernel(xc_ref, xx_ref, pc_ref, px_ref, ac_ref, ax_ref, gate_ref, scale_ref, shift_ref, gain_ref,
                    wo_ref, wrt_ref, rb_ref, before_ref, below_ref, x1_ref, xs_ref, slot_ref, wt_ref, len_ref,
                    *, n_ctx_tiles, half):
    is_ctx = pl.program_id(0) < n_ctx_tiles
    pool = jnp.where(is_ctx, pc_ref[...], px_ref[...])
    att = jnp.where(is_ctx, ac_ref[...], ax_ref[...])
    y = _dot(pool, wo_ref[0:half, :]) + _dot(att, wo_ref[half:, :])
    x1 = jnp.where(is_ctx, xc_ref[...], xx_ref[...]) + gate_ref[...] * y
    x1_ref[...] = x1
    h2 = _norm_mod(x1, gain_ref[...], scale_ref[...], shift_ref[...])
    logits = _dot3_nt(wrt_ref[...], h2)
    s = jax.nn.sigmoid(logits)
    sb = s + rb_ref[...]
    rows = lambda a: [a[i:i + 1, :] for i in range(N_EXPERTS)]
    e1, e2, w1, w2 = _route(rows(sb), rows(s))
    wt_ref[0:1, :] = w1
    wt_ref[1:2, :] = w2
    ie = lax.broadcasted_iota(jnp.int32, logits.shape, 0)
    oh1 = jnp.where(ie == e1, 1.0, 0.0)
    oh2 = jnp.where(ie == e2, 1.0, 0.0)
    c1 = jnp.sum(oh1, axis=1, keepdims=True)
    c2 = jnp.sum(oh2, axis=1, keepdims=True)
    seg_len = jnp.ceil((c1 + c2) * (1.0 / SEG_ALIGN)) * SEG_ALIGN
    seg_len_l = jnp.broadcast_to(seg_len, len_ref.shape)
    seg_off = _dot(below_ref[...], seg_len_l.astype(BF16))[:, 0:1]
    before1 = _dot(oh1.astype(BF16), before_ref[...]) + seg_off
    before2 = _dot(oh2.astype(BF16), before_ref[...]) + (seg_off + c1)
    slot1 = jnp.sum(oh1 * before1, axis=0, keepdims=True).astype(jnp.int32)
    slot2 = jnp.sum(oh2 * before2, axis=0, keepdims=True).astype(jnp.int32)
    slot_ref[0:1, :] = slot1
    slot_ref[1:2, :] = slot2
    len_ref[...] = seg_len_l.astype(jnp.int32)
    si = lax.broadcasted_iota(jnp.int32, (xs_ref.shape[0], slot1.shape[1]), 0)
    perm = jnp.where(si == slot1, 1.0, 0.0) + jnp.where(si == slot2, 1.0, 0.0)
    xs_ref[...] = _dot(perm.astype(BF16), h2.astype(BF16)).astype(BF16)


def _outproj(rows, pool_c, pool_x, att_c, att_x, modl, gain2, w_out, w_router_t, router_bias,
             *, tile, mod_row, first_tile, n_tiles):
    d = rows[0].shape[1]
    row_specs, row_args, _ = _row_specs(rows, tile, first_tile)
    half = pool_c.shape[1]
    n_ctx_tiles = pool_c.shape[0] // tile
    cidx = lambda i: (jnp.minimum(i + first_tile, n_ctx_tiles - 1), 0)
    xidx = lambda i: (jnp.maximum(i + first_tile - n_ctx_tiles, 0), 0)
    mod_spec = lambda m: pl.BlockSpec((None, None, 1, d), lambda i: (mod_row(i + first_tile), m, 0, 0))
    full = lambda a: pl.BlockSpec(a.shape, lambda i: (0,) * a.ndim)
    tok = lambda w: pl.BlockSpec((tile, w), lambda i: (i, 0))
    lane = pl.BlockSpec((2, tile), lambda i: (0, i))
    t_out = n_tiles * tile
    before = jnp.asarray(np.triu(np.ones((tile, tile), np.float32), 1), BF16)
    below = jnp.asarray(np.tril(np.ones((N_EXPERTS, N_EXPERTS), np.float32), -1), BF16)
    xs_rows = _xs_rows(tile)
    return pl.pallas_call(
        functools.partial(_outproj_kernel, n_ctx_tiles=n_ctx_tiles - first_tile, half=half),
        grid=(n_tiles,),
        in_specs=row_specs + [
            pl.BlockSpec((tile, half), cidx), pl.BlockSpec((tile, half), xidx),
            pl.BlockSpec((tile, half), cidx), pl.BlockSpec((tile, half), xidx),
            mod_spec(2), mod_spec(4), mod_spec(3), full(gain2), full(w_out), full(w_router_t),
            full(router_bias), full(before), full(below),
        ],
        out_specs=[tok(d), pl.BlockSpec((xs_rows, d), lambda i: (i, 0)), lane, lane,
                   pl.BlockSpec((None, N_EXPERTS, 128), lambda i: (i, 0, 0))],
        out_shape=[
            jax.ShapeDtypeStruct((t_out, d), F32),
            jax.ShapeDtypeStruct((n_tiles * xs_rows, d), BF16),
            jax.ShapeDtypeStruct((2, t_out), jnp.int32),
            jax.ShapeDtypeStruct((2, t_out), F32),
            jax.ShapeDtypeStruct((n_tiles, N_EXPERTS, 128), jnp.int32),
        ],
        compiler_params=_cparams(1),
        name="outproj_router",
    )(*row_args, pool_c, pool_x, att_c, att_x, modl, modl, modl, gain2, w_out, w_router_t, router_bias,
      before, below)


def _xs_rows(tile):
    return 2 * tile + N_EXPERTS * SEG_ALIGN


def _for_each_unit(n, fn):
    def body(u, carry):
        fn(u * SEG_ALIGN)
        return carry
    lax.fori_loop(0, n // SEG_ALIGN, body, 0)


def _expert_kernel(be_ref, nu_ref, usrc_ref, nval_ref, xs_ref, w1_ref, w3_ref, w2_ref, yb_ref,
                   xbuf, wb1, wb3, wb2, sem):
    i = pl.program_id(0)
    n_used = nu_ref[0]
    units = MOE_BM // SEG_ALIGN

    def fetch(blk, half, action):
        def body(u, carry):
            src = pl.multiple_of(usrc_ref[blk * units + u], SEG_ALIGN)
            dst = pl.multiple_of(u * SEG_ALIGN, SEG_ALIGN)
            action(pltpu.make_async_copy(xs_ref.at[pl.ds(src, SEG_ALIGN), :],
                                         xbuf.at[half, pl.ds(dst, SEG_ALIGN), :], sem.at[half]))
            return carry
        lax.fori_loop(0, units, body, 0, unroll=8)

    half = i % 2
    pl.when(i == 0)(lambda: fetch(i, 0, lambda c: c.start()))
    pl.when(i + 1 < n_used)(lambda: fetch(i + 1, 1 - half, lambda c: c.start()))
    used = i < n_used
    new_expert = (i == 0) | (be_ref[i] != be_ref[jnp.maximum(i - 1, 0)])

    @pl.when(used & new_expert)
    def _():
        for src, dst in ((w1_ref, wb1), (w3_ref, wb3), (w2_ref, wb2)):
            step = src.shape[0] // 4
            for r in range(0, src.shape[0], step):
                dst[r:r + step, :] = src[r:r + step, :].astype(BF16)

    @pl.when(used)
    def _():
        fetch(i, half, lambda c: c.wait())
        def groups(n_live):
            for g, r in enumerate(range(0, MOE_BM, EXPERT_ROWS)):
                if g < n_live:
                    x = xbuf[half, r:r + EXPERT_ROWS, :]
                    a = _dot(x, wb1[...])
                    b = _dot(x, wb3[...])
                    y = _dot((_silu(a) * b).astype(BF16), wb2[...]).astype(BF16)
                else:
                    y = jnp.zeros((EXPERT_ROWS, yb_ref.shape[1]), yb_ref.dtype)
                yb_ref[r:r + EXPERT_ROWS, :] = y

        n_groups = MOE_BM // EXPERT_ROWS
        live = (nval_ref[i] + EXPERT_ROWS - 1) // EXPERT_ROWS
        for n_live in range(1, n_groups + 1):
            pl.when(live == n_live)(functools.partial(groups, n_live))

    @pl.when(jnp.logical_not(used))
    def _():
        yb_ref[...] = jnp.zeros_like(yb_ref)


def _experts(xs, unit_src, block_e, n_used, n_valid, w1, w3, w2, layer, n_slots):
    d = xs.shape[1]
    de = w1.shape[3]
    nb = n_slots // MOE_BM
    wspec = lambda a: pl.BlockSpec((None, None) + a.shape[2:], lambda i, be, *_: (layer, be[i], 0, 0))
    return pl.pallas_call(
        _expert_kernel,
        grid_spec=pltpu.PrefetchScalarGridSpec(
            num_scalar_prefetch=4,
            grid=(nb,),
            in_specs=[pl.BlockSpec(memory_space=pl.ANY), wspec(w1), wspec(w3), wspec(w2)],
            out_specs=pl.BlockSpec((MOE_BM, d), lambda i, *_: (i, 0)),
            scratch_shapes=[pltpu.VMEM((2, MOE_BM, d), xs.dtype), pltpu.VMEM((d, de), BF16),
                            pltpu.VMEM((d, de), BF16), pltpu.VMEM((de, d), BF16),
                            pltpu.SemaphoreType.DMA((2,))],
        ),
        out_shape=jax.ShapeDtypeStruct((n_slots, d), BF16),
        compiler_params=_cparams(1),
        name="moe_experts",
    )(block_e, n_used, unit_src.astype(jnp.int32), n_valid.astype(jnp.int32), xs, w1, w3, w2)


def _combine_kernel(seg_off, src_row, seg_len, x_ref, slot_ref, wt_ref, gate_ref, gainf_ref, yb_ref, o_ref,
                    buf, sem, *, final_norm):
    i = pl.program_id(0)
    n = pl.num_programs(0)

    def fetch(tile_idx, half, action):
        for e in range(N_EXPERTS):
            j = tile_idx * N_EXPERTS + e

            def unit(offset, j=j):
                s0 = pl.multiple_of(src_row[j] + offset, SEG_ALIGN)
                d0 = pl.multiple_of(seg_off[j] + offset, SEG_ALIGN)
                action(pltpu.make_async_copy(yb_ref.at[pl.ds(s0, SEG_ALIGN), :],
                                             buf.at[half, pl.ds(d0, SEG_ALIGN), :], sem.at[half]))
            _for_each_unit(seg_len[j], unit)

    @pl.when(i == 0)
    def _():
        buf[...] = jnp.zeros_like(buf)
        fetch(i, 0, lambda c: c.start())

    half = i % 2
    pl.when(i + 1 < n)(lambda: fetch(i + 1, 1 - half, lambda c: c.start()))
    fetch(i, half, lambda c: c.wait())
    for rows in _row_groups(o_ref.shape[0], groups=4):
        slot = slot_ref[rows, :]
        wt = wt_ref[rows, :]
        li = lax.broadcasted_iota(jnp.int32, (slot.shape[0], buf.shape[1]), 1)
        mix = (jnp.where(li == slot[:, 0:1], wt[:, 0:1], 0.0)
               + jnp.where(li == slot[:, 1:2], wt[:, 1:2], 0.0))
        y = _dot(mix.astype(BF16), buf[half])
        x2 = x_ref[rows, :] + gate_ref[...] * y
        if final_norm:
            x2 = x2 * lax.rsqrt(jnp.mean(x2 * x2, axis=-1, keepdims=True) + EPS) * gainf_ref[...]
        o_ref[rows, :] = x2


def _combine(x1, yb, seg_off, src_row, seg_len, slot_cols, wt_cols, modl, gain_f, *, tile, mod_row,
             first_tile, final_norm):
    t, d = x1.shape
    flat = lambda a: a.reshape(-1).astype(jnp.int32)
    return pl.pallas_call(
        functools.partial(_combine_kernel, final_norm=final_norm),
        grid_spec=pltpu.PrefetchScalarGridSpec(
            num_scalar_prefetch=3,
            grid=(t // tile,),
            in_specs=[
                pl.BlockSpec((tile, d), lambda i, *_: (i, 0)),
                pl.BlockSpec((tile, 2), lambda i, *_: (i, 0)),
                pl.BlockSpec((tile, 2), lambda i, *_: (i, 0)),
                pl.BlockSpec((None, None, 1, d), lambda i, *_: (mod_row(i + first_tile), 5, 0, 0)),
                pl.BlockSpec((1, d), lambda i, *_: (0, 0)),
                pl.BlockSpec(memory_space=pl.ANY),
            ],
            out_specs=pl.BlockSpec((tile, d), lambda i, *_: (i, 0)),
            scratch_shapes=[pltpu.VMEM((2, _xs_rows(tile), d), yb.dtype), pltpu.SemaphoreType.DMA((2,))],
        ),
        out_shape=jax.ShapeDtypeStruct((t, d), F32),
        compiler_params=_cparams(1),
        name="moe_combine",
    )(flat(seg_off), flat(src_row), flat(seg_len), x1, slot_cols, wt_cols, modl, gain_f, yb)


def _slot_plan(seg_len, *, tile, n_slots):
    n_tiles = seg_len.shape[0]
    xs_rows = _xs_rows(tile)
    nb = n_slots // MOE_BM
    seg_off = jnp.cumsum(seg_len, axis=1) - seg_len
    rows_e = jnp.sum(seg_len, axis=0)
    padded = (rows_e + MOE_BM - 1) // MOE_BM * MOE_BM
    pend = jnp.cumsum(padded)
    pstart = pend - padded
    xb_row = pstart[None, :] + jnp.cumsum(seg_len, axis=0) - seg_len
    n_used = pend[-1] // MOE_BM
    blk = jnp.minimum(jnp.arange(nb, dtype=jnp.int32), n_used - 1)
    block_e = jnp.sum((pend[None, :] <= (blk * MOE_BM)[:, None]).astype(jnp.int32), axis=1)
    block_e = jnp.minimum(block_e, N_EXPERTS - 1).astype(jnp.int32)
    starts = xb_row.T.reshape(1, -1)
    lens = seg_len.T.reshape(1, -1)
    srcs = (jnp.arange(n_tiles, dtype=jnp.int32)[:, None] * xs_rows + seg_off).T.reshape(1, -1)
    unit_row = (jnp.arange(n_slots // SEG_ALIGN, dtype=jnp.int32) * SEG_ALIGN)[:, None]
    inside = (unit_row >= starts) & (unit_row < starts + lens)
    unit_src = jnp.sum(jnp.where(inside, srcs + unit_row - starts, 0), axis=1)
    unit_src = jnp.where(jnp.any(inside, axis=1), unit_src, xs_rows - SEG_ALIGN)
    own = block_e[:, None] == jnp.arange(N_EXPERTS, dtype=jnp.int32)[None, :]
    n_valid = jnp.clip(jnp.sum(jnp.where(own, (pstart + rows_e)[None, :], 0), axis=1) - blk * MOE_BM,
                       0, MOE_BM)
    return seg_off, xb_row, unit_src, block_e, n_used.astype(jnp.int32).reshape(1), n_valid


def kernel(x, c, ctx, c_ctx, w_mod, b_mod, norm1, norm2, w_in, w_gk2, b_gk, w_pool, pool_scale, gla_gain,
           w_out, w_router, router_bias, w1, w3, w2, norm_f):
    batch, seq, d = x.shape
    ctx_len = ctx.shape[1]
    depth = w_mod.shape[0]
    n_ctx = batch * ctx_len
    t_all = n_ctx + batch * seq
    def row_tile(limit):
        return max(t for t in (1024, 512, 256) if t <= limit and n_ctx % t == 0 and seq % t == 0)

    tile_a = row_tile(1024)
    tile_b = row_tile(512)

    def mod_row_for(tile):
        nct, per_b = n_ctx // tile, seq // tile
        return lambda i: jnp.where(i < nct, 0, 1 + (i - nct) // per_b)

    rows = (ctx.reshape(n_ctx, d), x.reshape(batch * seq, d), 0)

    cvecs = jnp.zeros((16, d), F32).at[0].set(c_ctx).at[1:1 + batch].set(c)
    mods = _modulation(cvecs, w_mod, b_mod).reshape(depth, 16, N_MOD, 1, d)

    w2g = jnp.zeros((depth, 128, 2 * GLA_QK), F32)
    w2g = w2g.at[:, 0:GATE_RANK, 0:GLA_QK].set(w_gk2[:, 0])
    w2g = w2g.at[:, GATE_RANK:2 * GATE_RANK, GLA_QK:].set(w_gk2[:, 1])
    bgk = b_gk.reshape(depth, 1, 2 * GLA_QK)
    w_out_b = w_out.astype(BF16)
    w_pool_b = w_pool.astype(BF16)
    w_router_t = w_router.T
    rbias = router_bias.reshape(N_EXPERTS, 1)
    g1 = lambda a: a.reshape(1, -1)

    out = None
    for l in range(depth):
        last = l == depth - 1
        pbf, lg = _inproj(rows, t_all, mods[l], g1(norm1[l]), w_in, l, w2g[l], bgk[l],
                          tile=tile_a, mod_row=mod_row_for(tile_a))
        att_c, att_x = _gla(pbf, lg, gla_gain[l], batch=batch, ctx_len=ctx_len, seq=seq)
        pool_c, pool_x = _pool(pbf, w_pool_b[l], pool_scale[l], batch=batch, ctx_len=ctx_len, seq=seq)
        first_tile = n_ctx // tile_b if last else 0
        n_tiles = t_all // tile_b - first_tile
        x1, xs, slots, wts, seg_len = _outproj(
            rows, pool_c, pool_x, att_c, att_x, mods[l], g1(norm2[l]), w_out_b[l], w_router_t, rbias,
            tile=tile_b, mod_row=mod_row_for(tile_b), first_tile=first_tile, n_tiles=n_tiles)
        seg_len = seg_len[:, :, 0]
        max_rows = n_tiles * (2 * tile_b + N_EXPERTS * (SEG_ALIGN - 1)) + N_EXPERTS * (MOE_BM - 1)
        n_slots = -(-max_rows // MOE_BM) * MOE_BM
        seg_off, xb_row, unit_src, block_e, n_used, n_valid = _slot_plan(
            seg_len, tile=tile_b, n_slots=n_slots)
        yb = _experts(xs, unit_src, block_e, n_used, n_valid, w1, w3, w2, l, n_slots)
        res = _combine(x1, yb, seg_off, xb_row, seg_len, slots.T, wts.T, mods[l], g1(norm_f), tile=tile_b,
                       mod_row=mod_row_for(tile_b), first_tile=first_tile, final_norm=last)
        if last:
            out = res
        else:
            rows = (res, res, n_ctx)
    return out.reshape(batch, seq, d)
```

```python
import functools

import numpy as np
import jax
import jax.numpy as jnp
from jax import lax
from jax.experimental import pallas as pl
from jax.experimental.pallas import tpu as pltpu

F32 = jnp.float32
BF16 = jnp.bfloat16

GRID_W = 64
POOL_GROUPS = 4
POOL_WINDOWS = (2, 4, 8, 16)
GLA_HEADS = 4
GLA_DK = 64
GLA_DV = 128
GLA_QK = GLA_HEADS * GLA_DK
GATE_RANK = 16
GATE_NORM = 16.0
N_EXPERTS = 16
N_GROUPS = 4
EXPERTS_PER_GROUP = 4
EPS = 1e-6
N_MOD = 6

VMEM_LIMIT_BYTES = 56 * 1024 * 1024
SPAN = 256
SUB = 32
N_SUB = SPAN // SUB
LOG2_E = 1.4426950408889634
EXP2_CLAMP = 115.0
BIG = 2 * SUB
N_BIG = SPAN // BIG
CROSS0 = 256
GLA_KW = 2 * CROSS0
MOE_BM = 512
EXPERT_ROWS = 256
SEG_ALIGN = 16


def _cparams(n_axes):
    return pltpu.CompilerParams(
        dimension_semantics=("arbitrary",) * n_axes, vmem_limit_bytes=VMEM_LIMIT_BYTES)


def _split(a):
    hi = a.astype(BF16)
    lo = (a - hi.astype(F32)).astype(BF16)
    return hi, lo


def _dot(a, b):
    return jnp.dot(a, b, preferred_element_type=F32)


def _dot_nt(a, b):
    return lax.dot_general(a, b, (((1,), (1,)), ((), ())), preferred_element_type=F32)


def _dot_tn(a, b):
    return lax.dot_general(a, b, (((0,), (0,)), ((), ())), preferred_element_type=F32)


def _dot3(a, b):
    ah, al = _split(a)
    bh, bl = _split(b)
    return _dot(ah, bh) + (_dot(ah, bl) + _dot(al, bh))


def _dot3_nt(a, b):
    ah, al = _split(a)
    bh, bl = _split(b)
    return _dot_nt(ah, bh) + (_dot_nt(ah, bl) + _dot_nt(al, bh))


def _dot_exact_lhs(m_bf16, b, terms=3):
    out = None
    rest = b
    for _ in range(terms):
        part = rest.astype(BF16)
        rest = rest - part.astype(F32)
        prod = _dot(m_bf16, part)
        out = prod if out is None else out + prod
    return out


def _silu(x):
    return x * jax.nn.sigmoid(x)


def _mod_kernel(c_ref, w_ref, b_ref, o_ref):
    o_ref[0] = _dot3(_silu(c_ref[...]), w_ref[0]) + b_ref[0]


def _modulation(cvecs, w_mod, b_mod):
    depth, d, _ = w_mod.shape
    rows = cvecs.shape[0]
    return pl.pallas_call(
        _mod_kernel,
        grid=(depth, N_MOD),
        in_specs=[
            pl.BlockSpec((rows, d), lambda l, j: (0, 0)),
            pl.BlockSpec((1, d, d), lambda l, j: (l, 0, j)),
            pl.BlockSpec((1, 1, d), lambda l, j: (l, 0, j)),
        ],
        out_specs=pl.BlockSpec((1, rows, d), lambda l, j: (l, 0, j)),
        out_shape=jax.ShapeDtypeStruct((depth, rows, N_MOD * d), F32),
        compiler_params=_cparams(2),
        name="adaln_vectors",
    )(cvecs, w_mod, b_mod.reshape(depth, 1, N_MOD * d))


def _norm_mod(x, gain, scale, shift):
    y = x * lax.rsqrt(jnp.mean(x * x, axis=-1, keepdims=True) + EPS)
    return (y * gain) * (1.0 + scale) + shift


def _row_specs(rows, tile, first_tile=0):
    xc, xx, first_latent_row = rows
    d = xc.shape[1]
    x_off = first_latent_row // tile
    nct = (xc.shape[0] // tile) if first_latent_row == 0 else x_off
    cspec = pl.BlockSpec((tile, d), lambda i, *_: (jnp.minimum(i + first_tile, nct - 1), 0))
    xspec = pl.BlockSpec((tile, d), lambda i, *_: (jnp.maximum(i + first_tile - nct, 0) + x_off, 0))
    return [cspec, xspec], [xc, xx], nct - first_tile


def _row_groups(n_rows, groups=2):
    size = n_rows // groups
    return [slice(g * size, (g + 1) * size) for g in range(groups)]


def _inproj_kernel(xc_ref, xx_ref, scale_ref, shift_ref, gain_ref, win_ref, w2_ref, bgk_ref,
                   p_ref, lg_ref, w_ref, wr_ref, *, n_chunk, n_ctx_tiles):
    n_main = p_ref.shape[1]
    width = n_main // n_chunk

    @pl.when(pl.program_id(0) == 0)
    def _():
        for n in range(n_chunk):
            cols = slice(n * width, (n + 1) * width)
            w_ref[:, cols] = win_ref[:, cols].astype(BF16)
        wr_ref[...] = jnp.zeros_like(wr_ref)
        wr_ref[:, 0:2 * GATE_RANK] = win_ref[:, n_main:].astype(BF16)

    is_ctx = pl.program_id(0) < n_ctx_tiles
    for rows in _row_groups(p_ref.shape[0], groups=4):
        x = jnp.where(is_ctx, xc_ref[rows, :], xx_ref[rows, :])
        h = _norm_mod(x, gain_ref[...], scale_ref[...], shift_ref[...])
        hb = h.astype(BF16)
        for n in range(n_chunk):
            cols = slice(n * width, (n + 1) * width)
            p_ref[rows, cols] = _dot(hb, w_ref[:, cols]).astype(BF16)
        r = _dot(hb, wr_ref[...])
        z = _dot3(r, w2_ref[...]) + bgk_ref[...]
        log_sig = jnp.minimum(z, 0.0) - jnp.log(1.0 + jnp.exp(-jnp.abs(z)))
        lg_ref[rows, :] = log_sig * (LOG2_E / GATE_NORM)


def _inproj(rows, t, modl, gain, w_in_all, layer, w2, bgk, *, tile, mod_row):
    d = rows[0].shape[1]
    n_main = w_in_all.shape[2] - 2 * GATE_RANK
    n_gate = w2.shape[1]
    mod_spec = lambda m: pl.BlockSpec((None, None, 1, d), lambda i: (mod_row(i), m, 0, 0))
    full = lambda a: pl.BlockSpec(a.shape, lambda i: (0,) * a.ndim)
    row_specs, row_args, n_ctx_tiles = _row_specs(rows, tile)
    return pl.pallas_call(
        functools.partial(_inproj_kernel, n_chunk=4, n_ctx_tiles=n_ctx_tiles),
        grid=(t // tile,),
        in_specs=row_specs + [
            mod_spec(1), mod_spec(0), full(gain),
            pl.BlockSpec((None,) + w_in_all.shape[1:], lambda i: (layer, 0, 0)), full(w2), full(bgk),
        ],
        out_specs=[
            pl.BlockSpec((tile, n_main), lambda i: (i, 0)),
            pl.BlockSpec((tile, n_gate), lambda i: (i, 0)),
        ],
        out_shape=[
            jax.ShapeDtypeStruct((t, n_main), BF16),
            jax.ShapeDtypeStruct((t, n_gate), F32),
        ],
        scratch_shapes=[pltpu.VMEM((d, n_main), BF16), pltpu.VMEM((d, 128), BF16)],
        compiler_params=_cparams(1),
        name="norm_inproj",
    )(*row_args, modl, modl, gain, w_in_all, w2, bgk)


def _store_heads(dst_ref, rows, col0, val):
    half = (col0 // GLA_DK) & 1
    plain = val.astype(BF16)
    turned = pltpu.roll(val, GLA_DK, axis=1).astype(BF16)
    for h in range(GLA_HEADS):
        if (h & 1) == half:
            src = plain[:, h * GLA_DK:(h + 1) * GLA_DK]
        else:
            hh = (h + 1) % GLA_HEADS
            src = turned[:, hh * GLA_DK:(hh + 1) * GLA_DK]
        dst_ref[h, rows, col0:col0 + GLA_DK] = src


def _gla_span(q_ref, k_ref, v_ref, lg_ref, o_ref, row0, st_ref, qh_ref, kh_ref, b_ref, qf_ref, kf_ref,
              tri_ref, dmask_ref, smask_ref, *, rev, assign):
    rows = pl.ds(row0, SPAN)
    dcol = slice(rev * GLA_QK, (rev + 1) * GLA_QK)
    b_ref[...] = _dot_exact_lhs(tri_ref[rev], lg_ref[rows, dcol])
    qf_ref[...] = q_ref[rows, :].astype(F32) * (GLA_DK ** -0.5)
    kf_ref[...] = k_ref[rows, :].astype(F32)
    e_tot = b_ref[0:1, :] if rev else b_ref[SPAN - 1:SPAN, :]

    def slab(col0, q_rows, k_rows, ref_row, clamp=None):
        e_r = b_ref[ref_row:ref_row + 1, :]
        dq = b_ref[q_rows, :] - e_r
        dk = e_r - b_ref[k_rows, :]
        if clamp is not None:
            dq, dk = jnp.minimum(dq, clamp), jnp.minimum(dk, clamp)
        _store_heads(qh_ref, q_rows, col0, qf_ref[q_rows, :] * jnp.exp2(dq))
        _store_heads(kh_ref, k_rows, col0, kf_ref[k_rows, :] * jnp.exp2(dk))

    for s in range(N_BIG - 1):
        j = s + 1 if rev else s
        blk = slice(j * BIG, (j + 1) * BIG)
        beyond = slice(0, j * BIG) if rev else slice((j + 1) * BIG, SPAN)
        slab(CROSS0 + s * GLA_DK, beyond, blk, j * BIG if rev else j * BIG + BIG - 1)
    for i in range(N_SUB):
        blk = slice(i * SUB, (i + 1) * SUB)
        slab((i & 1) * GLA_DK, blk, blk, i * SUB + SUB // 2, clamp=EXP2_CLAMP)
    for g in range(N_BIG):
        first = slice(g * BIG, g * BIG + SUB)
        second = slice(g * BIG + SUB, (g + 1) * BIG)
        if rev:
            slab(2 * GLA_DK, first, second, g * BIG + SUB)
        else:
            slab(2 * GLA_DK, second, first, g * BIG + SUB - 1)

    st = st_ref[...]
    qd = (qf_ref[...] * jnp.exp2(b_ref[...])).astype(BF16)
    o_state = _dot_nt(qd, st.astype(BF16))
    in_block = dmask_ref[rev] != 0.0
    for h in range(GLA_HEADS):
        vc = slice(h * GLA_DV, (h + 1) * GLA_DV)
        a_diag = _dot_nt(qh_ref[h, :, 0:CROSS0], kh_ref[h, :, 0:CROSS0])
        a_cross = _dot_nt(qh_ref[h, :, CROSS0:], kh_ref[h, :, CROSS0:])
        a = (a_cross + jnp.where(in_block, a_diag, 0.0)).astype(BF16)
        o_h = _dot(a, v_ref[rows, vc]) + o_state[:, vc]
        if assign:
            o_ref[rows, vc] = o_h
        else:
            o_ref[rows, vc] += o_h
    kd =(kf_ref[...] * jnp.exp2(e_tot - b_ref[...])).astype(BF16)
    upd = _dot_tn(v_ref[rows, :], kd)
    st_ref[...] = st * jnp.exp2(e_tot) + upd * smask_ref[...]


def _gla_finish(o_ref, g_ref, gain, out_ref, n_rows):
    def body(c, carry):
        rows = pl.ds(pl.multiple_of(c * SPAN, SPAN), SPAN)
        for h in range(GLA_HEADS):
            vc = slice(h * GLA_DV, (h + 1) * GLA_DV)
            o = o_ref[rows, vc]
            y = o * lax.rsqrt(jnp.mean(o * o, axis=-1, keepdims=True) + EPS) * gain
            g = g_ref[rows, vc].astype(F32)
            out_ref[rows, vc] = (y * _silu(g)).astype(BF16)
        return carry
    lax.fori_loop(0, n_rows // SPAN, body, 0)


def _gla_kernel(qc, kc, vc, gc, lgc, qx, kx, vx, gx, lgx, gain_ref, tri, dmask, smask, outc, outx,
                oc, ox, *per_direction):
    n_c = qc.shape[0] // SPAN
    n_x = qx.shape[0] // SPAN
    names = ("st_ref", "qh_ref", "kh_ref", "b_ref", "qf_ref", "kf_ref")
    scratch = [dict(zip(names, per_direction[r * len(names):(r + 1) * len(names)])) for r in (0, 1)]
    for ref in tuple(s[n] for s in scratch for n in ("st_ref", "qh_ref", "kh_ref")):
        ref[...] = jnp.zeros_like(ref)
    spans = [functools.partial(_gla_span, tri_ref=tri, dmask_ref=dmask, smask_ref=smask, rev=rev,
                               **scratch[rev]) for rev in (0, 1)]

    def step(refs, n, s, fwd_first, bwd_first):
        row = lambda idx: idx * SPAN if isinstance(idx, int) else pl.multiple_of(idx * SPAN, SPAN)
        spans[0](*refs, row(s), assign=fwd_first)
        spans[1](*refs, row(n - 1 - s), assign=bwd_first)

    def segment(refs, n):
        both_first = range(0, n // 2)
        both_second = range(n - n // 2, n)
        loop = lambda rng, first: lax.fori_loop(
            rng.start, rng.stop, lambda s, c: (step(refs, n, s, first, first), c)[1], 0)
        loop(both_first, True)
        if n % 2:
            step(refs, n, n // 2, True, False)
        loop(both_second, False)

    segment((qc, kc, vc, lgc, oc), n_c)
    segment((qx, kx, vx, lgx, ox), n_x)
    gain = gain_ref[...]
    _gla_finish(oc, gc, gain, outc, qc.shape[0])
    _gla_finish(ox, gx, gain, outx, qx.shape[0])


def _gla(pbf, lg, gla_gain, *, batch, ctx_len, seq):
    nxb = (batch * ctx_len) // seq
    assert nxb * seq == batch * ctx_len
    hv = GLA_HEADS * GLA_DV
    cspec = lambda w, j: pl.BlockSpec((ctx_len, w), lambda b: (b, j))
    xspec = lambda w, j: pl.BlockSpec((seq, w), lambda b: (nxb + b, j))
    full = lambda a: pl.BlockSpec(a.shape, lambda b: (0,) * a.ndim)
    pos = np.arange(SPAN)
    lower = pos[None, :] <= pos[:, None]
    same = (pos[None, :] // BIG) == (pos[:, None] // BIG)
    tri = jnp.asarray(np.stack([lower, lower.T]).astype(np.float32), BF16)
    dmask = jnp.asarray(np.stack([lower & same, lower.T & same]).astype(np.float32))
    smask = jnp.asarray(((np.arange(hv)[:, None] // GLA_DV) == (np.arange(GLA_QK)[None, :] // GLA_DK))
                        .astype(np.float32))
    return pl.pallas_call(
        _gla_kernel,
        grid=(batch,),
        in_specs=[
            cspec(GLA_QK, 2), cspec(GLA_QK, 3), cspec(hv, 2), cspec(hv, 3), cspec(2 * GLA_QK, 0),
            xspec(GLA_QK, 2), xspec(GLA_QK, 3), xspec(hv, 2), xspec(hv, 3), xspec(2 * GLA_QK, 0),
            pl.BlockSpec((1, GLA_DV), lambda b: (0, 0)), full(tri), full(dmask), full(smask),
        ],
        out_specs=[
            pl.BlockSpec((ctx_len, hv), lambda b: (b, 0)),
            pl.BlockSpec((seq, hv), lambda b: (b, 0)),
        ],
        out_shape=[
            jax.ShapeDtypeStruct((batch * ctx_len, hv), BF16),
            jax.ShapeDtypeStruct((batch * seq, hv), BF16),
        ],
        scratch_shapes=[
            pltpu.VMEM((ctx_len, hv), F32),
            pltpu.VMEM((seq, hv), F32),
        ] + 2 * [
            pltpu.VMEM((hv, GLA_QK), F32),
            pltpu.VMEM((GLA_HEADS, SPAN, GLA_KW), BF16),
            pltpu.VMEM((GLA_HEADS, SPAN, GLA_KW), BF16),
            pltpu.VMEM((SPAN, GLA_QK), F32),
            pltpu.VMEM((SPAN, GLA_QK), F32),
            pltpu.VMEM((SPAN, GLA_QK), F32),
        ],
        compiler_params=_cparams(1),
        name="gla_bidirectional",
    )(pbf, pbf, pbf, pbf, lg, pbf, pbf, pbf, pbf, lg, gla_gain.reshape(1, GLA_DV), tri, dmask, smask)


def _window_offsets(w):
    return range(-(w // 2), w - w // 2)


def _band(w, period):
    ri = lax.broadcasted_iota(jnp.int32, (SPAN, SPAN), 0)
    ci = lax.broadcasted_iota(jnp.int32, (SPAN, SPAN), 1)
    d = ci - ri
    shift = period.bit_length() - 1
    assert period == 1 << shift
    same_row = (ri >> shift) == (ci >> shift)
    inside = jnp.where(d >= -(w // 2), jnp.where(d <= w - w // 2 - 1, 1.0, 0.0), 0.0)
    return jnp.where(same_row, inside, 0.0).astype(BF16)


def _pool_kernel(uc_ref, ux_ref, invc_ref, invx_ref, wp_ref, ps_ref, outc_ref, outx_ref, pad_ref,
                 *, pad_rows):
    n_x = ux_ref.shape[0]
    gw = wp_ref.shape[1]
    pad_ref[0:pad_rows, :] = jnp.zeros((pad_rows, gw), F32)
    pad_ref[pad_rows + n_x:, :] = jnp.zeros((pad_rows, gw), F32)
    for g, w in enumerate(POOL_WINDOWS):
        gc = slice(g * gw, (g + 1) * gw)
        wp = wp_ref[g]
        ps = ps_ref[:, gc]
        band_c = _band(w, SPAN)
        for t in range(uc_ref.shape[0] // SPAN):
            rows = slice(t * SPAN, (t + 1) * SPAN)
            u = uc_ref[rows, gc].astype(F32)
            pooled = _dot_exact_lhs(band_c, u, terms=2) * invc_ref[g, rows, :] - u
            outc_ref[rows, gc] = (_dot(pooled.astype(BF16), wp) * ps).astype(BF16)
        pad_ref[pad_rows:pad_rows + n_x, :] = ux_ref[:, gc].astype(F32)
        band_x = _band(w, GRID_W)

        def body(t, carry):
            r0 = pl.multiple_of(t * SPAN, SPAN)
            acc = jnp.zeros((SPAN, gw), F32)
            for dr in _window_offsets(w):
                acc = acc + pad_ref[pl.ds(pad_rows + r0 + dr * GRID_W, SPAN), :]
            rows = pl.ds(r0, SPAN)
            u = pad_ref[pl.ds(pad_rows + r0, SPAN), :]
            pooled = _dot_exact_lhs(band_x, acc, terms=2) * invx_ref[g, rows, :] - u
            outx_ref[rows, gc] = (_dot(pooled.astype(BF16), wp) * ps).astype(BF16)
            return carry
        lax.fori_loop(0, n_x // SPAN, body, 0, unroll=8)


def _inv_counts(length, rows_of):
    out = []
    for w in POOL_WINDOWS:
        cnt = np.ones((length,), np.float64)
        for axis_len, coord in rows_of(length):
            lo = np.clip(coord - w // 2, 0, axis_len)
            hi = np.clip(coord + w - w // 2, 0, axis_len)
            cnt = cnt * (hi - lo)
        out.append(1.0 / cnt)
    return np.broadcast_to(np.stack(out)[:, :, None], (len(POOL_WINDOWS), length, 128)).astype(np.float32)


def _pool(pbf, w_pool, pool_scale, *, batch, ctx_len, seq):
    nxb = (batch * ctx_len) // seq
    pw = w_pool.shape[0] * w_pool.shape[1]
    pad_rows = (max(POOL_WINDOWS) // 2) * GRID_W
    inv_c = jnp.asarray(_inv_counts(ctx_len, lambda n: [(n, np.arange(n))]))
    inv_x = jnp.asarray(_inv_counts(
        seq, lambda n: [(n // GRID_W, np.arange(n) // GRID_W), (GRID_W, np.arange(n) % GRID_W)]))
    full = lambda a: pl.BlockSpec(a.shape, lambda b: (0,) * a.ndim)
    return pl.pallas_call(
        functools.partial(_pool_kernel, pad_rows=pad_rows),
        grid=(batch,),
        in_specs=[
            pl.BlockSpec((ctx_len, pw), lambda b: (b, 0)),
            pl.BlockSpec((seq, pw), lambda b: (nxb + b, 0)),
            full(inv_c), full(inv_x), full(w_pool),
            pl.BlockSpec((1, pw), lambda b: (0, 0)),
        ],
        out_specs=[
            pl.BlockSpec((ctx_len, pw), lambda b: (b, 0)),
            pl.BlockSpec((seq, pw), lambda b: (b, 0)),
        ],
        out_shape=[
            jax.ShapeDtypeStruct((batch * ctx_len, pw), BF16),
            jax.ShapeDtypeStruct((batch * seq, pw), BF16),
        ],
        scratch_shapes=[pltpu.VMEM((seq + 2 * pad_rows, w_pool.shape[1]), F32)],
        compiler_params=_cparams(1),
        name="pool_mixer",
    )(pbf, pbf, inv_c, inv_x, w_pool, pool_scale.reshape(1, pw))


def _route(sb, s):
    def rank_in_group(vals):
        ranks = []
        for i, vi in enumerate(vals):
            r = jnp.zeros(vi.shape, jnp.int32)
            for j, vj in enumerate(vals):
                if j == i:
                    continue
                ahead = (vj >= vi) if j < i else (vj > vi)
                r = r + jnp.where(ahead, 1, 0)
            ranks.append(r)
        return ranks

    best = None
    for g in range(N_GROUPS):
        vals = sb[g * EXPERTS_PER_GROUP:(g + 1) * EXPERTS_PER_GROUP]
        svals = s[g * EXPERTS_PER_GROUP:(g + 1) * EXPERTS_PER_GROUP]
        score = None
        for i in range(EXPERTS_PER_GROUP):
            for j in range(i + 1, EXPERTS_PER_GROUP):
                pair = vals[i] + vals[j]
                score = pair if score is None else jnp.maximum(score, pair)
        ranks = rank_in_group(vals)
        e1 = jnp.zeros(score.shape, jnp.int32)
        e2 = jnp.zeros(score.shape, jnp.int32)
        w1 = jnp.zeros(score.shape, F32)
        w2 = jnp.zeros(score.shape, F32)
        for i in range(EXPERTS_PER_GROUP):
            e1 = jnp.where(ranks[i] == 0, g * EXPERTS_PER_GROUP + i, e1)
            e2 = jnp.where(ranks[i] == 1, g * EXPERTS_PER_GROUP + i, e2)
            w1 = jnp.where(ranks[i] == 0, svals[i], w1)
            w2 = jnp.where(ranks[i] == 1, svals[i], w2)
        cand = (score, e1, e2, w1, w2)
        if best is None:
            best = cand
        else:
            take = cand[0] > best[0]
            best = tuple(jnp.where(take, c, b) for c, b in zip(cand, best))
    _, e1, e2, w1, w2 = best
    tot = w1 + w2
    return e1, e2, w1 / tot, w2 / tot


def _outproj_kernel(xc_ref, xx_ref, pc_ref, px_ref, ac_ref, ax_ref, gate_ref, scale_ref, shift_ref, gain_ref,
                    wo_ref, wrt_ref, rb_ref, before_ref, below_ref, x1_ref, xs_ref, slot_ref, wt_ref, len_ref,
                    *, n_ctx_tiles, half):
    is_ctx = pl.program_id(0) < n_ctx_tiles
    pool = jnp.where(is_ctx, pc_ref[...], px_ref[...])
    att = jnp.where(is_ctx, ac_ref[...], ax_ref[...])
    y = _dot(pool, wo_ref[0:half, :]) + _dot(att, wo_ref[half:, :])
    x1 = jnp.where(is_ctx, xc_ref[...], xx_ref[...]) + gate_ref[...] * y
    x1_ref[...] = x1
    h2 = _norm_mod(x1, gain_ref[...], scale_ref[...], shift_ref[...])
    logits = _dot3_nt(wrt_ref[...], h2)
    s = jax.nn.sigmoid(logits)
    sb = s + rb_ref[...]
    rows = lambda a: [a[i:i + 1, :] for i in range(N_EXPERTS)]
    e1, e2, w1, w2 = _route(rows(sb), rows(s))
    wt_ref[0:1, :] = w1
    wt_ref[1:2, :] = w2
    ie = lax.broadcasted_iota(jnp.int32, logits.shape, 0)
    oh1 = jnp.where(ie == e1, 1.0, 0.0)
    oh2 = jnp.where(ie == e2, 1.0, 0.0)
    c1 = jnp.sum(oh1, axis=1, keepdims=True)
    c2 = jnp.sum(oh2, axis=1, keepdims=True)
    seg_len = jnp.ceil((c1 + c2) * (1.0 / SEG_ALIGN)) * SEG_ALIGN
    seg_len_l = jnp.broadcast_to(seg_len, len_ref.shape)
    seg_off = _dot(below_ref[...], seg_len_l.astype(BF16))[:, 0:1]
    before1 = _dot(oh1.astype(BF16), before_ref[...]) + seg_off
    before2 = _dot(oh2.astype(BF16), before_ref[...]) + (seg_off + c1)
    slot1 = jnp.sum(oh1 * before1, axis=0, keepdims=True).astype(jnp.int32)
    slot2 = jnp.sum(oh2 * before2, axis=0, keepdims=True).astype(jnp.int32)
    slot_ref[0:1, :] = slot1
    slot_ref[1:2, :] = slot2
    len_ref[...] = seg_len_l.astype(jnp.int32)
    si = lax.broadcasted_iota(jnp.int32, (xs_ref.shape[0], slot1.shape[1]), 0)
    perm = jnp.where(si == slot1, 1.0, 0.0) + jnp.where(si == slot2, 1.0, 0.0)
    xs_ref[...] = _dot(perm.astype(BF16), h2.astype(BF16)).astype(BF16)


def _outproj(rows, pool_c, pool_x, att_c, att_x, modl, gain2, w_out, w_router_t, router_bias,
             *, tile, mod_row, first_tile, n_tiles):
    d = rows[0].shape[1]
    row_specs, row_args, _ = _row_specs(rows, tile, first_tile)
    half = pool_c.shape[1]
    n_ctx_tiles = pool_c.shape[0] // tile
    cidx = lambda i: (jnp.minimum(i + first_tile, n_ctx_tiles - 1), 0)
    xidx = lambda i: (jnp.maximum(i + first_tile - n_ctx_tiles, 0), 0)
    mod_spec = lambda m: pl.BlockSpec((None, None, 1, d), lambda i: (mod_row(i + first_tile), m, 0, 0))
    full = lambda a: pl.BlockSpec(a.shape, lambda i: (0,) * a.ndim)
    tok = lambda w: pl.BlockSpec((tile, w), lambda i: (i, 0))
    lane = pl.BlockSpec((2, tile), lambda i: (0, i))
    t_out = n_tiles * tile
    before = jnp.asarray(np.triu(np.ones((tile, tile), np.float32), 1), BF16)
    below = jnp.asarray(np.tril(np.ones((N_EXPERTS, N_EXPERTS), np.float32), -1), BF16)
    xs_rows = _xs_rows(tile)
    return pl.pallas_call(
        functools.partial(_outproj_kernel, n_ctx_tiles=n_ctx_tiles - first_tile, half=half),
        grid=(n_tiles,),
        in_specs=row_specs + [
            pl.BlockSpec((tile, half), cidx), pl.BlockSpec((tile, half), xidx),
            pl.BlockSpec((tile, half), cidx), pl.BlockSpec((tile, half), xidx),
            mod_spec(2), mod_spec(4), mod_spec(3), full(gain2), full(w_out), full(w_router_t),
            full(router_bias), full(before), full(below),
        ],
        out_specs=[tok(d), pl.BlockSpec((xs_rows, d), lambda i: (i, 0)), lane, lane,
                   pl.BlockSpec((None, N_EXPERTS, 128), lambda i: (i, 0, 0))],
        out_shape=[
            jax.ShapeDtypeStruct((t_out, d), F32),
            jax.ShapeDtypeStruct((n_tiles * xs_rows, d), BF16),
            jax.ShapeDtypeStruct((2, t_out), jnp.int32),
            jax.ShapeDtypeStruct((2, t_out), F32),
            jax.ShapeDtypeStruct((n_tiles, N_EXPERTS, 128), jnp.int32),
        ],
        compiler_params=_cparams(1),
        name="outproj_router",
    )(*row_args, pool_c, pool_x, att_c, att_x, modl, modl, modl, gain2, w_out, w_router_t, router_bias,
      before, below)


def _xs_rows(tile):
    return 2 * tile + N_EXPERTS * SEG_ALIGN


def _for_each_unit(n, fn):
    def body(u, carry):
        fn(u * SEG_ALIGN)
        return carry
    lax.fori_loop(0, n // SEG_ALIGN, body, 0)


def _expert_kernel(be_ref, nu_ref, usrc_ref, nval_ref, xs_ref, w1_ref, w3_ref, w2_ref, yb_ref,
                   xbuf, wb1, wb3, wb2, sem):
    i = pl.program_id(0)
    n_used = nu_ref[0]
    units = MOE_BM // SEG_ALIGN

    def fetch(blk, half, action):
        def body(u, carry):
            src = pl.multiple_of(usrc_ref[blk * units + u], SEG_ALIGN)
            dst = pl.multiple_of(u * SEG_ALIGN, SEG_ALIGN)
            action(pltpu.make_async_copy(xs_ref.at[pl.ds(src, SEG_ALIGN), :],
                                         xbuf.at[half, pl.ds(dst, SEG_ALIGN), :], sem.at[half]))
            return carry
        lax.fori_loop(0, units, body, 0, unroll=8)

    half = i % 2
    pl.when(i == 0)(lambda: fetch(i, 0, lambda c: c.start()))
    pl.when(i + 1 < n_used)(lambda: fetch(i + 1, 1 - half, lambda c: c.start()))
    used = i < n_used
    new_expert = (i == 0) | (be_ref[i] != be_ref[jnp.maximum(i - 1, 0)])

    def weights(src, dst, cast):
        if cast:
            step = src.shape[0] // 4
            for r in range(0, src.shape[0], step):
                dst[r:r + step, :] = src[r:r + step, :].astype(BF16)
        return dst[...]

    @pl.when(used)
    def _():
        fetch(i, half, lambda c: c.wait())

        def groups(n_live, cast):
            row_groups = [slice(r, r + EXPERT_ROWS) for r in range(0, MOE_BM, EXPERT_ROWS)]
            x = [xbuf[half, rows, :] for rows in row_groups[:n_live]]
            w = weights(w1_ref, wb1, cast)
            a = [_dot(xg, w) for xg in x]
            w = weights(w3_ref, wb3, cast)
            hidden = [(_silu(ag) * _dot(xg, w)).astype(BF16) for ag, xg in zip(a, x)]
            w = weights(w2_ref, wb2, cast)
            for g, rows in enumerate(row_groups):
                if g < n_live:
                    yb_ref[rows, :] = _dot(hidden[g], w).astype(BF16)
                else:
                    yb_ref[rows, :] = jnp.zeros((EXPERT_ROWS, yb_ref.shape[1]), yb_ref.dtype)

        n_groups = MOE_BM // EXPERT_ROWS
        live = (nval_ref[i] + EXPERT_ROWS - 1) // EXPERT_ROWS
        for n_live in range(1, n_groups + 1):
            for cast in (False, True):
                first_block = new_expert if cast else jnp.logical_not(new_expert)
                pl.when((live == n_live) & first_block)(functools.partial(groups, n_live, cast))

    @pl.when(jnp.logical_not(used))
    def _():
        yb_ref[...] = jnp.zeros_like(yb_ref)


def _experts(xs, unit_src, block_e, n_used, n_valid, w1, w3, w2, layer, n_slots):
    d = xs.shape[1]
    de = w1.shape[3]
    nb = n_slots // MOE_BM
    wspec = lambda a: pl.BlockSpec((None, None) + a.shape[2:], lambda i, be, *_: (layer, be[i], 0, 0))
    return pl.pallas_call(
        _expert_kernel,
        grid_spec=pltpu.PrefetchScalarGridSpec(
            num_scalar_prefetch=4,
            grid=(nb,),
            in_specs=[pl.BlockSpec(memory_space=pl.ANY), wspec(w1), wspec(w3), wspec(w2)],
            out_specs=pl.BlockSpec((MOE_BM, d), lambda i, *_: (i, 0)),
            scratch_shapes=[pltpu.VMEM((2, MOE_BM, d), xs.dtype), pltpu.VMEM((d, de), BF16),
                            pltpu.VMEM((d, de), BF16), pltpu.VMEM((de, d), BF16),
                            pltpu.SemaphoreType.DMA((2,))],
        ),
        out_shape=jax.ShapeDtypeStruct((n_slots, d), BF16),
        compiler_params=_cparams(1),
        name="moe_experts",
    )(block_e, n_used, unit_src.astype(jnp.int32), n_valid.astype(jnp.int32), xs, w1, w3, w2)


def _combine_kernel(seg_off, src_row, seg_len, x_ref, slot_ref, wt_ref, gate_ref, gainf_ref, yb_ref, o_ref,
                    buf, sem, *, final_norm):
    i = pl.program_id(0)
    n = pl.num_programs(0)

    def fetch(tile_idx, half, action):
        for e in range(N_EXPERTS):
            j = tile_idx * N_EXPERTS + e

            def unit(offset, j=j):
                s0 = pl.multiple_of(src_row[j] + offset, SEG_ALIGN)
                d0 = pl.multiple_of(seg_off[j] + offset, SEG_ALIGN)
                action(pltpu.make_async_copy(yb_ref.at[pl.ds(s0, SEG_ALIGN), :],
                                             buf.at[half, pl.ds(d0, SEG_ALIGN), :], sem.at[half]))
            _for_each_unit(seg_len[j], unit)

    @pl.when(i == 0)
    def _():
        buf[...] = jnp.zeros_like(buf)
        fetch(i, 0, lambda c: c.start())

    half = i % 2
    pl.when(i + 1 < n)(lambda: fetch(i + 1, 1 - half, lambda c: c.start()))
    fetch(i, half, lambda c: c.wait())
    for rows in _row_groups(o_ref.shape[0], groups=4):
        slot = slot_ref[rows, :]
        wt = wt_ref[rows, :]
        li = lax.broadcasted_iota(jnp.int32, (slot.shape[0], buf.shape[1]), 1)
        mix = (jnp.where(li == slot[:, 0:1], wt[:, 0:1], 0.0)
               + jnp.where(li == slot[:, 1:2], wt[:, 1:2], 0.0))
        y = _dot(mix.astype(BF16), buf[half])
        x2 = x_ref[rows, :] + gate_ref[...] * y
        if final_norm:
            x2 = x2 * lax.rsqrt(jnp.mean(x2 * x2, axis=-1, keepdims=True) + EPS) * gainf_ref[...]
        o_ref[rows, :] = x2


def _combine(x1, yb, seg_off, src_row, seg_len, slot_cols, wt_cols, modl, gain_f, *, tile, mod_row,
             first_tile, final_norm):
    t, d = x1.shape
    flat = lambda a: a.reshape(-1).astype(jnp.int32)
    return pl.pallas_call(
        functools.partial(_combine_kernel, final_norm=final_norm),
        grid_spec=pltpu.PrefetchScalarGridSpec(
            num_scalar_prefetch=3,
            grid=(t // tile,),
            in_specs=[
                pl.BlockSpec((tile, d), lambda i, *_: (i, 0)),
                pl.BlockSpec((tile, 2), lambda i, *_: (i, 0)),
                pl.BlockSpec((tile, 2), lambda i, *_: (i, 0)),
                pl.BlockSpec((None, None, 1, d), lambda i, *_: (mod_row(i + first_tile), 5, 0, 0)),
                pl.BlockSpec((1, d), lambda i, *_: (0, 0)),
                pl.BlockSpec(memory_space=pl.ANY),
            ],
            out_specs=pl.BlockSpec((tile, d), lambda i, *_: (i, 0)),
            scratch_shapes=[pltpu.VMEM((2, _xs_rows(tile), d), yb.dtype), pltpu.SemaphoreType.DMA((2,))],
        ),
        out_shape=jax.ShapeDtypeStruct((t, d), F32),
        compiler_params=_cparams(1),
        name="moe_combine",
    )(flat(seg_off), flat(src_row), flat(seg_len), x1, slot_cols, wt_cols, modl, gain_f, yb)


def _slot_plan(seg_len, *, tile, n_slots):
    n_tiles = seg_len.shape[0]
    xs_rows = _xs_rows(tile)
    nb = n_slots // MOE_BM
    seg_off = jnp.cumsum(seg_len, axis=1) - seg_len
    rows_e = jnp.sum(seg_len, axis=0)
    padded = (rows_e + MOE_BM - 1) // MOE_BM * MOE_BM
    pend = jnp.cumsum(padded)
    pstart = pend - padded
    xb_row = pstart[None, :] + jnp.cumsum(seg_len, axis=0) - seg_len
    n_used = pend[-1] // MOE_BM
    blk = jnp.minimum(jnp.arange(nb, dtype=jnp.int32), n_used - 1)
    block_e = jnp.sum((pend[None, :] <= (blk * MOE_BM)[:, None]).astype(jnp.int32), axis=1)
    block_e = jnp.minimum(block_e, N_EXPERTS - 1).astype(jnp.int32)
    starts = xb_row.T.reshape(1, -1)
    lens = seg_len.T.reshape(1, -1)
    srcs = (jnp.arange(n_tiles, dtype=jnp.int32)[:, None] * xs_rows + seg_off).T.reshape(1, -1)
    unit_row = (jnp.arange(n_slots // SEG_ALIGN, dtype=jnp.int32) * SEG_ALIGN)[:, None]
    inside = (unit_row >= starts) & (unit_row < starts + lens)
    unit_src = jnp.sum(jnp.where(inside, srcs + unit_row - starts, 0), axis=1)
    unit_src = jnp.where(jnp.any(inside, axis=1), unit_src, xs_rows - SEG_ALIGN)
    own = block_e[:, None] == jnp.arange(N_EXPERTS, dtype=jnp.int32)[None, :]
    n_valid = jnp.clip(jnp.sum(jnp.where(own, (pstart + rows_e)[None, :], 0), axis=1) - blk * MOE_BM,
                       0, MOE_BM)
    return seg_off, xb_row, unit_src, block_e, n_used.astype(jnp.int32).reshape(1), n_valid


def kernel(x, c, ctx, c_ctx, w_mod, b_mod, norm1, norm2, w_in, w_gk2, b_gk, w_pool, pool_scale, gla_gain,
           w_out, w_router, router_bias, w1, w3, w2, norm_f):
    batch, seq, d = x.shape
    ctx_len = ctx.shape[1]
    depth = w_mod.shape[0]
    n_ctx = batch * ctx_len
    t_all = n_ctx + batch * seq
    def row_tile(limit):
        return max(t for t in (1024, 512, 256) if t <= limit and n_ctx % t == 0 and seq % t == 0)

    tile_a = row_tile(1024)
    tile_b = row_tile(512)

    def mod_row_for(tile):
        nct, per_b = n_ctx // tile, seq // tile
        return lambda i: jnp.where(i < nct, 0, 1 + (i - nct) // per_b)

    rows = (ctx.reshape(n_ctx, d), x.reshape(batch * seq, d), 0)

    cvecs = jnp.zeros((16, d), F32).at[0].set(c_ctx).at[1:1 + batch].set(c)
    mods = _modulation(cvecs, w_mod, b_mod).reshape(depth, 16, N_MOD, 1, d)

    w2g = jnp.zeros((depth, 128, 2 * GLA_QK), F32)
    w2g = w2g.at[:, 0:GATE_RANK, 0:GLA_QK].set(w_gk2[:, 0])
    w2g = w2g.at[:, GATE_RANK:2 * GATE_RANK, GLA_QK:].set(w_gk2[:, 1])
    bgk = b_gk.reshape(depth, 1, 2 * GLA_QK)
    w_out_b = w_out.astype(BF16)
    w_pool_b = w_pool.astype(BF16)
    w_router_t = w_router.T
    rbias = router_bias.reshape(N_EXPERTS, 1)
    g1 = lambda a: a.reshape(1, -1)

    out = None
    for l in range(depth):
        last = l == depth - 1
        pbf, lg = _inproj(rows, t_all, mods[l], g1(norm1[l]), w_in, l, w2g[l], bgk[l],
                          tile=tile_a, mod_row=mod_row_for(tile_a))
        att_c, att_x = _gla(pbf, lg, gla_gain[l], batch=batch, ctx_len=ctx_len, seq=seq)
        pool_c, pool_x = _pool(pbf, w_pool_b[l], pool_scale[l], batch=batch, ctx_len=ctx_len, seq=seq)
        first_tile = n_ctx // tile_b if last else 0
        n_tiles = t_all // tile_b - first_tile
        x1, xs, slots, wts, seg_len = _outproj(
            rows, pool_c, pool_x, att_c, att_x, mods[l], g1(norm2[l]), w_out_b[l], w_router_t, rbias,
            tile=tile_b, mod_row=mod_row_for(tile_b), first_tile=first_tile, n_tiles=n_tiles)
        seg_len = seg_len[:, :, 0]
        max_rows = n_tiles * (2 * tile_b + N_EXPERTS * (SEG_ALIGN - 1)) + N_EXPERTS * (MOE_BM - 1)
        n_slots = -(-max_rows // MOE_BM) * MOE_BM
        seg_off, xb_row, unit_src, block_e, n_used, n_valid = _slot_plan(
            seg_len, tile=tile_b, n_slots=n_slots)
        yb = _experts(xs, unit_src, block_e, n_used, n_valid, w1, w3, w2, l, n_slots)
        res = _combine(x1, yb, seg_off, xb_row, seg_len, slots.T, wts.T, mods[l], g1(norm_f), tile=tile_b,
                       mod_row=mod_row_for(tile_b), first_tile=first_tile, final_norm=last)
        if last:
            out = res
        else:
            rows = (res, res, n_ctx)
    return out.reshape(batch, seq, d)
```
